```python
import math
import jax, jax.numpy as jnp
from jax import lax
import numpy as np

D_MODEL = 1024
BATCH = 8
SEQ = 2048
DEPTH = 4

GRID_W = 64
CTX_LEN = 256
ROPE_BASE = 10000.0
LN_EPS = 1e-5
DEEPNORM_ALPHA = (2 * DEPTH) ** 0.25
DEEPNORM_BETA = (8 * DEPTH) ** -0.25
N_EVEN = (DEPTH + 1) // 2
N_ODD = DEPTH // 2
H_A = 4
DK_A = D_MODEL // 8
DV_A = D_MODEL // 8
CHUNK_A = 128
RET_EXP_FWD = 5.0
RET_EXP_BWD = 5.5
H_B = 4
DK_B = D_MODEL // 16
DV_B = D_MODEL // 8
GLA_RANK = 16
GLA_TAU = 16.0
CHUNK_B = 64
AB_SPLITS = (H_A * DK_A, H_A * DK_A, H_A * DV_A, H_A * DV_A,
             H_B * DK_B, H_B * DK_B, H_B * DV_B, H_B * DV_B, GLA_RANK, GLA_RANK)
AB_IN = sum(AB_SPLITS)
AB_OUT = H_A * DV_A + H_B * DV_B
H_C = 8
DH_C = D_MODEL // 16
DV_C = 2 * DH_C
Q_BLOCK = 128
C_SPLITS = (H_C * 2 * DH_C, H_C * 2 * DH_C, H_C * DV_C)
C_IN = sum(C_SPLITS)
C_OUT = H_C * DV_C
LAMBDA_STD = 0.1
N_GROUPS = 4
EXPERTS_PER_GROUP = 8
N_EXPERTS = N_GROUPS * EXPERTS_PER_GROUP
TOP_K_EXPERT = 2
D_EXPERT = D_MODEL // 2
MOE_BLOCK = 128

kernel_name = 'hybrid_retention_gla_diffattn_hmoe_dit'


def layer_norm(x, g, b):
    xf = x.astype(jnp.float32)
    mu = jnp.mean(xf, axis=-1, keepdims=True)
    var = jnp.mean(jnp.square(xf - mu), axis=-1, keepdims=True)
    return ((xf - mu) * lax.rsqrt(var + LN_EPS) * g + b).astype(x.dtype)


def head_norm(o, center):
    of = o.astype(jnp.float32)
    if center:
        of = of - jnp.mean(of, axis=-1, keepdims=True)
    return of * lax.rsqrt(jnp.mean(jnp.square(of), axis=-1, keepdims=True) + LN_EPS)


def split_sizes(t, sizes):
    cuts = [int(s) for s in np.cumsum(sizes)[:-1]]
    return jnp.split(t, cuts, axis=-1)


def axial_rope(rows, head_dim):
    row = jnp.repeat(jnp.arange(rows, dtype=jnp.float32), GRID_W)
    col = jnp.tile(jnp.arange(GRID_W, dtype=jnp.float32), rows)
    quarter = head_dim // 4
    inv = ROPE_BASE ** (-jnp.arange(quarter, dtype=jnp.float32) / quarter)
    ang_r = row[:, None] * inv
    ang_c = col[:, None] * inv
    ang = jnp.concatenate([ang_r, ang_r, ang_c, ang_c], axis=-1)
    return jnp.cos(ang), jnp.sin(ang)


def rope(x, cos, sin):
    a, b, c, d = jnp.split(x, 4, axis=-1)
    rot = jnp.concatenate([-b, a, -d, c], axis=-1)
    return (x * cos + rot * sin).astype(x.dtype)


def chunk_scan(q, k, v, log_a, s0, chunk, inclusive):
    bn, h, t, dk = q.shape
    dv = v.shape[-1]
    n = t // chunk

    def to_chunks(z):
        return jnp.moveaxis(z.reshape(bn, h, n, chunk, z.shape[-1]), 2, 0)

    mask = jnp.tril(jnp.ones((chunk, chunk), dtype=bool), 0 if inclusive else -1)

    def step(s, inp):
        qi, ki, vi, li = inp
        b = jnp.cumsum(li, axis=-2)
        bq = b if inclusive else b - li
        btot = b[..., -1:, :]
        qd = qi * jnp.exp(bq)
        kd = ki * jnp.exp(-b)
        att = jnp.where(mask, jnp.einsum('bhid,bhjd->bhij', qd, kd), 0.0)
        o = jnp.einsum('bhij,bhje->bhie', att, vi) + jnp.einsum('bhid,bhde->bhie', qd, s)
        kt = ki * jnp.exp(btot - b)
        s = jnp.exp(btot)[..., 0, :, None] * s + jnp.einsum('bhjd,bhje->bhde', kt, vi)
        return s, o

    s_fin, o = lax.scan(step, s0, (to_chunks(q), to_chunks(k), to_chunks(v), to_chunks(log_a)))
    return jnp.moveaxis(o, 0, 2).reshape(bn, h, t, dv), s_fin


def bidir_scan(q, k, v, la_f, la_b, s0_f, s0_b, chunk):
    q, k, v = (z.astype(jnp.float32) for z in (q, k, v))
    o_f, s_f = chunk_scan(q, k, v, la_f, s0_f, chunk, True)
    flip = lambda z: jnp.flip(z, axis=2)
    o_b, s_b = chunk_scan(flip(q), flip(k), flip(v), flip(la_b), s0_b, chunk, False)
    return o_f + flip(o_b), s_f, s_b


def ret_logdecay(exp0, like):
    gam = jnp.log1p(-jnp.exp2(-(exp0 + jnp.arange(H_A, dtype=jnp.float32))))
    return jnp.broadcast_to(gam[None, :, None, None], like.shape)


def ab_heads(z, w_in, w_lr_f, b_lr_f, w_lr_b, b_lr_b, cs):
    bn, t, _ = z.shape
    qa, ka, va, ga, qb, kb, vb, gb, lr_f, lr_b = split_sizes(z @ w_in, AB_SPLITS)
    qa = qa.reshape(bn, t, H_A, DK_A)
    ka = ka.reshape(bn, t, H_A, DK_A)
    if cs is not None:
        qa = rope(qa, *cs)
        ka = rope(ka, *cs)
    qa = qa * (DK_A ** -0.5)
    va = va.reshape(bn, t, H_A, DV_A)
    qb = qb.reshape(bn, t, H_B, DK_B) * (DK_B ** -0.5)
    kb = kb.reshape(bn, t, H_B, DK_B)
    vb = vb.reshape(bn, t, H_B, DV_B)

    def gla_logdecay(lr, w, b):
        g = (lr @ w + b).astype(jnp.float32)
        return (jax.nn.log_sigmoid(g) / GLA_TAU).reshape(bn, t, H_B, DK_B)

    la_f = gla_logdecay(lr_f, w_lr_f, b_lr_f)
    la_b = gla_logdecay(lr_b, w_lr_b, b_lr_b)
    bhtd = lambda u: jnp.swapaxes(u, 1, 2)
    return (bhtd(qa), bhtd(ka), bhtd(va), ga, bhtd(qb), bhtd(kb), bhtd(vb), gb, bhtd(la_f), bhtd(la_b))


def mixer_ab(ux, uc, w_in, w_lr_f, b_lr_f, w_lr_b, b_lr_b, gn_a, gn_b, w_out, cs, need_ctx):
    fx = ab_heads(ux, w_in, w_lr_f, b_lr_f, w_lr_b, b_lr_b, cs)
    fc = ab_heads(uc, w_in, w_lr_f, b_lr_f, w_lr_b, b_lr_b, None)
    bn = ux.shape[0]
    za = jnp.zeros((bn, H_A, DK_A, DV_A), jnp.float32)
    zb = jnp.zeros((bn, H_B, DK_B, DV_B), jnp.float32)

    def mix(f, st):
        qa, ka, va, ga, qb, kb, vb, gb, laf, lab = f
        oa, saf, sab = bidir_scan(qa, ka, va, ret_logdecay(RET_EXP_FWD, qa),
                                  ret_logdecay(RET_EXP_BWD, qa), st[0], st[1], CHUNK_A)
        ob, sbf, sbb = bidir_scan(qb, kb, vb, laf, lab, st[2], st[3], CHUNK_B)
        return (oa, ga, ob, gb), (saf, sab, sbf, sbb)

    out_c, st_c = mix(fc, (za, za, zb, zb))
    out_x, _ = mix(fx, st_c)

    def merge(oa, ga, ob, gb):
        b_, _, t_, _ = oa.shape
        ya = jnp.swapaxes(head_norm(oa, True), 1, 2).reshape(b_, t_, H_A * DV_A) * gn_a
        yb = jnp.swapaxes(head_norm(ob, False), 1, 2).reshape(b_, t_, H_B * DV_B) * gn_b
        y = jnp.concatenate([jax.nn.silu(ga.astype(jnp.float32)) * ya,
                             jax.nn.silu(gb.astype(jnp.float32)) * yb], axis=-1)
        return y.astype(ux.dtype) @ w_out

    yx = merge(*out_x)
    yc = merge(*out_c) if need_ctx else None
    return yx, yc


def c_heads(z, w_qkv, cs):
    bn, t, _ = z.shape
    q, k, v = split_sizes(z @ w_qkv, C_SPLITS)
    q = q.reshape(bn, t, H_C, 2, DH_C)
    k = k.reshape(bn, t, H_C, 2, DH_C)
    if cs is not None:
        q = rope(q, *cs)
        k = rope(k, *cs)
    return q * (DH_C ** -0.5), k, v.reshape(bn, t, H_C, DV_C)


def diff_attend(q, k, v, lam):
    s = jnp.einsum('bqhcd,bkhcd->bchqk', q, k).astype(jnp.float32)
    p = jax.nn.softmax(s, axis=-1)
    a = p[:, 0] - lam[:, None, None] * p[:, 1]
    return jnp.einsum('bhqk,bkhe->bqhe', a.astype(v.dtype), v)


def mixer_c(ux, uc, w_qkv, lq1, lk1, lq2, lk2, subln_g, w_out, lam_init, cs, need_ctx):
    qx, kx, vx = c_heads(ux, w_qkv, cs)
    qc, kc, vc = c_heads(uc, w_qkv, None)
    lam = (jnp.exp(jnp.sum(lq1 * lk1, axis=-1)) - jnp.exp(jnp.sum(lq2 * lk2, axis=-1))).astype(jnp.float32) + lam_init
    k_all = jnp.concatenate([kc, kx], axis=1)
    v_all = jnp.concatenate([vc, vx], axis=1)
    bn, t = qx.shape[:2]
    qb = jnp.moveaxis(qx.reshape(bn, t // Q_BLOCK, Q_BLOCK, H_C, 2, DH_C), 1, 0)
    ox = lax.map(lambda qi: diff_attend(qi, k_all, v_all, lam), qb)
    ox = jnp.moveaxis(ox, 0, 1).reshape(bn, t, H_C, DV_C)

    def finish(o):
        b_, t_ = o.shape[:2]
        y = head_norm(o, False).reshape(b_, t_, C_OUT) * subln_g * (1.0 - lam_init)
        return y.astype(ux.dtype) @ w_out

    yx = finish(ox)
    yc = finish(diff_attend(qc, kc, vc, lam)) if need_ctx else None
    return yx, yc


def hier_moe(z, w_grp, b_grp, w_rexp, b_rexp, w_gate, w_up, w_down):
    n, d = z.shape
    zf = z.astype(jnp.float32)
    g_logits = zf @ w_grp.astype(jnp.float32) + b_grp
    g_prob = jax.nn.softmax(g_logits, axis=-1)
    grp = jnp.argmax(g_logits, axis=-1).astype(jnp.int32)
    g_oh = jax.nn.one_hot(grp, N_GROUPS, dtype=jnp.float32)
    g_w = jnp.sum(g_prob * g_oh, axis=-1)
    e_logits = (zf @ w_rexp.astype(jnp.float32) + b_rexp).reshape(n, N_GROUPS, EXPERTS_PER_GROUP)
    e_sel = jnp.einsum('nge,ng->ne', e_logits, g_oh)
    top_v, top_i = lax.top_k(e_sel, TOP_K_EXPERT)
    comb = g_w[:, None] * jax.nn.softmax(top_v, axis=-1)
    eid = grp[:, None] * EXPERTS_PER_GROUP + top_i.astype(jnp.int32)
    na = n * TOP_K_EXPERT
    e_flat = eid.reshape(-1)
    t_flat = jnp.repeat(jnp.arange(n, dtype=jnp.int32), TOP_K_EXPERT)
    w_flat = comb.reshape(-1)
    order = jnp.argsort(e_flat)
    e_sorted = e_flat[order]
    counts = jnp.bincount(e_flat, length=N_EXPERTS)
    padded = (counts + MOE_BLOCK - 1) // MOE_BLOCK * MOE_BLOCK
    pad_end = jnp.cumsum(padded)
    pad_start = pad_end - padded
    raw_start = jnp.cumsum(counts) - counts
    dest = pad_start[e_sorted] + jnp.arange(na) - raw_start[e_sorted]
    p_rows = -(-na // MOE_BLOCK) * MOE_BLOCK + N_EXPERTS * MOE_BLOCK
    n_blocks = p_rows // MOE_BLOCK
    row_tok = jnp.full((p_rows,), n, jnp.int32).at[dest].set(t_flat[order])
    row_w = jnp.zeros((p_rows,), jnp.float32).at[dest].set(w_flat[order])
    blk_exp = jnp.minimum(jnp.searchsorted(pad_end, jnp.arange(n_blocks) * MOE_BLOCK, side='right'),
                          N_EXPERTS - 1)
    z_pad = jnp.concatenate([z, jnp.zeros((1, d), z.dtype)], axis=0)
    xs = z_pad[row_tok].reshape(n_blocks, MOE_BLOCK, d)

    def run(args):
        xb, e = args
        h = jax.nn.silu(xb @ w_gate[e]) * (xb @ w_up[e])
        return h @ w_down[e]

    ys = lax.map(run, (xs, blk_exp)).reshape(p_rows, d)
    out = jnp.zeros((n + 1, d), z.dtype).at[row_tok].add(ys * row_w[:, None].astype(z.dtype))
    return out[:n]


def setup_inputs(seed: int = 0) -> dict:
    key = jax.random.key(seed)
    ks = iter(jax.random.split(key, 40))
    nrm = lambda shape, scale: jax.random.normal(next(ks), shape, jnp.float32) * scale
    gain = lambda shape: 1.0 + nrm(shape, 0.05)
    d = D_MODEL
    return {
        'x': nrm((BATCH, SEQ, d), 1.0),
        'c': nrm((BATCH, d), 1.0),
        'ctx': nrm((BATCH, CTX_LEN, d), 1.0),
        'c_ctx': nrm((d,), 1.0),
        'ada_w': nrm((DEPTH, d, 6 * d), 0.5 * d ** -0.5),
        'ada_b': nrm((DEPTH, 6 * d), 0.01),
        'ln1_g': gain((DEPTH, d)),
        'ln1_b': nrm((DEPTH, d), 0.01),
        'ln2_g': gain((DEPTH, d)),
        'ln2_b': nrm((DEPTH, d), 0.01),
        'ab_w_in': nrm((N_EVEN, d, AB_IN), d ** -0.5),
        'ab_w_lr_f': nrm((N_EVEN, GLA_RANK, H_B * DK_B), GLA_RANK ** -0.5),
        'ab_b_lr_f': nrm((N_EVEN, H_B * DK_B), 0.01),
        'ab_w_lr_b': nrm((N_EVEN, GLA_RANK, H_B * DK_B), GLA_RANK ** -0.5),
        'ab_b_lr_b': nrm((N_EVEN, H_B * DK_B), 0.01),
        'ab_gn_a': gain((N_EVEN, H_A * DV_A)),
        'ab_gn_b': gain((N_EVEN, H_B * DV_B)),
        'ab_w_out': nrm((N_EVEN, AB_OUT, d), DEEPNORM_BETA * AB_OUT ** -0.5),
        'c_w_qkv': nrm((N_ODD, d, C_IN), d ** -0.5),
        'c_lq1': nrm((N_ODD, H_C, DH_C), LAMBDA_STD),
        'c_lk1': nrm((N_ODD, H_C, DH_C), LAMBDA_STD),
        'c_lq2': nrm((N_ODD, H_C, DH_C), LAMBDA_STD),
        'c_lk2': nrm((N_ODD, H_C, DH_C), LAMBDA_STD),
        'c_subln_g': gain((N_ODD, C_OUT)),
        'c_w_out': nrm((N_ODD, C_OUT, d), DEEPNORM_BETA * C_OUT ** -0.5),
        'moe_w_grp': nrm((DEPTH, d, N_GROUPS), d ** -0.5),
        'moe_b_grp': nrm((DEPTH, N_GROUPS), 0.01),
        'moe_w_rexp': nrm((DEPTH, d, N_EXPERTS), d ** -0.5),
        'moe_b_rexp': nrm((DEPTH, N_EXPERTS), 0.01),
        'moe_w_gate': nrm((DEPTH, N_EXPERTS, d, D_EXPERT), d ** -0.5),
        'moe_w_up': nrm((DEPTH, N_EXPERTS, d, D_EXPERT), d ** -0.5),
        'moe_w_down': nrm((DEPTH, N_EXPERTS, D_EXPERT, d), DEEPNORM_BETA * D_EXPERT ** -0.5),
    }


def reference(x, c, ctx, c_ctx, ada_w, ada_b, ln1_g, ln1_b, ln2_g, ln2_b,
              ab_w_in, ab_w_lr_f, ab_b_lr_f, ab_w_lr_b, ab_b_lr_b, ab_gn_a, ab_gn_b, ab_w_out,
              c_w_qkv, c_lq1, c_lk1, c_lq2, c_lk2, c_subln_g, c_w_out,
              moe_w_grp, moe_b_grp, moe_w_rexp, moe_b_rexp, moe_w_gate, moe_w_up, moe_w_down):
    bn, t, d = x.shape
    n_ctx = ctx.shape[1]
    rows = t // GRID_W
    cos_a, sin_a = axial_rope(rows, DK_A)
    cs_a = (cos_a[:, None, :], sin_a[:, None, :])
    cos_c, sin_c = axial_rope(rows, DH_C)
    cs_c = (cos_c[:, None, None, :], sin_c[:, None, None, :])
    sc = jax.nn.silu(c)
    scc = jax.nn.silu(c_ctx)
    for l in range(DEPTH):
        last = l == DEPTH - 1
        i = l // 2
        sh1x, s1x, g1x, sh2x, s2x, g2x = jnp.split(sc @ ada_w[l] + ada_b[l], 6, axis=-1)
        sh1c, s1c, g1c, sh2c, s2c, g2c = jnp.split(scc @ ada_w[l] + ada_b[l], 6, axis=-1)
        ux = x * (1.0 + s1x[:, None, :]) + sh1x[:, None, :]
        uc = ctx * (1.0 + s1c) + sh1c
        if l % 2 == 0:
            yx, yc = mixer_ab(ux, uc, ab_w_in[i], ab_w_lr_f[i], ab_b_lr_f[i], ab_w_lr_b[i], ab_b_lr_b[i],
                              ab_gn_a[i], ab_gn_b[i], ab_w_out[i], cs_a, not last)
        else:
            lam_init = 0.8 - 0.6 * math.exp(-0.3 * l)
            yx, yc = mixer_c(ux, uc, c_w_qkv[i], c_lq1[i], c_lk1[i], c_lq2[i], c_lk2[i], c_subln_g[i],
                             c_w_out[i], lam_init, cs_c, not last)
        x = layer_norm(DEEPNORM_ALPHA * x + g1x[:, None, :] * yx, ln1_g[l], ln1_b[l])
        if not last:
            ctx = layer_norm(DEEPNORM_ALPHA * ctx + g1c * yc, ln1_g[l], ln1_b[l])
        ux = x * (1.0 + s2x[:, None, :]) + sh2x[:, None, :]
        tok = ux.reshape(bn * t, d)
        if not last:
            uc = ctx * (1.0 + s2c) + sh2c
            tok = jnp.concatenate([tok, uc.reshape(bn * n_ctx, d)], axis=0)
        y = hier_moe(tok, moe_w_grp[l], moe_b_grp[l], moe_w_rexp[l], moe_b_rexp[l],
                     moe_w_gate[l], moe_w_up[l], moe_w_down[l])
        x = layer_norm(DEEPNORM_ALPHA * x + g2x[:, None, :] * y[:bn * t].reshape(bn, t, d), ln2_g[l], ln2_b[l])
        if not last:
            ctx = layer_norm(DEEPNORM_ALPHA * ctx + g2c * y[bn * t:].reshape(bn, n_ctx, d), ln2_g[l], ln2_b[l])
    return x
```

```python
import functools
import math

import numpy as np
import jax
import jax.numpy as jnp
from jax import lax
from jax.experimental import pallas as pl
from jax.experimental.pallas import tpu as pltpu

F32 = jnp.float32
BF16 = jnp.bfloat16
HIGHEST = lax.Precision.HIGHEST

D_MODEL = 1024
BATCH = 8
SEQ = 2048
DEPTH = 4
GRID_W = 64
CTX_LEN = 256
ROPE_BASE = 10000.0
LN_EPS = 1e-5
DEEPNORM_ALPHA = (2 * DEPTH) ** 0.25
H_A = 4
DK_A = 128
DV_A = 128
CHUNK_A = 128
RET_EXP_FWD = 5.0
RET_EXP_BWD = 5.5
H_B = 4
DK_B = 64
DV_B = 128
GLA_RANK = 16
GLA_TAU = 16.0
CHUNK_B = 64
H_C = 8
DH_C = 64
DV_C = 128
N_GROUPS = 4
EXPERTS_PER_GROUP = 8
N_EXPERTS = 32
D_EXPERT = 512

LANES = 128
T_ALL = CTX_LEN + SEQ
N_ALL = BATCH * T_ALL
TM = 256
TILES_PER_BATCH = T_ALL // TM
LATENT_TILES_PER_BATCH = SEQ // TM
AB_COLS = 29 * LANES
MOE_ROWS = 256
TQ = 256
VMEM_LIMIT = 56 * 1024 * 1024


def _cparams(sem):
    return pltpu.CompilerParams(dimension_semantics=sem, vmem_limit_bytes=VMEM_LIMIT)


def _silu(v):
    return v * (1.0 / (1.0 + jnp.exp(-v)))


def _row_tile(latent_only):
    if latent_only:
        return lambda i: (i // LATENT_TILES_PER_BATCH) * TILES_PER_BATCH + 1 + i % LATENT_TILES_PER_BATCH
    return lambda i: i


def _mod_row(latent_only):
    if latent_only:
        return lambda i: i // LATENT_TILES_PER_BATCH
    return lambda i: jnp.where(i % TILES_PER_BATCH == 0, BATCH, i // TILES_PER_BATCH)


def _ada_kernel(c_ref, w_ref, b_ref, o_ref):
    sc = _silu(c_ref[...])
    o_ref[0] = jnp.dot(sc.astype(BF16), w_ref[0].astype(BF16), preferred_element_type=F32) + b_ref[0]


def _ada_tables(c_all, ada_w, ada_b):
    tn = 1536
    n_out = 6 * D_MODEL
    return pl.pallas_call(
        _ada_kernel,
        grid=(DEPTH, n_out // tn),
        in_specs=[
            pl.BlockSpec((16, D_MODEL), lambda l, j: (0, 0)),
            pl.BlockSpec((1, D_MODEL, tn), lambda l, j: (l, 0, j)),
            pl.BlockSpec((1, 1, tn), lambda l, j: (l, 0, j)),
        ],
        out_specs=pl.BlockSpec((1, 16, tn), lambda l, j: (l, 0, j)),
        out_shape=jax.ShapeDtypeStruct((DEPTH, 16, n_out), F32),
        compiler_params=_cparams(("arbitrary", "arbitrary")),
        name="ada_tables",
    )(c_all, ada_w, ada_b.reshape(DEPTH, 1, n_out))


def _modmm_kernel(x_ref, mod_ref, w_ref, o_ref):
    u = x_ref[...] * (1.0 + mod_ref[0, 1:2, :]) + mod_ref[0, 0:1, :]
    o_ref[...] = jnp.dot(u.astype(BF16), w_ref[...], preferred_element_type=F32)


def _mod_matmul(x, mod_l, w_bf16):
    n_out = w_bf16.shape[1]
    return pl.pallas_call(
        _modmm_kernel,
        grid=(N_ALL // TM,),
        in_specs=[
            pl.BlockSpec((TM, D_MODEL), lambda i: (i, 0)),
            pl.BlockSpec((1, 6, D_MODEL), lambda i: (_mod_row(False)(i), 0, 0)),
            pl.BlockSpec((D_MODEL, n_out), lambda i: (0, 0)),
        ],
        out_specs=pl.BlockSpec((TM, n_out), lambda i: (i, 0)),
        out_shape=jax.ShapeDtypeStruct((N_ALL, n_out), F32),
        compiler_params=_cparams(("arbitrary",)),
        name="mod_matmul",
    )(x, mod_l, w_bf16)


def _rope_tables(head_dim, reps):
    rows = SEQ // GRID_W
    row = np.repeat(np.arange(rows, dtype=np.float32), GRID_W)
    col = np.tile(np.arange(GRID_W, dtype=np.float32), rows)
    quarter = head_dim // 4
    inv = (ROPE_BASE ** (-np.arange(quarter, dtype=np.float32) / quarter)).astype(np.float32)
    ang_r = row[:, None] * inv
    ang_c = col[:, None] * inv
    ang = np.concatenate([ang_r, ang_r, ang_c, ang_c], axis=-1)
    cos = np.cos(ang).astype(np.float32)
    sin = np.sin(ang).astype(np.float32)
    q_idx = (np.arange(head_dim) // quarter) % 2
    sin_up = np.where(q_idx == 1, sin, 0.0).astype(np.float32)
    sin_dn = np.where(q_idx == 0, -sin, 0.0).astype(np.float32)

    def full(t, ctx_val):
        t = np.tile(t, (1, reps))
        return jnp.asarray(np.concatenate([np.full((CTX_LEN, t.shape[1]), ctx_val, np.float32), t], axis=0))

    return full(cos, 1.0), full(sin_up, 0.0), full(sin_dn, 0.0)


def _rope(x, cos, sin_up, sin_dn, quarter):
    width = x.shape[-1]
    return x * cos + pltpu.roll(x, quarter, 1) * sin_up + pltpu.roll(x, width - quarter, 1) * sin_dn


def _dot_tb(a, b):
    return lax.dot_general(a, b, (((1,), (1,)), ((), ())), preferred_element_type=F32)


def _dot_ta(a, b, precision=None):
    return lax.dot_general(a, b, (((0,), (0,)), ((), ())), preferred_element_type=F32, precision=precision)


def _retention_tables():
    c = CHUNK_A
    i = np.arange(c, dtype=np.float64)
    out = np.zeros((H_A, 7, c, LANES), np.float64)
    for h in range(H_A):
        lgf = np.log1p(-np.exp2(-(RET_EXP_FWD + h)))
        lgb = np.log1p(-np.exp2(-(RET_EXP_BWD + h)))
        d = i[:, None] - i[None, :]
        out[h, 0] = np.where(d >= 0, np.exp(lgf * d), np.exp(lgb * (-d - 1)))
        out[h, 1] = np.exp(lgf * (i + 1))[:, None]
        out[h, 2] = np.exp(lgb * (c - 1 - i))[:, None]
        out[h, 3] = np.exp(lgf * (c - 1 - i))[:, None]
        out[h, 4] = np.exp(lgb * i)[:, None]
        out[h, 5] = np.exp(lgf * c)
        out[h, 6] = np.exp(lgb * c)
    return jnp.asarray(out.astype(np.float32))


def _scan_a_kernel(q_ref, k_ref, v_ref, g_ref, cos_ref, sup_ref, sdn_ref, dec_ref, gn_ref, o_ref,
                   kr_scr, sb_scr):
    c = CHUNK_A
    n_ctx = CTX_LEN // c
    n_all = T_ALL // c
    scale = DK_A ** -0.5
    kr_scr[...] = _rope(k_ref[...], cos_ref[...], sup_ref[...], sdn_ref[...], DK_A // 4)
    dmat = dec_ref[0, 0]
    q_f, q_b, k_f, k_b = dec_ref[0, 1], dec_ref[0, 2], dec_ref[0, 3], dec_ref[0, 4]
    g_fc, g_bc = dec_ref[0, 5], dec_ref[0, 6]
    gn = gn_ref[...]
    zero = jnp.zeros((DK_A, DV_A), F32)

    def chunk(ci):
        return pl.ds(pl.multiple_of(ci * c, c), c)

    def kv_state(ci, k_dec):
        sl = chunk(ci)
        return _dot_ta((kr_scr[sl, :] * k_dec).astype(BF16), v_ref[sl, :].astype(BF16))

    def run(lo, hi, sf0, sb0):
        def bwd(j, sb):
            ci = hi - 1 - j
            sb_scr[ci] = sb
            return g_bc * sb + kv_state(ci, k_b)

        sb_fin = lax.fori_loop(0, hi - lo, bwd, sb0)

        def fwd(j, sf):
            ci = lo + j
            sl = chunk(ci)
            q = _rope(q_ref[sl, :], cos_ref[sl, :], sup_ref[sl, :], sdn_ref[sl, :], DK_A // 4) * scale
            k = kr_scr[sl, :]
            vb = v_ref[sl, :].astype(BF16)
            att = _dot_tb(q.astype(BF16), k.astype(BF16)) * dmat
            o = jnp.dot(att.astype(BF16), vb, preferred_element_type=F32)
            o = o + jnp.dot((q * q_f).astype(BF16), sf.astype(BF16), preferred_element_type=F32)
            o = o + jnp.dot((q * q_b).astype(BF16), sb_scr[ci].astype(BF16), preferred_element_type=F32)
            o = o - jnp.mean(o, axis=-1, keepdims=True)
            o = o * lax.rsqrt(jnp.mean(o * o, axis=-1, keepdims=True) + LN_EPS)
            o_ref[sl, :] = _silu(g_ref[sl, :]) * (o * gn)
            return g_fc * sf + _dot_ta((k * k_f).astype(BF16), vb)

        sf_fin = lax.fori_loop(0, hi - lo, fwd, sf0)
        return sf_fin, sb_fin

    sf_c, sb_c = run(0, n_ctx, zero, zero)
    run(n_ctx, n_all, sf_c, sb_c)


def _scan_a(z, cos, sup, sdn, dec, gn_a):
    blk = lambda col0: pl.BlockSpec((T_ALL, LANES), lambda b, h: (b, col0 + h))
    tbl = pl.BlockSpec((T_ALL, LANES), lambda b, h: (0, 0))
    return pl.pallas_call(
        _scan_a_kernel,
        grid=(BATCH, H_A),
        in_specs=[blk(0), blk(4), blk(8), blk(12), tbl, tbl, tbl,
                  pl.BlockSpec((1, 7, CHUNK_A, LANES), lambda b, h: (h, 0, 0, 0)),
                  pl.BlockSpec((1, LANES), lambda b, h: (0, h))],
        out_specs=pl.BlockSpec((T_ALL, LANES), lambda b, h: (b, h)),
        out_shape=jax.ShapeDtypeStruct((N_ALL, H_A * DV_A), F32),
        scratch_shapes=[pltpu.VMEM((T_ALL, LANES), F32),
                        pltpu.VMEM((T_ALL // CHUNK_A, DK_A, DV_A), F32)],
        compiler_params=_cparams(("arbitrary", "arbitrary")),
        name="scan_retention",
    )(z, z, z, z, cos, sup, sdn, dec, gn_a)


def _log_sigmoid(g):
    return jnp.minimum(g, 0.0) - jnp.log1p(jnp.exp(-jnp.abs(g)))


def _scan_b_kernel(q_ref, k_ref, v_ref, g_ref, lr_ref, wlr_ref, blr_ref, gn_ref, o_ref,
                   laf_scr, lab_scr, sb_scr):
    c = CHUNK_B
    n_ctx = CTX_LEN // c
    n_all = T_ALL // c
    scale = DK_B ** -0.5
    gates = jnp.dot(lr_ref[...], wlr_ref[0], preferred_element_type=F32, precision=HIGHEST) + blr_ref[0]
    laf_scr[...] = _log_sigmoid(gates[:, :LANES]) * (1.0 / GLA_TAU)
    lab_scr[...] = _log_sigmoid(gates[:, LANES:]) * (1.0 / GLA_TAU)

    lane = lax.broadcasted_iota(jnp.int32, (1, LANES), 1)
    masks = [(lane < DK_B).astype(F32), (lane >= DK_B).astype(F32)]
    ri = lax.broadcasted_iota(jnp.int32, (c, c), 0)
    cj = lax.broadcasted_iota(jnp.int32, (c, c), 1)
    lower = (cj <= ri)
    tri_lo = lower.astype(F32)
    tri_up = (cj >= ri).astype(F32)
    ones = jnp.ones((c, LANES), F32)
    gn = gn_ref[...]
    zero = jnp.zeros((LANES, DV_B), F32)

    def chunk(ci):
        return pl.ds(pl.multiple_of(ci * c, c), c)

    def run(lo, hi, sf0, sb0):
        def bwd(j, sb):
            ci = hi - 1 - j
            sl = chunk(ci)
            la = lab_scr[sl, :]
            rb = jnp.dot(tri_up, la, preferred_element_type=F32, precision=HIGHEST)
            kt = k_ref[sl, :] * jnp.exp(rb[0:1, :] - rb)
            e_tot = jnp.exp(_dot_ta(la, ones, precision=HIGHEST))
            v = v_ref[sl, :]
            new = []
            for h in range(2):
                sb_scr[ci, h] = sb[h]
                kv = _dot_ta((kt * masks[h]).astype(BF16), v[:, h * DV_B:(h + 1) * DV_B].astype(BF16))
                new.append(e_tot * sb[h] + kv)
            return tuple(new)

        sb_fin = lax.fori_loop(0, hi - lo, bwd, sb0)

        def fwd(j, sf):
            ci = lo + j
            sl = chunk(ci)
            laf = laf_scr[sl, :]
            lab = lab_scr[sl, :]
            b = jnp.dot(tri_lo, laf, preferred_element_type=F32, precision=HIGHEST)
            rb = jnp.dot(tri_up, lab, preferred_element_type=F32, precision=HIGHEST)
            q = q_ref[sl, :] * scale
            k = k_ref[sl, :]
            qd_f = q * jnp.exp(b)
            kd_f = (k * jnp.exp(-b)).astype(BF16)
            qd_b = q * jnp.exp(rb - lab)
            kd_b = (k * jnp.exp(-rb)).astype(BF16)
            kt = k * jnp.exp(b[c - 1:c, :] - b)
            e_tot = jnp.exp(_dot_ta(laf, ones, precision=HIGHEST))
            v = v_ref[sl, :]
            g = g_ref[sl, :]
            new = []
            for h in range(2):
                vh = v[:, h * DV_B:(h + 1) * DV_B].astype(BF16)
                qf = (qd_f * masks[h]).astype(BF16)
                qb = (qd_b * masks[h]).astype(BF16)
                att = jnp.where(lower, _dot_tb(qf, kd_f), _dot_tb(qb, kd_b))
                o = jnp.dot(att.astype(BF16), vh, preferred_element_type=F32)
                o = o + jnp.dot(qf, sf[h].astype(BF16), preferred_element_type=F32)
                o = o + jnp.dot(qb, sb_scr[ci, h].astype(BF16), preferred_element_type=F32)
                o = o * lax.rsqrt(jnp.mean(o * o, axis=-1, keepdims=True) + LN_EPS)
                cols = slice(h * DV_B, (h + 1) * DV_B)
                o_ref[sl, cols] = _silu(g[:, cols]) * (o * gn[:, cols])
                new.append(e_tot * sf[h] + _dot_ta((kt * masks[h]).astype(BF16), vh))
            return tuple(new)

        sf_fin = lax.fori_loop(0, hi - lo, fwd, sf0)
        return sf_fin, sb_fin

    sf_c, sb_c = run(0, n_ctx, (zero, zero), (zero, zero))
    run(n_ctx, n_all, sf_c, sb_c)


def _scan_b(z, wlr, blr, gn_b):
    pairs = H_B // 2
    return pl.pallas_call(
        _scan_b_kernel,
        grid=(BATCH, pairs),
        in_specs=[
            pl.BlockSpec((T_ALL, LANES), lambda b, p: (b, 16 + p)),
            pl.BlockSpec((T_ALL, LANES), lambda b, p: (b, 18 + p)),
            pl.BlockSpec((T_ALL, 2 * DV_B), lambda b, p: (b, 10 + p)),
            pl.BlockSpec((T_ALL, 2 * DV_B), lambda b, p: (b, 12 + p)),
            pl.BlockSpec((T_ALL, LANES), lambda b, p: (b, 28)),
            pl.BlockSpec((1, LANES, 2 * LANES), lambda b, p: (p, 0, 0)),
            pl.BlockSpec((1, 1, 2 * LANES), lambda b, p: (p, 0, 0)),
            pl.BlockSpec((1, 2 * DV_B), lambda b, p: (0, p)),
        ],
        out_specs=pl.BlockSpec((T_ALL, 2 * DV_B), lambda b, p: (b, p)),
        out_shape=jax.ShapeDtypeStruct((N_ALL, H_B * DV_B), F32),
        scratch_shapes=[pltpu.VMEM((T_ALL, LANES), F32),
                        pltpu.VMEM((T_ALL, LANES), F32),
                        pltpu.VMEM((T_ALL // CHUNK_B, 2, LANES, DV_B), F32)],
        compiler_params=_cparams(("arbitrary", "arbitrary")),
        name="scan_gla",
    )(z, z, z, z, z, wlr, blr, gn_b)


def _attn_kernel(lam_ref, q_ref, k_ref, v_ref, qcos_ref, qsup_ref, qsdn_ref, kcos_ref, ksup_ref, ksdn_ref,
                 gsub_ref, o_ref, k_scr, v_scr, *, post_scale):
    h = pl.program_id(1)
    quarter = DH_C // 4
    scale = DH_C ** -0.5

    @pl.when(pl.program_id(2) == 0)
    def _():
        k_scr[...] = _rope(k_ref[...], kcos_ref[...], ksup_ref[...], ksdn_ref[...], quarter).astype(BF16)
        v_scr[...] = v_ref[...].astype(BF16)

    lam = lam_ref[h]
    lane = lax.broadcasted_iota(jnp.int32, (1, LANES), 1)
    m1 = (lane < DH_C).astype(F32)
    m2 = (lane >= DH_C).astype(F32)
    q = _rope(q_ref[...], qcos_ref[...], qsup_ref[...], qsdn_ref[...], quarter) * scale
    kb = k_scr[...]
    s1 = _dot_tb((q * m1).astype(BF16), kb)
    s2 = _dot_tb((q * m2).astype(BF16), kb)
    e1 = jnp.exp(s1 - jnp.max(s1, axis=-1, keepdims=True))
    e2 = jnp.exp(s2 - jnp.max(s2, axis=-1, keepdims=True))
    r1 = 1.0 / jnp.sum(e1, axis=-1, keepdims=True)
    r2 = lam / jnp.sum(e2, axis=-1, keepdims=True)
    a = e1 * r1 - e2 * r2
    o = jnp.dot(a.astype(BF16), v_scr[...], preferred_element_type=F32)
    o = o * lax.rsqrt(jnp.mean(o * o, axis=-1, keepdims=True) + LN_EPS)
    o_ref[...] = o * (gsub_ref[...] * post_scale)


def _attention(z, lam, cos, sup, sdn, gsub, post_scale, y_prev, latent):
    kern = functools.partial(_attn_kernel, post_scale=post_scale)
    tile0, n_qt, n_keys = (CTX_LEN // TQ, SEQ // TQ, T_ALL) if latent else (0, CTX_LEN // TQ, CTX_LEN)
    kv_per_batch = T_ALL // n_keys
    q_row = lambda b, h, t, lam_r: (b * TILES_PER_BATCH + tile0 + t, h)
    kv = lambda col0: pl.BlockSpec((n_keys, LANES), lambda b, h, t, lam_r: (b * kv_per_batch, col0 + h))
    q_tbl = pl.BlockSpec((TQ, LANES), lambda b, h, t, lam_r: (tile0 + t, 0))
    k_tbl = pl.BlockSpec((n_keys, LANES), lambda b, h, t, lam_r: (0, 0))
    in_specs = [pl.BlockSpec((TQ, LANES), q_row), kv(H_C), kv(2 * H_C),
                q_tbl, q_tbl, q_tbl, k_tbl, k_tbl, k_tbl,
                pl.BlockSpec((1, LANES), lambda b, h, t, lam_r: (0, h))]
    args = [z, z, z, cos, sup, sdn, cos, sup, sdn, gsub]
    aliases = {}
    kern_fn = kern
    if y_prev is not None:
        in_specs.append(pl.BlockSpec(memory_space=pl.ANY))
        args.append(y_prev)
        aliases = {len(args): 0}
        kern_fn = lambda *refs: kern(*refs[:11], *refs[12:])
    return pl.pallas_call(
        kern_fn,
        grid_spec=pltpu.PrefetchScalarGridSpec(
            num_scalar_prefetch=1,
            grid=(BATCH, H_C, n_qt),
            in_specs=in_specs,
            out_specs=pl.BlockSpec((TQ, LANES), q_row),
            scratch_shapes=[pltpu.VMEM((n_keys, LANES), BF16), pltpu.VMEM((n_keys, LANES), BF16)],
        ),
        out_shape=jax.ShapeDtypeStruct((N_ALL, H_C * DV_C), F32),
        input_output_aliases=aliases,
        compiler_params=_cparams(("arbitrary", "arbitrary", "arbitrary")),
        name="diff_attention_latent" if latent else "diff_attention_ctx",
    )(lam, *args)


def _layer_norm(r, g, b):
    mu = jnp.mean(r, axis=-1, keepdims=True)
    d = r - mu
    var = jnp.mean(d * d, axis=-1, keepdims=True)
    return d * lax.rsqrt(var + LN_EPS) * g + b


def _proj_ln_kernel(y1_ref, y2_ref, w1_ref, w2_ref, x_ref, mod_ref, g_ref, b_ref, o_ref):
    y = jnp.dot(y1_ref[...].astype(BF16), w1_ref[...], preferred_element_type=F32)
    y = y + jnp.dot(y2_ref[...].astype(BF16), w2_ref[...], preferred_element_type=F32)
    r = DEEPNORM_ALPHA * x_ref[...] + mod_ref[0, 2:3, :] * y
    o_ref[...] = _layer_norm(r, g_ref[...], b_ref[...])


def _proj_ln(y1, y2, col2, w_out_bf16, x, mod_l, ln_g, ln_b, latent_only):
    half = D_MODEL // 2
    rt = _row_tile(latent_only)
    mr = _mod_row(latent_only)
    n_tiles = BATCH * (LATENT_TILES_PER_BATCH if latent_only else TILES_PER_BATCH)
    return pl.pallas_call(
        _proj_ln_kernel,
        grid=(n_tiles,),
        in_specs=[
            pl.BlockSpec((TM, half), lambda i: (rt(i), 0)),
            pl.BlockSpec((TM, half), lambda i: (rt(i), col2)),
            pl.BlockSpec((half, D_MODEL), lambda i: (0, 0)),
            pl.BlockSpec((half, D_MODEL), lambda i: (1, 0)),
            pl.BlockSpec((TM, D_MODEL), lambda i: (rt(i), 0)),
            pl.BlockSpec((1, 6, D_MODEL), lambda i: (mr(i), 0, 0)),
            pl.BlockSpec((1, D_MODEL), lambda i: (0, 0)),
            pl.BlockSpec((1, D_MODEL), lambda i: (0, 0)),
        ],
        out_specs=pl.BlockSpec((TM, D_MODEL), lambda i: (rt(i), 0)),
        out_shape=jax.ShapeDtypeStruct((N_ALL, D_MODEL), F32),
        compiler_params=_cparams(("arbitrary",)),
        name="proj_ln",
    )(y1, y2, w_out_bf16, w_out_bf16, x, mod_l, ln_g, ln_b)


def _router_kernel(x_ref, mod_ref, w_ref, b_ref, u_ref, mi_ref, mf_ref, cnt_ref, carry):
    i = pl.program_id(0)

    @pl.when(i == 0)
    def _():
        carry[...] = jnp.zeros_like(carry)

    u = x_ref[...] * (1.0 + mod_ref[0, 4:5, :]) + mod_ref[0, 3:4, :]
    u_ref[...] = u
    logits = jnp.dot(u, w_ref[...], preferred_element_type=F32, precision=HIGHEST) + b_ref[...]
    lane = lax.broadcasted_iota(jnp.int32, (TM, LANES), 1)
    lane_f = lane.astype(F32)
    neg = -jnp.inf
    big = 1e9

    gmask = lane < N_GROUPS
    gl = jnp.where(gmask, logits, neg)
    gmax = jnp.max(gl, axis=-1, keepdims=True)
    gidx = jnp.min(jnp.where(gl == gmax, lane_f, big), axis=-1, keepdims=True)
    gw = 1.0 / jnp.sum(jnp.where(gmask, jnp.exp(logits - gmax), 0.0), axis=-1, keepdims=True)

    e_lane = lane - N_GROUPS
    in_grp = (e_lane >= 0) & (e_lane < N_EXPERTS) & ((e_lane >> 3) == gidx.astype(jnp.int32))
    el = jnp.where(in_grp, logits, neg)
    v1 = jnp.max(el, axis=-1, keepdims=True)
    i1 = jnp.min(jnp.where(el == v1, lane_f, big), axis=-1, keepdims=True)
    el2 = jnp.where(lane_f == i1, neg, el)
    v2 = jnp.max(el2, axis=-1, keepdims=True)
    i2 = jnp.min(jnp.where(el2 == v2, lane_f, big), axis=-1, keepdims=True)
    t = jnp.exp(v2 - v1)
    c0 = gw / (1.0 + t)
    c1 = gw * t / (1.0 + t)
    e0 = i1 - N_GROUPS
    e1 = i2 - N_GROUPS

    oh0 = lane_f == e0
    oh1 = lane_f == e1
    cnt = oh0.astype(F32) + oh1.astype(F32)
    ri = lax.broadcasted_iota(jnp.int32, (TM, TM), 0)
    cj = lax.broadcasted_iota(jnp.int32, (TM, TM), 1)
    strict = (cj < ri).astype(BF16)
    before = jnp.dot(strict, cnt.astype(BF16), preferred_element_type=F32) + carry[0:1, :]
    r0 = jnp.sum(jnp.where(oh0, before, 0.0), axis=-1, keepdims=True)
    r1 = jnp.sum(jnp.where(oh1, before, 0.0), axis=-1, keepdims=True)
    carry[0:1, :] = carry[0:1, :] + jnp.sum(cnt, axis=0, keepdims=True)

    mi = jnp.where(lane == 0, e0, jnp.where(lane == 1, e1, jnp.where(lane == 2, r0, jnp.where(lane == 3, r1, 0.0))))
    mi_ref[...] = mi.astype(jnp.int32)
    mf_ref[...] = jnp.where(lane == 0, c0, jnp.where(lane == 1, c1, 0.0))
    cnt_ref[...] = carry[...]


def _router(x, mod_l, w_route, b_route, latent_only):
    rt = _row_tile(latent_only)
    mr = _mod_row(latent_only)
    n_tiles = BATCH * (LATENT_TILES_PER_BATCH if latent_only else TILES_PER_BATCH)
    row_blk = lambda w: pl.BlockSpec((TM, w), lambda i: (rt(i), 0))
    return pl.pallas_call(
        _router_kernel,
        grid=(n_tiles,),
        in_specs=[
            row_blk(D_MODEL),
            pl.BlockSpec((1, 6, D_MODEL), lambda i: (mr(i), 0, 0)),
            pl.BlockSpec((D_MODEL, LANES), lambda i: (0, 0)),
            pl.BlockSpec((1, LANES), lambda i: (0, 0)),
        ],
        out_specs=[row_blk(D_MODEL), row_blk(LANES), row_blk(LANES),
                   pl.BlockSpec((8, LANES), lambda i: (0, 0))],
        out_shape=[jax.ShapeDtypeStruct((N_ALL, D_MODEL), F32),
                   jax.ShapeDtypeStruct((N_ALL, LANES), jnp.int32),
                   jax.ShapeDtypeStruct((N_ALL, LANES), F32),
                   jax.ShapeDtypeStruct((8, LANES), F32)],
        scratch_shapes=[pltpu.VMEM((8, LANES), F32)],
        compiler_params=_cparams(("arbitrary",)),
        name="moe_router",
    )(x, mod_l, w_route, b_route)


def _expert_kernel(blk_exp_ref, n_used_ref, rows_hbm, u_hbm, wg_ref, wu_ref, wd_ref, buf_hbm,
                   idx_smem, xs, ys, sem_idx, sem_in, sem_out):
    i = pl.program_id(0)

    @pl.when(i < n_used_ref[0])
    def _():
        idx_copy = pltpu.make_async_copy(rows_hbm.at[pl.ds(i * MOE_ROWS, MOE_ROWS)], idx_smem, sem_idx)
        idx_copy.start()
        idx_copy.wait()

        def row_in(r):
            s = idx_smem[r]
            tok = jnp.where(s >= 2 * N_ALL, s - 2 * N_ALL, jnp.where(s >= N_ALL, s - N_ALL, s))
            return pltpu.make_async_copy(u_hbm.at[pl.ds(tok, 1)], xs.at[pl.ds(r, 1)], sem_in)

        def row_out(r):
            return pltpu.make_async_copy(ys.at[pl.ds(r, 1)], buf_hbm.at[pl.ds(idx_smem[r], 1)], sem_out)

        def each(fn):
            def body(r, carry):
                fn(r)
                return carry
            lax.fori_loop(0, MOE_ROWS, body, 0)

        each(lambda r: row_in(r).start())
        each(lambda r: row_in(r).wait())
        x = xs[...].astype(BF16)
        gate = jnp.dot(x, wg_ref[0].astype(BF16), preferred_element_type=F32)
        up = jnp.dot(x, wu_ref[0].astype(BF16), preferred_element_type=F32)
        hid = (_silu(gate) * up).astype(BF16)
        ys[...] = jnp.dot(hid, wd_ref[0].astype(BF16), preferred_element_type=F32)
        each(lambda r: row_out(r).start())
        each(lambda r: row_out(r).wait())


def _experts(blk_exp, n_used, rows, u, w_gate, w_up, w_down, n_blocks):
    return pl.pallas_call(
        _expert_kernel,
        grid_spec=pltpu.PrefetchScalarGridSpec(
            num_scalar_prefetch=2,
            grid=(n_blocks,),
            in_specs=[
                pl.BlockSpec(memory_space=pl.ANY),
                pl.BlockSpec(memory_space=pl.ANY),
                pl.BlockSpec((1, D_MODEL, D_EXPERT), lambda i, be, nu: (be[i], 0, 0)),
                pl.BlockSpec((1, D_MODEL, D_EXPERT), lambda i, be, nu: (be[i], 0, 0)),
                pl.BlockSpec((1, D_EXPERT, D_MODEL), lambda i, be, nu: (be[i], 0, 0)),
            ],
            out_specs=pl.BlockSpec(memory_space=pl.ANY),
            scratch_shapes=[
                pltpu.SMEM((MOE_ROWS,), jnp.int32),
                pltpu.VMEM((MOE_ROWS, D_MODEL), F32),
                pltpu.VMEM((MOE_ROWS, D_MODEL), F32),
                pltpu.SemaphoreType.DMA(()),
                pltpu.SemaphoreType.DMA(()),
                pltpu.SemaphoreType.DMA(()),
            ],
        ),
        out_shape=jax.ShapeDtypeStruct((2 * N_ALL + MOE_ROWS, D_MODEL), F32),
        compiler_params=_cparams(("arbitrary",)),
        name="moe_experts",
    )(blk_exp, n_used, rows, u, w_gate, w_up, w_down)


def _combine_ln_kernel(y0_ref, y1_ref, mf_ref, x_ref, mod_ref, g_ref, b_ref, o_ref):
    mf = mf_ref[...]
    y = mf[:, 0:1] * y0_ref[...] + mf[:, 1:2] * y1_ref[...]
    r = DEEPNORM_ALPHA * x_ref[...] + mod_ref[0, 5:6, :] * y
    o_ref[...] = _layer_norm(r, g_ref[...], b_ref[...])


def _combine_ln(buf, mf, x, mod_l, ln_g, ln_b, latent_only):
    rt = _row_tile(latent_only)
    mr = _mod_row(latent_only)
    n_tiles = BATCH * (LATENT_TILES_PER_BATCH if latent_only else TILES_PER_BATCH)
    out_rows = n_tiles * TM
    return pl.pallas_call(
        _combine_ln_kernel,
        grid=(n_tiles,),
        in_specs=[
            pl.BlockSpec((TM, D_MODEL), lambda i: (rt(i), 0)),
            pl.BlockSpec((TM, D_MODEL), lambda i: (N_ALL // TM + rt(i), 0)),
            pl.BlockSpec((TM, LANES), lambda i: (rt(i), 0)),
            pl.BlockSpec((TM, D_MODEL), lambda i: (rt(i), 0)),
            pl.BlockSpec((1, 6, D_MODEL), lambda i: (mr(i), 0, 0)),
            pl.BlockSpec((1, D_MODEL), lambda i: (0, 0)),
            pl.BlockSpec((1, D_MODEL), lambda i: (0, 0)),
        ],
        out_specs=pl.BlockSpec((TM, D_MODEL), lambda i: (i, 0)),
        out_shape=jax.ShapeDtypeStruct((out_rows, D_MODEL), F32),
        compiler_params=_cparams(("arbitrary",)),
        name="combine_ln",
    )(buf, buf, mf, x, mod_l, ln_g, ln_b)


def _moe(x1, mod_l, w_grp, b_grp, w_rexp, b_rexp, w_gate, w_up, w_down, ln_g, ln_b, latent_only):
    pad = LANES - N_GROUPS - N_EXPERTS
    w_route = jnp.concatenate([w_grp, w_rexp, jnp.zeros((D_MODEL, pad), F32)], axis=1)
    b_route = jnp.concatenate([b_grp, b_rexp, jnp.zeros((pad,), F32)])[None, :]
    u, mi, mf, cnt = _router(x1, mod_l, w_route, b_route, latent_only)

    counts = cnt[0, :N_EXPERTS].astype(jnp.int32)
    padded = (counts + MOE_ROWS - 1) // MOE_ROWS * MOE_ROWS
    pad_end = jnp.cumsum(padded)
    pad_start = pad_end - padded
    n_tok = N_ALL if not latent_only else BATCH * SEQ
    n_blocks = (2 * n_tok) // MOE_ROWS + N_EXPERTS
    p_rows = n_blocks * MOE_ROWS
    eid = mi[:, 0:2]
    dest = pad_start[jnp.clip(eid, 0, N_EXPERTS - 1)] + mi[:, 2:4]
    if latent_only:
        is_latent = (jnp.arange(N_ALL, dtype=jnp.int32) % T_ALL) >= CTX_LEN
        dest = jnp.where(is_latent[:, None], dest, p_rows)
    slot = jnp.arange(2 * N_ALL, dtype=jnp.int32)
    spare = 2 * N_ALL + (jnp.arange(p_rows, dtype=jnp.int32) % MOE_ROWS)
    rows = spare.at[dest.T.reshape(-1)].set(slot, mode="drop")
    blk_exp = jnp.minimum(jnp.searchsorted(pad_end, jnp.arange(n_blocks, dtype=jnp.int32) * MOE_ROWS,
                                           side="right"), N_EXPERTS - 1).astype(jnp.int32)
    n_used = (pad_end[-1:] // MOE_ROWS).astype(jnp.int32)
    buf = _experts(blk_exp, n_used, rows, u, w_gate, w_up, w_down, n_blocks)
    return _combine_ln(buf, mf, x1, mod_l, ln_g, ln_b, latent_only)


def kernel(x, c, ctx, c_ctx, ada_w, ada_b, ln1_g, ln1_b, ln2_g, ln2_b, ab_w_in, ab_w_lr_f, ab_b_lr_f, ab_w_lr_b, ab_b_lr_b, ab_gn_a, ab_gn_b, ab_w_out, c_w_qkv, c_lq1, c_lk1, c_lq2, c_lk2, c_subln_g, c_w_out, moe_w_grp, moe_b_grp, moe_w_rexp, moe_b_rexp, moe_w_gate, moe_w_up, moe_w_down):
    assert x.shape == (BATCH, SEQ, D_MODEL) and ctx.shape == (BATCH, CTX_LEN, D_MODEL)
    xs = jnp.concatenate([ctx, x], axis=1).reshape(N_ALL, D_MODEL)
    c_all = jnp.concatenate([c, c_ctx[None, :], jnp.zeros((16 - BATCH - 1, D_MODEL), F32)], axis=0)
    mod = _ada_tables(c_all, ada_w, ada_b).reshape(DEPTH, 16, 6, D_MODEL)

    rope_a = _rope_tables(DK_A, 1)
    rope_c = _rope_tables(DH_C, 2)
    dec_a = _retention_tables()

    for l in range(DEPTH):
        last = l == DEPTH - 1
        i = l // 2
        mod_l = mod[l]
        row = lambda v: v[None, :]
        if l % 2 == 0:
            assert not last
            w_in = jnp.pad(ab_w_in[i], ((0, 0), (0, AB_COLS - ab_w_in.shape[2]))).astype(BF16)
            z = _mod_matmul(xs, mod_l, w_in)
            ya = _scan_a(z, *rope_a, dec_a, row(ab_gn_a[i]))
            wf = ab_w_lr_f[i].reshape(GLA_RANK, H_B // 2, LANES)
            wb = ab_w_lr_b[i].reshape(GLA_RANK, H_B // 2, LANES)
            wlr = jnp.zeros((H_B // 2, LANES, 2 * LANES), F32)
            wlr = wlr.at[:, 0:GLA_RANK, 0:LANES].set(jnp.swapaxes(wf, 0, 1))
            wlr = wlr.at[:, GLA_RANK:2 * GLA_RANK, LANES:].set(jnp.swapaxes(wb, 0, 1))
            blr = jnp.concatenate([ab_b_lr_f[i].reshape(H_B // 2, 1, LANES),
                                   ab_b_lr_b[i].reshape(H_B // 2, 1, LANES)], axis=-1)
            yb = _scan_b(z, wlr, blr, row(ab_gn_b[i]))
            x1 = _proj_ln(ya, yb, 0, ab_w_out[i].astype(BF16), xs, mod_l, row(ln1_g[l]), row(ln1_b[l]), False)
        else:
            lam_init = 0.8 - 0.6 * math.exp(-0.3 * l)
            lam = (jnp.exp(jnp.sum(c_lq1[i] * c_lk1[i], axis=-1))
                   - jnp.exp(jnp.sum(c_lq2[i] * c_lk2[i], axis=-1))).astype(F32) + lam_init
            z = _mod_matmul(xs, mod_l, c_w_qkv[i].astype(BF16))
            gsub = row(c_subln_g[i])
            y = _attention(z, lam, *rope_c, gsub, 1.0 - lam_init, None, True)
            if not last:
                y = _attention(z, lam, *rope_c, gsub, 1.0 - lam_init, y, False)
            x1 = _proj_ln(y, y, 1, c_w_out[i].astype(BF16), xs, mod_l, row(ln1_g[l]), row(ln1_b[l]), last)
        xs = _moe(x1, mod_l, moe_w_grp[l], moe_b_grp[l], moe_w_rexp[l], moe_b_rexp[l],
                  moe_w_gate[l], moe_w_up[l], moe_w_down[l], row(ln2_g[l]), row(ln2_b[l]), last)
    return xs.reshape(BATCH, SEQ, D_MODEL)
```

```python
import functools
import math

import numpy as np
import jax
import jax.numpy as jnp
from jax import lax
from jax.experimental import pallas as pl
from jax.experimental.pallas import tpu as pltpu

F32 = jnp.float32
BF16 = jnp.bfloat16
HIGHEST = lax.Precision.HIGHEST

D_MODEL = 1024
BATCH = 8
SEQ = 2048
DEPTH = 4
GRID_W = 64
CTX_LEN = 256
ROPE_BASE = 10000.0
LN_EPS = 1e-5
DEEPNORM_ALPHA = (2 * DEPTH) ** 0.25
H_A = 4
DK_A = 128
DV_A = 128
CHUNK_A = 128
RET_EXP_FWD = 5.0
RET_EXP_BWD = 5.5
H_B = 4
DK_B = 64
DV_B = 128
GLA_RANK = 16
GLA_TAU = 16.0
CHUNK_B = 64
H_C = 8
DH_C = 64
DV_C = 128
N_GROUPS = 4
EXPERTS_PER_GROUP = 8
N_EXPERTS = 32
D_EXPERT = 512

LANES = 128
T_ALL = CTX_LEN + SEQ
N_ALL = BATCH * T_ALL
TM = 256
TILES_PER_BATCH = T_ALL // TM
LATENT_TILES_PER_BATCH = SEQ // TM
AB_COLS = 29 * LANES
MOE_ROWS = TM
ROW_UNROLL = 8
TQ = 256
VMEM_LIMIT = 56 * 1024 * 1024


def _cparams(sem):
    return pltpu.CompilerParams(dimension_semantics=sem, vmem_limit_bytes=VMEM_LIMIT)


def _silu(v):
    return v * (1.0 / (1.0 + jnp.exp(-v)))


def _n_tiles(latent_only):
    return BATCH * (LATENT_TILES_PER_BATCH if latent_only else TILES_PER_BATCH)


def _row_tile(latent_only):
    if latent_only:
        return lambda i: (i // LATENT_TILES_PER_BATCH) * TILES_PER_BATCH + 1 + i % LATENT_TILES_PER_BATCH
    return lambda i: i


def _mod_row(latent_only):
    if latent_only:
        return lambda i: i // LATENT_TILES_PER_BATCH
    return lambda i: jnp.where(i % TILES_PER_BATCH == 0, BATCH, i // TILES_PER_BATCH)


def _ada_kernel(c_ref, w_ref, b_ref, o_ref):
    sc = _silu(c_ref[...])
    o_ref[0] = jnp.dot(sc.astype(BF16), w_ref[0].astype(BF16), preferred_element_type=F32) + b_ref[0]


def _ada_tables(c_all, ada_w, ada_b):
    tn = 1536
    n_out = 6 * D_MODEL
    return pl.pallas_call(
        _ada_kernel,
        grid=(DEPTH, n_out // tn),
        in_specs=[
            pl.BlockSpec((16, D_MODEL), lambda l, j: (0, 0)),
            pl.BlockSpec((1, D_MODEL, tn), lambda l, j: (l, 0, j)),
            pl.BlockSpec((1, 1, tn), lambda l, j: (l, 0, j)),
        ],
        out_specs=pl.BlockSpec((1, 16, tn), lambda l, j: (l, 0, j)),
        out_shape=jax.ShapeDtypeStruct((DEPTH, 16, n_out), F32),
        compiler_params=_cparams(("arbitrary", "arbitrary")),
        name="ada_tables",
    )(c_all, ada_w, ada_b.reshape(DEPTH, 1, n_out))


def _modmm_kernel(x_ref, mod_ref, w_ref, o_ref):
    u = x_ref[...] * (1.0 + mod_ref[0, 1:2, :]) + mod_ref[0, 0:1, :]
    o_ref[...] = jnp.dot(u.astype(BF16), w_ref[...], preferred_element_type=F32)


def _mod_matmul(x, mod_l, w_bf16):
    n_out = w_bf16.shape[1]
    return pl.pallas_call(
        _modmm_kernel,
        grid=(N_ALL // TM,),
        in_specs=[
            pl.BlockSpec((TM, D_MODEL), lambda i: (i, 0)),
            pl.BlockSpec((1, 6, D_MODEL), lambda i: (_mod_row(False)(i), 0, 0)),
            pl.BlockSpec((D_MODEL, n_out), lambda i: (0, 0)),
        ],
        out_specs=pl.BlockSpec((TM, n_out), lambda i: (i, 0)),
        out_shape=jax.ShapeDtypeStruct((N_ALL, n_out), F32),
        compiler_params=_cparams(("arbitrary",)),
        name="mod_matmul",
    )(x, mod_l, w_bf16)


def _rope_tables(head_dim, reps):
    rows = SEQ // GRID_W
    row = np.repeat(np.arange(rows, dtype=np.float32), GRID_W)
    col = np.tile(np.arange(GRID_W, dtype=np.float32), rows)
    quarter = head_dim // 4
    inv = (ROPE_BASE ** (-np.arange(quarter, dtype=np.float32) / quarter)).astype(np.float32)
    ang_r = row[:, None] * inv
    ang_c = col[:, None] * inv
    ang = np.concatenate([ang_r, ang_r, ang_c, ang_c], axis=-1)
    cos = np.cos(ang).astype(np.float32)
    sin = np.sin(ang).astype(np.float32)
    q_idx = (np.arange(head_dim) // quarter) % 2
    sin_up = np.where(q_idx == 1, sin, 0.0).astype(np.float32)
    sin_dn = np.where(q_idx == 0, -sin, 0.0).astype(np.float32)

    def full(t, ctx_val):
        t = np.tile(t, (1, reps))
        return jnp.asarray(np.concatenate([np.full((CTX_LEN, t.shape[1]), ctx_val, np.float32), t], axis=0))

    return full(cos, 1.0), full(sin_up, 0.0), full(sin_dn, 0.0)


def _rope(x, cos, sin_up, sin_dn, quarter):
    width = x.shape[-1]
    return x * cos + pltpu.roll(x, quarter, 1) * sin_up + pltpu.roll(x, width - quarter, 1) * sin_dn


def _dot_tb(a, b):
    return lax.dot_general(a, b, (((1,), (1,)), ((), ())), preferred_element_type=F32)


def _dot_ta(a, b, precision=None):
    return lax.dot_general(a, b, (((0,), (0,)), ((), ())), preferred_element_type=F32, precision=precision)


def _retention_tables():
    c = CHUNK_A
    i = np.arange(c, dtype=np.float64)
    out = np.zeros((H_A, 7, c, LANES), np.float64)
    for h in range(H_A):
        lgf = np.log1p(-np.exp2(-(RET_EXP_FWD + h)))
        lgb = np.log1p(-np.exp2(-(RET_EXP_BWD + h)))
        d = i[:, None] - i[None, :]
        out[h, 0] = np.where(d >= 0, np.exp(lgf * d), np.exp(lgb * (-d - 1)))
        out[h, 1] = np.exp(lgf * (i + 1))[:, None]
        out[h, 2] = np.exp(lgb * (c - 1 - i))[:, None]
        out[h, 3] = np.exp(lgf * (c - 1 - i))[:, None]
        out[h, 4] = np.exp(lgb * i)[:, None]
        out[h, 5] = np.exp(lgf * c)
        out[h, 6] = np.exp(lgb * c)
    return jnp.asarray(out.astype(np.float32))


def _scan_a_kernel(q_ref, k_ref, v_ref, g_ref, cos_ref, sup_ref, sdn_ref, dec_ref, gn_ref, o_ref,
                   kr_scr, sb_scr):
    c = CHUNK_A
    n_ctx = CTX_LEN // c
    n_all = T_ALL // c
    scale = DK_A ** -0.5
    kr_scr[...] = _rope(k_ref[...], cos_ref[...], sup_ref[...], sdn_ref[...], DK_A // 4)
    dmat = dec_ref[0, 0]
    q_f, q_b, k_f, k_b = dec_ref[0, 1], dec_ref[0, 2], dec_ref[0, 3], dec_ref[0, 4]
    g_fc, g_bc = dec_ref[0, 5], dec_ref[0, 6]
    gn = gn_ref[...]
    zero = jnp.zeros((DK_A, DV_A), F32)

    def chunk(ci):
        return pl.ds(pl.multiple_of(ci * c, c), c)

    def kv_state(ci, k_dec):
        sl = chunk(ci)
        return _dot_ta((kr_scr[sl, :] * k_dec).astype(BF16), v_ref[sl, :].astype(BF16))

    def run(lo, hi, sf0, sb0):
        def bwd(j, sb):
            ci = hi - 1 - j
            sb_scr[ci] = sb
            return g_bc * sb + kv_state(ci, k_b)

        sb_fin = lax.fori_loop(0, hi - lo, bwd, sb0)

        def fwd(j, sf):
            ci = lo + j
            sl = chunk(ci)
            q = _rope(q_ref[sl, :], cos_ref[sl, :], sup_ref[sl, :], sdn_ref[sl, :], DK_A // 4) * scale
            k = kr_scr[sl, :]
            vb = v_ref[sl, :].astype(BF16)
            att = _dot_tb(q.astype(BF16), k.astype(BF16)) * dmat
            o = jnp.dot(att.astype(BF16), vb, preferred_element_type=F32)
            o = o + jnp.dot((q * q_f).astype(BF16), sf.astype(BF16), preferred_element_type=F32)
            o = o + jnp.dot((q * q_b).astype(BF16), sb_scr[ci].astype(BF16), preferred_element_type=F32)
            o = o - jnp.mean(o, axis=-1, keepdims=True)
            o = o * lax.rsqrt(jnp.mean(o * o, axis=-1, keepdims=True) + LN_EPS)
            o_ref[sl, :] = _silu(g_ref[sl, :]) * (o * gn)
            return g_fc * sf + _dot_ta((k * k_f).astype(BF16), vb)

        sf_fin = lax.fori_loop(0, hi - lo, fwd, sf0)
        return sf_fin, sb_fin

    sf_c, sb_c = run(0, n_ctx, zero, zero)
    run(n_ctx, n_all, sf_c, sb_c)


def _scan_a(z, cos, sup, sdn, dec, gn_a):
    blk = lambda col0: pl.BlockSpec((T_ALL, LANES), lambda b, h: (b, col0 + h))
    tbl = pl.BlockSpec((T_ALL, LANES), lambda b, h: (0, 0))
    return pl.pallas_call(
        _scan_a_kernel,
        grid=(BATCH, H_A),
        in_specs=[blk(0), blk(4), blk(8), blk(12), tbl, tbl, tbl,
                  pl.BlockSpec((1, 7, CHUNK_A, LANES), lambda b, h: (h, 0, 0, 0)),
                  pl.BlockSpec((1, LANES), lambda b, h: (0, h))],
        out_specs=pl.BlockSpec((T_ALL, LANES), lambda b, h: (b, h)),
        out_shape=jax.ShapeDtypeStruct((N_ALL, H_A * DV_A), F32),
        scratch_shapes=[pltpu.VMEM((T_ALL, LANES), F32),
                        pltpu.VMEM((T_ALL // CHUNK_A, DK_A, DV_A), F32)],
        compiler_params=_cparams(("arbitrary", "arbitrary")),
        name="scan_retention",
    )(z, z, z, z, cos, sup, sdn, dec, gn_a)


def _log_sigmoid(g):
    return jnp.minimum(g, 0.0) - jnp.log1p(jnp.exp(-jnp.abs(g)))


def _scan_b_kernel(q_ref, k_ref, v_ref, g_ref, lr_ref, wlr_ref, blr_ref, gn_ref, o_ref,
                   laf_scr, lab_scr, sb_scr):
    c = CHUNK_B
    n_ctx = CTX_LEN // c
    n_all = T_ALL // c
    scale = DK_B ** -0.5
    gates = jnp.dot(lr_ref[...], wlr_ref[0], preferred_element_type=F32, precision=HIGHEST) + blr_ref[0]
    laf_scr[...] = _log_sigmoid(gates[:, :LANES]) * (1.0 / GLA_TAU)
    lab_scr[...] = _log_sigmoid(gates[:, LANES:]) * (1.0 / GLA_TAU)

    lane = lax.broadcasted_iota(jnp.int32, (1, LANES), 1)
    masks = [(lane < DK_B).astype(F32), (lane >= DK_B).astype(F32)]
    ri = lax.broadcasted_iota(jnp.int32, (c, c), 0)
    cj = lax.broadcasted_iota(jnp.int32, (c, c), 1)
    lower = (cj <= ri)
    tri_lo = lower.astype(F32)
    tri_up = (cj >= ri).astype(F32)
    ones = jnp.ones((c, LANES), F32)
    gn = gn_ref[...]
    zero = jnp.zeros((LANES, DV_B), F32)

    def chunk(ci):
        return pl.ds(pl.multiple_of(ci * c, c), c)

    def run(lo, hi, sf0, sb0):
        def bwd(j, sb):
            ci = hi - 1 - j
            sl = chunk(ci)
            la = lab_scr[sl, :]
            rb = jnp.dot(tri_up, la, preferred_element_type=F32, precision=HIGHEST)
            kt = k_ref[sl, :] * jnp.exp(rb[0:1, :] - rb)
            e_tot = jnp.exp(_dot_ta(la, ones, precision=HIGHEST))
            v = v_ref[sl, :]
            new = []
            for h in range(2):
                sb_scr[ci, h] = sb[h]
                kv = _dot_ta((kt * masks[h]).astype(BF16), v[:, h * DV_B:(h + 1) * DV_B].astype(BF16))
                new.append(e_tot * sb[h] + kv)
            return tuple(new)

        sb_fin = lax.fori_loop(0, hi - lo, bwd, sb0)

        def fwd(j, sf):
            ci = lo + j
            sl = chunk(ci)
            laf = laf_scr[sl, :]
            lab = lab_scr[sl, :]
            b = jnp.dot(tri_lo, laf, preferred_element_type=F32, precision=HIGHEST)
            rb = jnp.dot(tri_up, lab, preferred_element_type=F32, precision=HIGHEST)
            q = q_ref[sl, :] * scale
            k = k_ref[sl, :]
            qd_f = q * jnp.exp(b)
            kd_f = (k * jnp.exp(-b)).astype(BF16)
            qd_b = q * jnp.exp(rb - lab)
            kd_b = (k * jnp.exp(-rb)).astype(BF16)
            kt = k * jnp.exp(b[c - 1:c, :] - b)
            e_tot = jnp.exp(_dot_ta(laf, ones, precision=HIGHEST))
            v = v_ref[sl, :]
            g = g_ref[sl, :]
            new = []
            for h in range(2):
                vh = v[:, h * DV_B:(h + 1) * DV_B].astype(BF16)
                qf = (qd_f * masks[h]).astype(BF16)
                qb = (qd_b * masks[h]).astype(BF16)
                att = jnp.where(lower, _dot_tb(qf, kd_f), _dot_tb(qb, kd_b))
                o = jnp.dot(att.astype(BF16), vh, preferred_element_type=F32)
                o = o + jnp.dot(qf, sf[h].astype(BF16), preferred_element_type=F32)
                o = o + jnp.dot(qb, sb_scr[ci, h].astype(BF16), preferred_element_type=F32)
                o = o * lax.rsqrt(jnp.mean(o * o, axis=-1, keepdims=True) + LN_EPS)
                cols = slice(h * DV_B, (h + 1) * DV_B)
                o_ref[sl, cols] = _silu(g[:, cols]) * (o * gn[:, cols])
                new.append(e_tot * sf[h] + _dot_ta((kt * masks[h]).astype(BF16), vh))
            return tuple(new)

        sf_fin = lax.fori_loop(0, hi - lo, fwd, sf0)
        return sf_fin, sb_fin

    sf_c, sb_c = run(0, n_ctx, (zero, zero), (zero, zero))
    run(n_ctx, n_all, sf_c, sb_c)


def _scan_b(z, wlr, blr, gn_b):
    pairs = H_B // 2
    return pl.pallas_call(
        _scan_b_kernel,
        grid=(BATCH, pairs),
        in_specs=[
            pl.BlockSpec((T_ALL, LANES), lambda b, p: (b, 16 + p)),
            pl.BlockSpec((T_ALL, LANES), lambda b, p: (b, 18 + p)),
            pl.BlockSpec((T_ALL, 2 * DV_B), lambda b, p: (b, 10 + p)),
            pl.BlockSpec((T_ALL, 2 * DV_B), lambda b, p: (b, 12 + p)),
            pl.BlockSpec((T_ALL, LANES), lambda b, p: (b, 28)),
            pl.BlockSpec((1, LANES, 2 * LANES), lambda b, p: (p, 0, 0)),
            pl.BlockSpec((1, 1, 2 * LANES), lambda b, p: (p, 0, 0)),
            pl.BlockSpec((1, 2 * DV_B), lambda b, p: (0, p)),
        ],
        out_specs=pl.BlockSpec((T_ALL, 2 * DV_B), lambda b, p: (b, p)),
        out_shape=jax.ShapeDtypeStruct((N_ALL, H_B * DV_B), F32),
        scratch_shapes=[pltpu.VMEM((T_ALL, LANES), F32),
                        pltpu.VMEM((T_ALL, LANES), F32),
                        pltpu.VMEM((T_ALL // CHUNK_B, 2, LANES, DV_B), F32)],
        compiler_params=_cparams(("arbitrary", "arbitrary")),
        name="scan_gla",
    )(z, z, z, z, z, wlr, blr, gn_b)


def _attn_kernel(lam_ref, q_ref, k_ref, v_ref, qcos_ref, qsup_ref, qsdn_ref, kcos_ref, ksup_ref, ksdn_ref,
                 gsub_ref, o_ref, k_scr, v_scr, *, post_scale, tile0):
    h = pl.program_id(1)
    t = pl.program_id(2)
    quarter = DH_C // 4
    scale = DH_C ** -0.5

    @pl.when(t == 0)
    def _():
        k_scr[...] = _rope(k_ref[...], kcos_ref[...], ksup_ref[...], ksdn_ref[...], quarter).astype(BF16)
        v_scr[...] = v_ref[...].astype(BF16)

    lam = lam_ref[h]
    lane = lax.broadcasted_iota(jnp.int32, (1, LANES), 1)
    m1 = (lane < DH_C).astype(F32)
    m2 = (lane >= DH_C).astype(F32)

    def attend(n_keys):
        q = _rope(q_ref[...], qcos_ref[...], qsup_ref[...], qsdn_ref[...], quarter) * scale
        kb = k_scr[0:n_keys, :]
        s1 = _dot_tb((q * m1).astype(BF16), kb)
        s2 = _dot_tb((q * m2).astype(BF16), kb)
        e1 = jnp.exp(s1 - jnp.max(s1, axis=-1, keepdims=True))
        e2 = jnp.exp(s2 - jnp.max(s2, axis=-1, keepdims=True))
        r1 = 1.0 / jnp.sum(e1, axis=-1, keepdims=True)
        r2 = lam / jnp.sum(e2, axis=-1, keepdims=True)
        a = e1 * r1 - e2 * r2
        o = jnp.dot(a.astype(BF16), v_scr[0:n_keys, :], preferred_element_type=F32)
        o = o * lax.rsqrt(jnp.mean(o * o, axis=-1, keepdims=True) + LN_EPS)
        o_ref[...] = o * (gsub_ref[...] * post_scale)

    if tile0 == 0:
        pl.when(t == 0)(lambda: attend(CTX_LEN))
        pl.when(t > 0)(lambda: attend(T_ALL))
    else:
        attend(T_ALL)


def _attention(z, lam, cos, sup, sdn, gsub, post_scale, latent_only):
    tile0 = 1 if latent_only else 0
    n_qt = TILES_PER_BATCH - tile0
    kern = functools.partial(_attn_kernel, post_scale=post_scale, tile0=tile0)
    kv = lambda col0: pl.BlockSpec((T_ALL, LANES), lambda b, h, t, lam_r: (b, col0 + h))
    q_tbl = pl.BlockSpec((TQ, LANES), lambda b, h, t, lam_r: (tile0 + t, 0))
    k_tbl = pl.BlockSpec((T_ALL, LANES), lambda b, h, t, lam_r: (0, 0))
    return pl.pallas_call(
        kern,
        grid_spec=pltpu.PrefetchScalarGridSpec(
            num_scalar_prefetch=1,
            grid=(BATCH, H_C, n_qt),
            in_specs=[pl.BlockSpec((TQ, LANES), lambda b, h, t, lam_r: (b * TILES_PER_BATCH + tile0 + t, h)),
                      kv(H_C), kv(2 * H_C), q_tbl, q_tbl, q_tbl, k_tbl, k_tbl, k_tbl,
                      pl.BlockSpec((1, LANES), lambda b, h, t, lam_r: (0, h))],
            out_specs=pl.BlockSpec((TQ, LANES), lambda b, h, t, lam_r: (b * n_qt + t, h)),
            scratch_shapes=[pltpu.VMEM((T_ALL, LANES), BF16), pltpu.VMEM((T_ALL, LANES), BF16)],
        ),
        out_shape=jax.ShapeDtypeStruct((BATCH * n_qt * TQ, H_C * DV_C), F32),
        compiler_params=_cparams(("arbitrary", "arbitrary", "arbitrary")),
        name="diff_attention",
    )(lam, z, z, z, cos, sup, sdn, cos, sup, sdn, gsub)


def _layer_norm(r, g, b):
    mu = jnp.mean(r, axis=-1, keepdims=True)
    d = r - mu
    var = jnp.mean(d * d, axis=-1, keepdims=True)
    return d * lax.rsqrt(var + LN_EPS) * g + b


def _proj_ln_kernel(y1_ref, y2_ref, w1_ref, w2_ref, x_ref, mod_ref, g_ref, b_ref, o_ref):
    y = jnp.dot(y1_ref[...].astype(BF16), w1_ref[...], preferred_element_type=F32)
    y = y + jnp.dot(y2_ref[...].astype(BF16), w2_ref[...], preferred_element_type=F32)
    r = DEEPNORM_ALPHA * x_ref[...] + mod_ref[0, 2:3, :] * y
    o_ref[...] = _layer_norm(r, g_ref[...], b_ref[...])


def _proj_ln(y1, y2, col2, w_out_bf16, x, mod_l, ln_g, ln_b, latent_only):
    half = D_MODEL // 2
    rt = _row_tile(latent_only)
    mr = _mod_row(latent_only)
    n_tiles = _n_tiles(latent_only)
    return pl.pallas_call(
        _proj_ln_kernel,
        grid=(n_tiles,),
        in_specs=[
            pl.BlockSpec((TM, half), lambda i: (i, 0)),
            pl.BlockSpec((TM, half), lambda i: (i, col2)),
            pl.BlockSpec((half, D_MODEL), lambda i: (0, 0)),
            pl.BlockSpec((half, D_MODEL), lambda i: (1, 0)),
            pl.BlockSpec((TM, D_MODEL), lambda i: (rt(i), 0)),
            pl.BlockSpec((1, 6, D_MODEL), lambda i: (mr(i), 0, 0)),
            pl.BlockSpec((1, D_MODEL), lambda i: (0, 0)),
            pl.BlockSpec((1, D_MODEL), lambda i: (0, 0)),
        ],
        out_specs=pl.BlockSpec((TM, D_MODEL), lambda i: (i, 0)),
        out_shape=jax.ShapeDtypeStruct((n_tiles * TM, D_MODEL), F32),
        compiler_params=_cparams(("arbitrary",)),
        name="proj_ln",
    )(y1, y2, w_out_bf16, w_out_bf16, x, mod_l, ln_g, ln_b)


def _router_kernel(x_ref, mod_ref, w_ref, b_ref, mi_ref, mf_ref, cnt_ref, carry):
    i = pl.program_id(0)

    @pl.when(i == 0)
    def _():
        carry[...] = jnp.zeros_like(carry)

    u = x_ref[...] * (1.0 + mod_ref[0, 4:5, :]) + mod_ref[0, 3:4, :]
    logits = jnp.dot(u, w_ref[...], preferred_element_type=F32, precision=HIGHEST) + b_ref[...]
    lane = lax.broadcasted_iota(jnp.int32, (TM, LANES), 1)
    lane_f = lane.astype(F32)
    neg = -jnp.inf
    big = 1e9

    gmask = lane < N_GROUPS
    gl = jnp.where(gmask, logits, neg)
    gmax = jnp.max(gl, axis=-1, keepdims=True)
    gidx = jnp.min(jnp.where(gl == gmax, lane_f, big), axis=-1, keepdims=True)
    gw = 1.0 / jnp.sum(jnp.where(gmask, jnp.exp(logits - gmax), 0.0), axis=-1, keepdims=True)

    e_lane = lane - N_GROUPS
    in_grp = (e_lane >= 0) & (e_lane < N_EXPERTS) & ((e_lane >> 3) == gidx.astype(jnp.int32))
    el = jnp.where(in_grp, logits, neg)
    v1 = jnp.max(el, axis=-1, keepdims=True)
    i1 = jnp.min(jnp.where(el == v1, lane_f, big), axis=-1, keepdims=True)
    el2 = jnp.where(lane_f == i1, neg, el)
    v2 = jnp.max(el2, axis=-1, keepdims=True)
    i2 = jnp.min(jnp.where(el2 == v2, lane_f, big), axis=-1, keepdims=True)
    t = jnp.exp(v2 - v1)
    c0 = gw / (1.0 + t)
    c1 = gw * t / (1.0 + t)
    e0 = i1 - N_GROUPS
    e1 = i2 - N_GROUPS

    oh0 = lane_f == e0
    oh1 = lane_f == e1
    cnt = oh0.astype(F32) + oh1.astype(F32)
    ri = lax.broadcasted_iota(jnp.int32, (TM, TM), 0)
    cj = lax.broadcasted_iota(jnp.int32, (TM, TM), 1)
    strict = (cj < ri).astype(BF16)
    before = jnp.dot(strict, cnt.astype(BF16), preferred_element_type=F32) + carry[0:1, :]
    r0 = jnp.sum(jnp.where(oh0, before, 0.0), axis=-1, keepdims=True)
    r1 = jnp.sum(jnp.where(oh1, before, 0.0), axis=-1, keepdims=True)
    carry[0:1, :] = carry[0:1, :] + jnp.sum(cnt, axis=0, keepdims=True)

    mi = jnp.where(lane == 0, e0, jnp.where(lane == 1, e1, jnp.where(lane == 2, r0, jnp.where(lane == 3, r1, 0.0))))
    mi_ref[...] = mi.astype(jnp.int32)
    mf_ref[...] = jnp.where(lane == 0, c0, jnp.where(lane == 1, c1, 0.0))
    cnt_ref[...] = carry[...]


def _router(x, mod_l, w_route, b_route, latent_only):
    mr = _mod_row(latent_only)
    n_tiles = _n_tiles(latent_only)
    n_tok = n_tiles * TM
    row_blk = lambda w: pl.BlockSpec((TM, w), lambda i: (i, 0))
    return pl.pallas_call(
        _router_kernel,
        grid=(n_tiles,),
        in_specs=[
            row_blk(D_MODEL),
            pl.BlockSpec((1, 6, D_MODEL), lambda i: (mr(i), 0, 0)),
            pl.BlockSpec((D_MODEL, LANES), lambda i: (0, 0)),
            pl.BlockSpec((1, LANES), lambda i: (0, 0)),
        ],
        out_specs=[row_blk(LANES), row_blk(LANES), pl.BlockSpec((8, LANES), lambda i: (0, 0))],
        out_shape=[jax.ShapeDtypeStruct((n_tok, LANES), jnp.int32),
                   jax.ShapeDtypeStruct((n_tok, LANES), F32),
                   jax.ShapeDtypeStruct((8, LANES), F32)],
        scratch_shapes=[pltpu.VMEM((8, LANES), F32)],
        compiler_params=_cparams(("arbitrary",)),
        name="moe_router",
    )(x, mod_l, w_route, b_route)


def _each_row(fn):
    def body(g, carry):
        for j in range(ROW_UNROLL):
            for k in range(2):
                fn(g * ROW_UNROLL + j, k)
        return carry
    lax.fori_loop(0, TM // ROW_UNROLL, body, 0)


def _dispatch_kernel(pad_end_ref, padded_ref, x_ref, mod_ref, dest_hbm, xs_hbm,
                     idx_smem, u_scr, sem_idx, sem_row, sem_zero):
    i = pl.program_id(0)
    slot = i % 2

    @pl.when(i == 0)
    def _():
        u_scr[1] = jnp.zeros((TM, D_MODEL), F32)

        def zero_block(first_row):
            rows = pl.ds(pl.multiple_of(first_row, MOE_ROWS), MOE_ROWS)
            return pltpu.make_async_copy(u_scr.at[1], xs_hbm.at[rows], sem_zero)

        n_rows = xs_hbm.shape[0]
        total = pad_end_ref[N_EXPERTS - 1]
        for e in range(N_EXPERTS):
            pl.when(padded_ref[e] > 0)(lambda e=e: zero_block(pad_end_ref[e] - MOE_ROWS).start())
            pl.when(total + e * MOE_ROWS < n_rows)(lambda e=e: zero_block(total + e * MOE_ROWS).start())
        for e in range(N_EXPERTS):
            pl.when(padded_ref[e] > 0)(lambda e=e: zero_block(0).wait())
            pl.when(total + e * MOE_ROWS < n_rows)(lambda e=e: zero_block(0).wait())

    idx_copy = pltpu.make_async_copy(dest_hbm.at[pl.ds(i * (2 * TM), 2 * TM)], idx_smem.at[slot], sem_idx)
    idx_copy.start()
    u_scr[slot] = x_ref[...] * (1.0 + mod_ref[0, 4:5, :]) + mod_ref[0, 3:4, :]
    idx_copy.wait()

    def row_copy(sl, r, dst_row):
        return pltpu.make_async_copy(u_scr.at[sl, pl.ds(r, 1)], xs_hbm.at[pl.ds(dst_row, 1)], sem_row.at[sl])

    _each_row(lambda r, k: row_copy(slot, r, idx_smem[slot, 2 * r + k]).start())

    @pl.when(i > 0)
    def _():
        _each_row(lambda r, k: row_copy(1 - slot, r, 0).wait())

    @pl.when(i == pl.num_programs(0) - 1)
    def _():
        _each_row(lambda r, k: row_copy(slot, r, 0).wait())


def _dispatch(pad_end, padded, x1, mod_l, dest, n_blocks, latent_only):
    mr = _mod_row(latent_only)
    return pl.pallas_call(
        _dispatch_kernel,
        grid_spec=pltpu.PrefetchScalarGridSpec(
            num_scalar_prefetch=2,
            grid=(_n_tiles(latent_only),),
            in_specs=[
                pl.BlockSpec((TM, D_MODEL), lambda i, pe, pd: (i, 0)),
                pl.BlockSpec((1, 6, D_MODEL), lambda i, pe, pd: (mr(i), 0, 0)),
                pl.BlockSpec(memory_space=pl.ANY),
            ],
            out_specs=pl.BlockSpec(memory_space=pl.ANY),
            scratch_shapes=[
                pltpu.SMEM((2, 2 * TM), jnp.int32),
                pltpu.VMEM((2, TM, D_MODEL), F32),
                pltpu.SemaphoreType.DMA(()),
                pltpu.SemaphoreType.DMA((2,)),
                pltpu.SemaphoreType.DMA(()),
            ],
        ),
        out_shape=jax.ShapeDtypeStruct((n_blocks * MOE_ROWS, D_MODEL), F32),
        compiler_params=_cparams(("arbitrary",)),
        name="moe_dispatch",
    )(pad_end, padded, x1, mod_l, dest)


def _expert_kernel(blk_exp_ref, n_used_ref, x_ref, wg_ref, wu_ref, wd_ref, o_ref):
    used = pl.program_id(0) < n_used_ref[0]

    @pl.when(used)
    def _():
        x = x_ref[...].astype(BF16)
        gate = jnp.dot(x, wg_ref[0, 0].astype(BF16), preferred_element_type=F32)
        up = jnp.dot(x, wu_ref[0, 0].astype(BF16), preferred_element_type=F32)
        hid = (_silu(gate) * up).astype(BF16)
        o_ref[...] = jnp.dot(hid, wd_ref[0, 0].astype(BF16), preferred_element_type=F32)

    @pl.when(jnp.logical_not(used))
    def _():
        o_ref[...] = jnp.zeros_like(o_ref)


def _experts(blk_exp, n_used, xs, layer, w_gate, w_up, w_down):
    n_blocks = xs.shape[0] // MOE_ROWS
    row_in = pl.BlockSpec((MOE_ROWS, D_MODEL), lambda i, be, nu: (jnp.minimum(i, nu[0] - 1), 0))
    w_in = pl.BlockSpec((1, 1, D_MODEL, D_EXPERT), lambda i, be, nu: (layer, be[i], 0, 0))
    w_out = pl.BlockSpec((1, 1, D_EXPERT, D_MODEL), lambda i, be, nu: (layer, be[i], 0, 0))
    return pl.pallas_call(
        _expert_kernel,
        grid_spec=pltpu.PrefetchScalarGridSpec(
            num_scalar_prefetch=2,
            grid=(n_blocks,),
            in_specs=[row_in, w_in, w_in, w_out],
            out_specs=pl.BlockSpec((MOE_ROWS, D_MODEL), lambda i, be, nu: (i, 0)),
        ),
        out_shape=jax.ShapeDtypeStruct(xs.shape, F32),
        compiler_params=_cparams(("arbitrary",)),
        name="moe_experts",
    )(blk_exp, n_used, xs, w_gate, w_up, w_down)


def _combine_ln_kernel(ys_hbm, dest_hbm, mf_ref, x_ref, mod_ref, g_ref, b_ref, o_ref,
                       idx_smem, y_buf, sem_idx, sem_row):
    i = pl.program_id(0)
    slot = i % 2

    def row_copy(sl, r, k, src_row):
        return pltpu.make_async_copy(ys_hbm.at[pl.ds(src_row, 1)], y_buf.at[sl, k, pl.ds(r, 1)], sem_row.at[sl])

    def request(tile, sl):
        idx_copy = pltpu.make_async_copy(dest_hbm.at[pl.ds(tile * (2 * TM), 2 * TM)], idx_smem.at[sl], sem_idx)
        idx_copy.start()
        idx_copy.wait()
        _each_row(lambda r, k: row_copy(sl, r, k, idx_smem[sl, 2 * r + k]).start())

    pl.when(i == 0)(lambda: request(0, 0))
    pl.when(i + 1 < pl.num_programs(0))(lambda: request(i + 1, 1 - slot))
    _each_row(lambda r, k: row_copy(slot, r, k, 0).wait())

    mf = mf_ref[...]
    y = mf[:, 0:1] * y_buf[slot, 0] + mf[:, 1:2] * y_buf[slot, 1]
    r = DEEPNORM_ALPHA * x_ref[...] + mod_ref[0, 5:6, :] * y
    o_ref[...] = _layer_norm(r, g_ref[...], b_ref[...])


def _combine_ln(ys, dest, mf, x1, mod_l, ln_g, ln_b, latent_only):
    mr = _mod_row(latent_only)
    n_tiles = _n_tiles(latent_only)
    return pl.pallas_call(
        _combine_ln_kernel,
        grid=(n_tiles,),
        in_specs=[
            pl.BlockSpec(memory_space=pl.ANY),
            pl.BlockSpec(memory_space=pl.ANY),
            pl.BlockSpec((TM, LANES), lambda i: (i, 0)),
            pl.BlockSpec((TM, D_MODEL), lambda i: (i, 0)),
            pl.BlockSpec((1, 6, D_MODEL), lambda i: (mr(i), 0, 0)),
            pl.BlockSpec((1, D_MODEL), lambda i: (0, 0)),
            pl.BlockSpec((1, D_MODEL), lambda i: (0, 0)),
        ],
        out_specs=pl.BlockSpec((TM, D_MODEL), lambda i: (i, 0)),
        out_shape=jax.ShapeDtypeStruct((n_tiles * TM, D_MODEL), F32),
        scratch_shapes=[
            pltpu.SMEM((2, 2 * TM), jnp.int32),
            pltpu.VMEM((2, 2, TM, D_MODEL), F32),
            pltpu.SemaphoreType.DMA(()),
            pltpu.SemaphoreType.DMA((2,)),
        ],
        compiler_params=_cparams(("arbitrary",)),
        name="combine_ln",
    )(ys, dest, mf, x1, mod_l, ln_g, ln_b)


def _moe(x1, mod_l, layer, w_grp, b_grp, w_rexp, b_rexp, w_gate, w_up, w_down, ln_g, ln_b, latent_only):
    pad = LANES - N_GROUPS - N_EXPERTS
    w_route = jnp.concatenate([w_grp, w_rexp, jnp.zeros((D_MODEL, pad), F32)], axis=1)
    b_route = jnp.concatenate([b_grp, b_rexp, jnp.zeros((pad,), F32)])[None, :]
    mi, mf, cnt = _router(x1, mod_l, w_route, b_route, latent_only)

    counts = cnt[0, :N_EXPERTS].astype(jnp.int32)
    padded = (counts + MOE_ROWS - 1) // MOE_ROWS * MOE_ROWS
    pad_end = jnp.cumsum(padded)
    pad_start = pad_end - padded
    n_tok = x1.shape[0]
    n_blocks = (2 * n_tok) // MOE_ROWS + N_EXPERTS
    experts = jnp.arange(N_EXPERTS, dtype=jnp.int32)
    start_of = jnp.sum(jnp.where(mi[:, 0:2, None] == experts, pad_start, 0), axis=-1)
    dest = (start_of + mi[:, 2:4]).reshape(-1)
    blk_start = jnp.arange(n_blocks, dtype=jnp.int32) * MOE_ROWS
    blk_exp = jnp.minimum(jnp.sum((pad_end[None, :] <= blk_start[:, None]).astype(jnp.int32), axis=1),
                          N_EXPERTS - 1)
    n_used = pad_end[-1:] // MOE_ROWS
    xs = _dispatch(pad_end, padded, x1, mod_l, dest, n_blocks, latent_only)
    ys = _experts(blk_exp, n_used, xs, layer, w_gate, w_up, w_down)
    return _combine_ln(ys, dest, mf, x1, mod_l, ln_g, ln_b, latent_only)


def kernel(x, c, ctx, c_ctx, ada_w, ada_b, ln1_g, ln1_b, ln2_g, ln2_b, ab_w_in, ab_w_lr_f, ab_b_lr_f, ab_w_lr_b, ab_b_lr_b, ab_gn_a, ab_gn_b, ab_w_out, c_w_qkv, c_lq1, c_lk1, c_lq2, c_lk2, c_subln_g, c_w_out, moe_w_grp, moe_b_grp, moe_w_rexp, moe_b_rexp, moe_w_gate, moe_w_up, moe_w_down):
    assert x.shape == (BATCH, SEQ, D_MODEL) and ctx.shape == (BATCH, CTX_LEN, D_MODEL)
    xs = jnp.concatenate([ctx, x], axis=1).reshape(N_ALL, D_MODEL)
    c_all = jnp.concatenate([c, c_ctx[None, :], jnp.zeros((16 - BATCH - 1, D_MODEL), F32)], axis=0)
    mod = _ada_tables(c_all, ada_w, ada_b).reshape(DEPTH, 16, 6, D_MODEL)

    rope_a = _rope_tables(DK_A, 1)
    rope_c = _rope_tables(DH_C, 2)
    dec_a = _retention_tables()

    for l in range(DEPTH):
        last = l == DEPTH - 1
        i = l // 2
        mod_l = mod[l]
        row = lambda v: v[None, :]
        if l % 2 == 0:
            assert not last
            w_in = jnp.pad(ab_w_in[i], ((0, 0), (0, AB_COLS - ab_w_in.shape[2]))).astype(BF16)
            z = _mod_matmul(xs, mod_l, w_in)
            ya = _scan_a(z, *rope_a, dec_a, row(ab_gn_a[i]))
            wf = ab_w_lr_f[i].reshape(GLA_RANK, H_B // 2, LANES)
            wb = ab_w_lr_b[i].reshape(GLA_RANK, H_B // 2, LANES)
            wlr = jnp.zeros((H_B // 2, LANES, 2 * LANES), F32)
            wlr = wlr.at[:, 0:GLA_RANK, 0:LANES].set(jnp.swapaxes(wf, 0, 1))
            wlr = wlr.at[:, GLA_RANK:2 * GLA_RANK, LANES:].set(jnp.swapaxes(wb, 0, 1))
            blr = jnp.concatenate([ab_b_lr_f[i].reshape(H_B // 2, 1, LANES),
                                   ab_b_lr_b[i].reshape(H_B // 2, 1, LANES)], axis=-1)
            yb = _scan_b(z, wlr, blr, row(ab_gn_b[i]))
            x1 = _proj_ln(ya, yb, 0, ab_w_out[i].astype(BF16), xs, mod_l, row(ln1_g[l]), row(ln1_b[l]), False)
        else:
            lam_init = 0.8 - 0.6 * math.exp(-0.3 * l)
            lam = (jnp.exp(jnp.sum(c_lq1[i] * c_lk1[i], axis=-1))
                   - jnp.exp(jnp.sum(c_lq2[i] * c_lk2[i], axis=-1))).astype(F32) + lam_init
            z = _mod_matmul(xs, mod_l, c_w_qkv[i].astype(BF16))
            gsub = row(c_subln_g[i])
            y = _attention(z, lam, *rope_c, gsub, 1.0 - lam_init, last)
            x1 = _proj_ln(y, y, 1, c_w_out[i].astype(BF16), xs, mod_l, row(ln1_g[l]), row(ln1_b[l]), last)
        xs = _moe(x1, mod_l, l, moe_w_grp[l], moe_b_grp[l], moe_w_rexp[l], moe_b_rexp[l],
                  moe_w_gate, moe_w_up, moe_w_down, row(ln2_g[l]), row(ln2_b[l]), last)
    return xs.reshape(BATCH, SEQ, D_MODEL)
```

```python
import functools
import math

import numpy as np
import jax
import jax.numpy as jnp
from jax import lax
from jax.experimental import pallas as pl
from jax.experimental.pallas import tpu as pltpu

F32 = jnp.float32
BF16 = jnp.bfloat16

D_MODEL = 1024
BATCH = 8
SEQ = 2048
DEPTH = 4
GRID_W = 64
CTX_LEN = 256
ROPE_BASE = 10000.0
LN_EPS = 1e-5
DEEPNORM_ALPHA = (2 * DEPTH) ** 0.25
H_A = 4
DK_A = 128
DV_A = 128
CHUNK_A = 128
RET_EXP_FWD = 5.0
RET_EXP_BWD = 5.5
H_B = 4
DK_B = 64
DV_B = 128
GLA_RANK = 16
GLA_TAU = 16.0
CHUNK_B = 64
H_C = 8
DH_C = 64
DV_C = 128
N_GROUPS = 4
EXPERTS_PER_GROUP = 8
N_EXPERTS = 32
D_EXPERT = 512

LANES = 128
T_ALL = CTX_LEN + SEQ
N_ALL = BATCH * T_ALL
TM = 256
TILES_PER_BATCH = T_ALL // TM
LATENT_TILES_PER_BATCH = SEQ // TM
AB_COLS = 29 * LANES
MOE_ROWS = TM
ROW_UNROLL = 8
SCAN_UNROLL = 4
GLA_GROUP = 256
TQ = 256
VMEM_LIMIT = 56 * 1024 * 1024


def _cparams(sem):
    return pltpu.CompilerParams(dimension_semantics=sem, vmem_limit_bytes=VMEM_LIMIT)


def _silu(v):
    return v * (1.0 / (1.0 + jnp.exp(-v)))


def _n_tiles(latent_only):
    return BATCH * (LATENT_TILES_PER_BATCH if latent_only else TILES_PER_BATCH)


def _row_tile(latent_only):
    if latent_only:
        return lambda i: (i // LATENT_TILES_PER_BATCH) * TILES_PER_BATCH + 1 + i % LATENT_TILES_PER_BATCH
    return lambda i: i


def _mod_row(latent_only):
    if latent_only:
        return lambda i: i // LATENT_TILES_PER_BATCH
    return lambda i: jnp.where(i % TILES_PER_BATCH == 0, BATCH, i // TILES_PER_BATCH)


def _ada_kernel(c_ref, w_ref, b_ref, o_ref):
    sc = _silu(c_ref[...])
    o_ref[0] = jnp.dot(sc.astype(BF16), w_ref[0].astype(BF16), preferred_element_type=F32) + b_ref[0]


def _ada_tables(c_all, ada_w, ada_b):
    tn = 1536
    n_out = 6 * D_MODEL
    return pl.pallas_call(
        _ada_kernel,
        grid=(DEPTH, n_out // tn),
        in_specs=[
            pl.BlockSpec((16, D_MODEL), lambda l, j: (0, 0)),
            pl.BlockSpec((1, D_MODEL, tn), lambda l, j: (l, 0, j)),
            pl.BlockSpec((1, 1, tn), lambda l, j: (l, 0, j)),
        ],
        out_specs=pl.BlockSpec((1, 16, tn), lambda l, j: (l, 0, j)),
        out_shape=jax.ShapeDtypeStruct((DEPTH, 16, n_out), F32),
        compiler_params=_cparams(("arbitrary", "arbitrary")),
        name="ada_tables",
    )(c_all, ada_w, ada_b.reshape(DEPTH, 1, n_out))


def _modmm_kernel(x_ref, mod_ref, w_ref, o_ref):
    u = x_ref[...] * (1.0 + mod_ref[0, 1:2, :]) + mod_ref[0, 0:1, :]
    o_ref[...] = jnp.dot(u.astype(BF16), w_ref[...], preferred_element_type=F32)


def _mod_matmul(x, mod_l, w_bf16):
    n_out = w_bf16.shape[1]
    return pl.pallas_call(
        _modmm_kernel,
        grid=(N_ALL // TM,),
        in_specs=[
            pl.BlockSpec((TM, D_MODEL), lambda i: (i, 0)),
            pl.BlockSpec((1, 6, D_MODEL), lambda i: (_mod_row(False)(i), 0, 0)),
            pl.BlockSpec((D_MODEL, n_out), lambda i: (0, 0)),
        ],
        out_specs=pl.BlockSpec((TM, n_out), lambda i: (i, 0)),
        out_shape=jax.ShapeDtypeStruct((N_ALL, n_out), F32),
        compiler_params=_cparams(("arbitrary",)),
        name="mod_matmul",
    )(x, mod_l, w_bf16)


def _rope_tables(head_dim, reps):
    rows = SEQ // GRID_W
    row = np.repeat(np.arange(rows, dtype=np.float32), GRID_W)
    col = np.tile(np.arange(GRID_W, dtype=np.float32), rows)
    quarter = head_dim // 4
    inv = (ROPE_BASE ** (-np.arange(quarter, dtype=np.float32) / quarter)).astype(np.float32)
    ang_r = row[:, None] * inv
    ang_c = col[:, None] * inv
    ang = np.concatenate([ang_r, ang_r, ang_c, ang_c], axis=-1)
    cos = np.cos(ang).astype(np.float32)
    sin = np.sin(ang).astype(np.float32)
    q_idx = (np.arange(head_dim) // quarter) % 2
    sin_up = np.where(q_idx == 1, sin, 0.0).astype(np.float32)
    sin_dn = np.where(q_idx == 0, -sin, 0.0).astype(np.float32)

    def full(t, ctx_val):
        t = np.tile(t, (1, reps))
        return jnp.asarray(np.concatenate([np.full((CTX_LEN, t.shape[1]), ctx_val, np.float32), t], axis=0))

    return full(cos, 1.0), full(sin_up, 0.0), full(sin_dn, 0.0)


def _rope(x, cos, sin_up, sin_dn, quarter):
    width = x.shape[-1]
    return x * cos + pltpu.roll(x, quarter, 1) * sin_up + pltpu.roll(x, width - quarter, 1) * sin_dn


def _dot_tb(a, b):
    return lax.dot_general(a, b, (((1,), (1,)), ((), ())), preferred_element_type=F32)


def _dot_ta(a, b):
    return lax.dot_general(a, b, (((0,), (0,)), ((), ())), preferred_element_type=F32)


def _split_bf16(x, parts):
    out = []
    for _ in range(parts):
        t = x.astype(BF16)
        out.append(t)
        x = x - t.astype(F32)
    return out


def _dot_split(a, b):
    a_hi, a_lo = _split_bf16(a, 2)
    b_hi, b_lo = _split_bf16(b, 2)
    dot = lambda u, v: jnp.dot(u, v, preferred_element_type=F32)
    return dot(a_hi, b_hi) + (dot(a_hi, b_lo) + dot(a_lo, b_hi))


def _dot_mask(mask_bf16, x):
    return sum(jnp.dot(mask_bf16, t, preferred_element_type=F32) for t in reversed(_split_bf16(x, 3)))


def _retention_tables():
    c = CHUNK_A
    i = np.arange(c, dtype=np.float64)
    out = np.zeros((H_A, 7, c, LANES), np.float64)
    for h in range(H_A):
        lgf = np.log1p(-np.exp2(-(RET_EXP_FWD + h)))
        lgb = np.log1p(-np.exp2(-(RET_EXP_BWD + h)))
        d = i[:, None] - i[None, :]
        out[h, 0] = np.where(d >= 0, np.exp(lgf * d), np.exp(lgb * (-d - 1)))
        out[h, 1] = np.exp(lgf * (i + 1))[:, None]
        out[h, 2] = np.exp(lgb * (c - 1 - i))[:, None]
        out[h, 3] = np.exp(lgf * (c - 1 - i))[:, None]
        out[h, 4] = np.exp(lgb * i)[:, None]
        out[h, 5] = np.exp(lgf * c)
        out[h, 6] = np.exp(lgb * c)
    return jnp.asarray(out.astype(np.float32))


def _scan_a_kernel(q_ref, k_ref, v_ref, g_ref, cos_ref, sup_ref, sdn_ref, dec_ref, gn_ref, o_ref,
                   kr_scr, sb_scr):
    c = CHUNK_A
    n_ctx = CTX_LEN // c
    n_all = T_ALL // c
    scale = DK_A ** -0.5
    kr_scr[...] = _rope(k_ref[...], cos_ref[...], sup_ref[...], sdn_ref[...], DK_A // 4)
    dmat = dec_ref[0, 0]
    q_f, q_b, k_f, k_b = dec_ref[0, 1], dec_ref[0, 2], dec_ref[0, 3], dec_ref[0, 4]
    g_fc, g_bc = dec_ref[0, 5], dec_ref[0, 6]
    gn = gn_ref[...]
    zero = jnp.zeros((DK_A, DV_A), F32)

    def chunk(ci):
        return pl.ds(pl.multiple_of(ci * c, c), c)

    def kv_state(ci, k_dec):
        sl = chunk(ci)
        return _dot_ta((kr_scr[sl, :] * k_dec).astype(BF16), v_ref[sl, :].astype(BF16))

    def run(lo, hi, sf0, sb0):
        def bwd(j, sb):
            ci = hi - 1 - j
            sb_scr[ci] = sb
            return g_bc * sb + kv_state(ci, k_b)

        sb_fin = lax.fori_loop(0, hi - lo, bwd, sb0, unroll=SCAN_UNROLL)

        def fwd(j, sf):
            ci = lo + j
            sl = chunk(ci)
            q = _rope(q_ref[sl, :], cos_ref[sl, :], sup_ref[sl, :], sdn_ref[sl, :], DK_A // 4) * scale
            k = kr_scr[sl, :]
            vb = v_ref[sl, :].astype(BF16)
            att = _dot_tb(q.astype(BF16), k.astype(BF16)) * dmat
            o = jnp.dot(att.astype(BF16), vb, preferred_element_type=F32)
            o = o + jnp.dot((q * q_f).astype(BF16), sf.astype(BF16), preferred_element_type=F32)
            o = o + jnp.dot((q * q_b).astype(BF16), sb_scr[ci].astype(BF16), preferred_element_type=F32)
            o = o - jnp.mean(o, axis=-1, keepdims=True)
            o = o * lax.rsqrt(jnp.mean(o * o, axis=-1, keepdims=True) + LN_EPS)
            o_ref[sl, :] = _silu(g_ref[sl, :]) * (o * gn)
            return g_fc * sf + _dot_ta((k * k_f).astype(BF16), vb)

        sf_fin = lax.fori_loop(0, hi - lo, fwd, sf0, unroll=SCAN_UNROLL)
        return sf_fin, sb_fin

    sf_c, sb_c = run(0, n_ctx, zero, zero)
    run(n_ctx, n_all, sf_c, sb_c)


def _scan_a(z, cos, sup, sdn, dec, gn_a):
    blk = lambda col0: pl.BlockSpec((T_ALL, LANES), lambda b, h: (b, col0 + h))
    tbl = pl.BlockSpec((T_ALL, LANES), lambda b, h: (0, 0))
    return pl.pallas_call(
        _scan_a_kernel,
        grid=(BATCH, H_A),
        in_specs=[blk(0), blk(4), blk(8), blk(12), tbl, tbl, tbl,
                  pl.BlockSpec((1, 7, CHUNK_A, LANES), lambda b, h: (h, 0, 0, 0)),
                  pl.BlockSpec((1, LANES), lambda b, h: (0, h))],
        out_specs=pl.BlockSpec((T_ALL, LANES), lambda b, h: (b, h)),
        out_shape=jax.ShapeDtypeStruct((N_ALL, H_A * DV_A), F32),
        scratch_shapes=[pltpu.VMEM((T_ALL, LANES), F32),
                        pltpu.VMEM((T_ALL // CHUNK_A, DK_A, DV_A), F32)],
        compiler_params=_cparams(("arbitrary", "arbitrary")),
        name="scan_retention",
    )(z, z, z, z, cos, sup, sdn, dec, gn_a)


def _log_sigmoid(g):
    return jnp.minimum(g, 0.0) - jnp.log1p(jnp.exp(-jnp.abs(g)))


def _scan_b_kernel(q_ref, k_ref, v_ref, g_ref, lr_ref, wlr_ref, blr_ref, gn_ref, o_ref,
                   qf_scr, kf_scr, qb_scr, kb_scr, ktf_scr, ktb_scr, ef_scr, eb_scr, sb_scr):
    c = CHUNK_B
    n_ctx = CTX_LEN // c
    n_all = T_ALL // c
    per_group = GLA_GROUP // c
    scale = DK_B ** -0.5

    gi_r = lax.broadcasted_iota(jnp.int32, (GLA_GROUP, GLA_GROUP), 0)
    gi_c = lax.broadcasted_iota(jnp.int32, (GLA_GROUP, GLA_GROUP), 1)
    same_chunk = (gi_r // c) == (gi_c // c)
    prefix = (same_chunk & (gi_c <= gi_r)).astype(BF16)
    suffix = (same_chunk & (gi_c >= gi_r)).astype(BF16)

    def prepare(gi, carry):
        sl = pl.ds(pl.multiple_of(gi * GLA_GROUP, GLA_GROUP), GLA_GROUP)
        gates = _dot_split(lr_ref[sl, :], wlr_ref[0]) + blr_ref[0]
        laf = _log_sigmoid(gates[:, :LANES]) * (1.0 / GLA_TAU)
        lab = _log_sigmoid(gates[:, LANES:]) * (1.0 / GLA_TAU)
        b = _dot_mask(prefix, laf)
        rb = _dot_mask(suffix, lab)
        q = q_ref[sl, :] * scale
        k = k_ref[sl, :]
        qf_scr[sl, :] = (q * jnp.exp(b)).astype(BF16)
        kf_scr[sl, :] = (k * jnp.exp(-b)).astype(BF16)
        qb_scr[sl, :] = (q * jnp.exp(rb - lab)).astype(BF16)
        kb_scr[sl, :] = (k * jnp.exp(-rb)).astype(BF16)
        b3 = b.reshape(per_group, c, LANES)
        rb3 = rb.reshape(per_group, c, LANES)
        k3 = k.reshape(per_group, c, LANES)
        b_tot = b3[:, c - 1:c, :]
        rb_tot = rb3[:, 0:1, :]
        ktf_scr[sl, :] = (k3 * jnp.exp(b_tot - b3)).reshape(GLA_GROUP, LANES).astype(BF16)
        ktb_scr[sl, :] = (k3 * jnp.exp(rb_tot - rb3)).reshape(GLA_GROUP, LANES).astype(BF16)
        for m in range(per_group):
            ef_scr[gi * per_group + m] = jnp.broadcast_to(jnp.exp(b_tot[m]), (8, LANES))
            eb_scr[gi * per_group + m] = jnp.broadcast_to(jnp.exp(rb_tot[m]), (8, LANES))
        return carry

    lax.fori_loop(0, T_ALL // GLA_GROUP, prepare, 0)

    lane = lax.broadcasted_iota(jnp.int32, (1, LANES), 1)
    masks = [lane < DK_B, lane >= DK_B]
    ri = lax.broadcasted_iota(jnp.int32, (c, c), 0)
    cj = lax.broadcasted_iota(jnp.int32, (c, c), 1)
    lower = cj <= ri
    gn = gn_ref[...]
    zero = jnp.zeros((DV_B, LANES), F32)
    zero_b = jnp.zeros((), BF16)

    def chunk(ci):
        return pl.ds(pl.multiple_of(ci * c, c), c)

    def run(lo, hi, sf0, sb0):
        def bwd(j, sb):
            ci = hi - 1 - j
            sl = chunk(ci)
            e_tot = eb_scr[ci][0:1, :]
            kt = ktb_scr[sl, :]
            v = v_ref[sl, :]
            new = []
            for h in range(2):
                sb_scr[ci, h] = sb[h]
                vh = v[:, h * DV_B:(h + 1) * DV_B].astype(BF16)
                new.append(sb[h] * e_tot + _dot_ta(vh, jnp.where(masks[h], kt, zero_b)))
            return tuple(new)

        sb_fin = lax.fori_loop(0, hi - lo, bwd, sb0, unroll=SCAN_UNROLL)

        def fwd(j, sf):
            ci = lo + j
            sl = chunk(ci)
            e_tot = ef_scr[ci][0:1, :]
            qf, kf, qb, kb, kt = qf_scr[sl, :], kf_scr[sl, :], qb_scr[sl, :], kb_scr[sl, :], ktf_scr[sl, :]
            v = v_ref[sl, :]
            g = g_ref[sl, :]
            new = []
            for h in range(2):
                pick = lambda t: jnp.where(masks[h], t, zero_b)
                vh = v[:, h * DV_B:(h + 1) * DV_B].astype(BF16)
                qfh, qbh = pick(qf), pick(qb)
                att = jnp.where(lower, _dot_tb(qfh, pick(kf)), _dot_tb(qbh, pick(kb)))
                o = jnp.dot(att.astype(BF16), vh, preferred_element_type=F32)
                o = o + _dot_tb(qfh, sf[h].astype(BF16))
                o = o + _dot_tb(qbh, sb_scr[ci, h].astype(BF16))
                o = o * lax.rsqrt(jnp.mean(o * o, axis=-1, keepdims=True) + LN_EPS)
                cols = slice(h * DV_B, (h + 1) * DV_B)
                o_ref[sl, cols] = _silu(g[:, cols]) * (o * gn[:, cols])
                new.append(sf[h] * e_tot + _dot_ta(vh, pick(kt)))
            return tuple(new)

        sf_fin = lax.fori_loop(0, hi - lo, fwd, sf0, unroll=SCAN_UNROLL)
        return sf_fin, sb_fin

    sf_c, sb_c = run(0, n_ctx, (zero, zero), (zero, zero))
    run(n_ctx, n_all, sf_c, sb_c)


def _scan_b(z, wlr, blr, gn_b):
    pairs = H_B // 2
    return pl.pallas_call(
        _scan_b_kernel,
        grid=(BATCH, pairs),
        in_specs=[
            pl.BlockSpec((T_ALL, LANES), lambda b, p: (b, 16 + p)),
            pl.BlockSpec((T_ALL, LANES), lambda b, p: (b, 18 + p)),
            pl.BlockSpec((T_ALL, 2 * DV_B), lambda b, p: (b, 10 + p)),
            pl.BlockSpec((T_ALL, 2 * DV_B), lambda b, p: (b, 12 + p)),
            pl.BlockSpec((T_ALL, LANES), lambda b, p: (b, 28)),
            pl.BlockSpec((1, LANES, 2 * LANES), lambda b, p: (p, 0, 0)),
            pl.BlockSpec((1, 1, 2 * LANES), lambda b, p: (p, 0, 0)),
            pl.BlockSpec((1, 2 * DV_B), lambda b, p: (0, p)),
        ],
        out_specs=pl.BlockSpec((T_ALL, 2 * DV_B), lambda b, p: (b, p)),
        out_shape=jax.ShapeDtypeStruct((N_ALL, H_B * DV_B), F32),
        scratch_shapes=[pltpu.VMEM((T_ALL, LANES), BF16)] * 6
                       + [pltpu.VMEM((T_ALL // CHUNK_B, 8, LANES), F32)] * 2
                       + [pltpu.VMEM((T_ALL // CHUNK_B, 2, DV_B, LANES), F32)],
        compiler_params=_cparams(("arbitrary", "arbitrary")),
        name="scan_gla",
    )(z, z, z, z, z, wlr, blr, gn_b)


def _attn_kernel(lam_ref, q_ref, k_ref, v_ref, qcos_ref, qsup_ref, qsdn_ref, kcos_ref, ksup_ref, ksdn_ref,
                 gsub_ref, o_ref, k_scr, v_scr, *, post_scale, tile0):
    h = pl.program_id(1)
    t = pl.program_id(2)
    quarter = DH_C // 4
    scale = DH_C ** -0.5

    @pl.when(t == 0)
    def _():
        k_scr[...] = _rope(k_ref[...], kcos_ref[...], ksup_ref[...], ksdn_ref[...], quarter).astype(BF16)
        v_scr[...] = v_ref[...].astype(BF16)

    lam = lam_ref[h]
    lane = lax.broadcasted_iota(jnp.int32, (1, LANES), 1)
    m1 = (lane < DH_C).astype(F32)
    m2 = (lane >= DH_C).astype(F32)

    def attend(n_keys):
        q = _rope(q_ref[...], qcos_ref[...], qsup_ref[...], qsdn_ref[...], quarter) * scale
        kb = k_scr[0:n_keys, :]
        s1 = _dot_tb((q * m1).astype(BF16), kb)
        s2 = _dot_tb((q * m2).astype(BF16), kb)
        e1 = jnp.exp(s1 - jnp.max(s1, axis=-1, keepdims=True))
        e2 = jnp.exp(s2 - jnp.max(s2, axis=-1, keepdims=True))
        r1 = 1.0 / jnp.sum(e1, axis=-1, keepdims=True)
        r2 = lam / jnp.sum(e2, axis=-1, keepdims=True)
        a = e1 * r1 - e2 * r2
        o = jnp.dot(a.astype(BF16), v_scr[0:n_keys, :], preferred_element_type=F32)
        o = o * lax.rsqrt(jnp.mean(o * o, axis=-1, keepdims=True) + LN_EPS)
        o_ref[...] = o * (gsub_ref[...] * post_scale)

    if tile0 == 0:
        pl.when(t == 0)(lambda: attend(CTX_LEN))
        pl.when(t > 0)(lambda: attend(T_ALL))
    else:
        attend(T_ALL)


def _attention(z, lam, cos, sup, sdn, gsub, post_scale, latent_only):
    tile0 = 1 if latent_only else 0
    n_qt = TILES_PER_BATCH - tile0
    kern = functools.partial(_attn_kernel, post_scale=post_scale, tile0=tile0)
    kv = lambda col0: pl.BlockSpec((T_ALL, LANES), lambda b, h, t, lam_r: (b, col0 + h))
    q_tbl = pl.BlockSpec((TQ, LANES), lambda b, h, t, lam_r: (tile0 + t, 0))
    k_tbl = pl.BlockSpec((T_ALL, LANES), lambda b, h, t, lam_r: (0, 0))
    return pl.pallas_call(
        kern,
        grid_spec=pltpu.PrefetchScalarGridSpec(
            num_scalar_prefetch=1,
            grid=(BATCH, H_C, n_qt),
            in_specs=[pl.BlockSpec((TQ, LANES), lambda b, h, t, lam_r: (b * TILES_PER_BATCH + tile0 + t, h)),
                      kv(H_C), kv(2 * H_C), q_tbl, q_tbl, q_tbl, k_tbl, k_tbl, k_tbl,
                      pl.BlockSpec((1, LANES), lambda b, h, t, lam_r: (0, h))],
            out_specs=pl.BlockSpec((TQ, LANES), lambda b, h, t, lam_r: (b * n_qt + t, h)),
            scratch_shapes=[pltpu.VMEM((T_ALL, LANES), BF16), pltpu.VMEM((T_ALL, LANES), BF16)],
        ),
        out_shape=jax.ShapeDtypeStruct((BATCH * n_qt * TQ, H_C * DV_C), F32),
        compiler_params=_cparams(("arbitrary", "arbitrary", "arbitrary")),
        name="diff_attention",
    )(lam, z, z, z, cos, sup, sdn, cos, sup, sdn, gsub)


def _layer_norm(r, g, b):
    mu = jnp.mean(r, axis=-1, keepdims=True)
    d = r - mu
    var = jnp.mean(d * d, axis=-1, keepdims=True)
    return d * lax.rsqrt(var + LN_EPS) * g + b


def _proj_ln_kernel(y1_ref, y2_ref, w1_ref, w2_ref, x_ref, mod_ref, g_ref, b_ref, o_ref):
    y = jnp.dot(y1_ref[...].astype(BF16), w1_ref[...], preferred_element_type=F32)
    y = y + jnp.dot(y2_ref[...].astype(BF16), w2_ref[...], preferred_element_type=F32)
    r = DEEPNORM_ALPHA * x_ref[...] + mod_ref[0, 2:3, :] * y
    o_ref[...] = _layer_norm(r, g_ref[...], b_ref[...])


def _proj_ln(y1, y2, col2, w_out_bf16, x, mod_l, ln_g, ln_b, latent_only):
    half = D_MODEL // 2
    rt = _row_tile(latent_only)
    mr = _mod_row(latent_only)
    n_tiles = _n_tiles(latent_only)
    return pl.pallas_call(
        _proj_ln_kernel,
        grid=(n_tiles,),
        in_specs=[
            pl.BlockSpec((TM, half), lambda i: (i, 0)),
            pl.BlockSpec((TM, half), lambda i: (i, col2)),
            pl.BlockSpec((half, D_MODEL), lambda i: (0, 0)),
            pl.BlockSpec((half, D_MODEL), lambda i: (1, 0)),
            pl.BlockSpec((TM, D_MODEL), lambda i: (rt(i), 0)),
            pl.BlockSpec((1, 6, D_MODEL), lambda i: (mr(i), 0, 0)),
            pl.BlockSpec((1, D_MODEL), lambda i: (0, 0)),
            pl.BlockSpec((1, D_MODEL), lambda i: (0, 0)),
        ],
        out_specs=pl.BlockSpec((TM, D_MODEL), lambda i: (i, 0)),
        out_shape=jax.ShapeDtypeStruct((n_tiles * TM, D_MODEL), F32),
        compiler_params=_cparams(("arbitrary",)),
        name="proj_ln",
    )(y1, y2, w_out_bf16, w_out_bf16, x, mod_l, ln_g, ln_b)


def _router_kernel(x_ref, mod_ref, w_ref, b_ref, mi_ref, mf_ref, cnt_ref, carry):
    i = pl.program_id(0)

    @pl.when(i == 0)
    def _():
        carry[...] = jnp.zeros_like(carry)

    u = x_ref[...] * (1.0 + mod_ref[0, 4:5, :]) + mod_ref[0, 3:4, :]
    logits = _dot_split(u, w_ref[...]) + b_ref[...]
    lane = lax.broadcasted_iota(jnp.int32, (TM, LANES), 1)
    lane_f = lane.astype(F32)
    neg = -jnp.inf
    big = 1e9

    gmask = lane < N_GROUPS
    gl = jnp.where(gmask, logits, neg)
    gmax = jnp.max(gl, axis=-1, keepdims=True)
    gidx = jnp.min(jnp.where(gl == gmax, lane_f, big), axis=-1, keepdims=True)
    gw = 1.0 / jnp.sum(jnp.where(gmask, jnp.exp(logits - gmax), 0.0), axis=-1, keepdims=True)

    e_lane = lane - N_GROUPS
    in_grp = (e_lane >= 0) & (e_lane < N_EXPERTS) & ((e_lane >> 3) == gidx.astype(jnp.int32))
    el = jnp.where(in_grp, logits, neg)
    v1 = jnp.max(el, axis=-1, keepdims=True)
    i1 = jnp.min(jnp.where(el == v1, lane_f, big), axis=-1, keepdims=True)
    el2 = jnp.where(lane_f == i1, neg, el)
    v2 = jnp.max(el2, axis=-1, keepdims=True)
    i2 = jnp.min(jnp.where(el2 == v2, lane_f, big), axis=-1, keepdims=True)
    t = jnp.exp(v2 - v1)
    c0 = gw / (1.0 + t)
    c1 = gw * t / (1.0 + t)
    e0 = i1 - N_GROUPS
    e1 = i2 - N_GROUPS

    oh0 = lane_f == e0
    oh1 = lane_f == e1
    cnt = oh0.astype(F32) + oh1.astype(F32)
    ri = lax.broadcasted_iota(jnp.int32, (TM, TM), 0)
    cj = lax.broadcasted_iota(jnp.int32, (TM, TM), 1)
    strict = (cj < ri).astype(BF16)
    before = jnp.dot(strict, cnt.astype(BF16), preferred_element_type=F32) + carry[0:1, :]
    r0 = jnp.sum(jnp.where(oh0, before, 0.0), axis=-1, keepdims=True)
    r1 = jnp.sum(jnp.where(oh1, before, 0.0), axis=-1, keepdims=True)
    carry[0:1, :] = carry[0:1, :] + jnp.sum(cnt, axis=0, keepdims=True)

    mi = jnp.where(lane == 0, e0, jnp.where(lane == 1, e1, jnp.where(lane == 2, r0, jnp.where(lane == 3, r1, 0.0))))
    mi_ref[...] = mi.astype(jnp.int32)
    mf_ref[...] = jnp.where(lane == 0, c0, jnp.where(lane == 1, c1, 0.0))
    cnt_ref[...] = carry[...]


def _router(x, mod_l, w_route, b_route, latent_only):
    mr = _mod_row(latent_only)
    n_tiles = _n_tiles(latent_only)
    n_tok = n_tiles * TM
    row_blk = lambda w: pl.BlockSpec((TM, w), lambda i: (i, 0))
    return pl.pallas_call(
        _router_kernel,
        grid=(n_tiles,),
        in_specs=[
            row_blk(D_MODEL),
            pl.BlockSpec((1, 6, D_MODEL), lambda i: (mr(i), 0, 0)),
            pl.BlockSpec((D_MODEL, LANES), lambda i: (0, 0)),
            pl.BlockSpec((1, LANES), lambda i: (0, 0)),
        ],
        out_specs=[row_blk(LANES), row_blk(LANES), pl.BlockSpec((8, LANES), lambda i: (0, 0))],
        out_shape=[jax.ShapeDtypeStruct((n_tok, LANES), jnp.int32),
                   jax.ShapeDtypeStruct((n_tok, LANES), F32),
                   jax.ShapeDtypeStruct((8, LANES), F32)],
        scratch_shapes=[pltpu.VMEM((8, LANES), F32)],
        compiler_params=_cparams(("arbitrary",)),
        name="moe_router",
    )(x, mod_l, w_route, b_route)


def _each_row(fn):
    def body(g, carry):
        for j in range(ROW_UNROLL):
            for k in range(2):
                fn(g, j, k)
        return carry
    lax.fori_loop(0, TM // ROW_UNROLL, body, 0)


def _dispatch_kernel(pad_end_ref, padded_ref, x_ref, mod_ref, dest_hbm, xs_hbm,
                     idx_smem, u_scr, zero_scr, sem_idx, sem_row, sem_zero):
    i = pl.program_id(0)
    slot = i % 2
    per_tile = 2 * TM

    @pl.when(i == 0)
    def _():
        zero_scr[...] = jnp.zeros_like(zero_scr)

        def zero_block(first_row):
            rows = pl.ds(pl.multiple_of(first_row, MOE_ROWS), MOE_ROWS)
            return pltpu.make_async_copy(zero_scr, xs_hbm.at[rows], sem_zero)

        n_rows = xs_hbm.shape[0]
        total = pad_end_ref[N_EXPERTS - 1]
        for e in range(N_EXPERTS):
            pl.when(padded_ref[e] > 0)(lambda e=e: zero_block(pad_end_ref[e] - MOE_ROWS).start())
            pl.when(total + e * MOE_ROWS < n_rows)(lambda e=e: zero_block(total + e * MOE_ROWS).start())
        for e in range(N_EXPERTS):
            pl.when(padded_ref[e] > 0)(lambda e=e: zero_block(0).wait())
            pl.when(total + e * MOE_ROWS < n_rows)(lambda e=e: zero_block(0).wait())

    idx_copy = pltpu.make_async_copy(dest_hbm.at[pl.ds(i * per_tile, per_tile)],
                                     idx_smem.at[pl.ds(slot * per_tile, per_tile)], sem_idx)
    idx_copy.start()
    u = x_ref[...] * (1.0 + mod_ref[0, 4:5, :]) + mod_ref[0, 3:4, :]
    u_scr[slot] = u.reshape(TM // ROW_UNROLL, ROW_UNROLL, D_MODEL)
    idx_copy.wait()

    def row_copy(sl, g, j, dst_row):
        return pltpu.make_async_copy(u_scr.at[sl, g, pl.ds(j, 1)], xs_hbm.at[pl.ds(dst_row, 1)], sem_row.at[sl])

    def dest_of(sl, g, j, k):
        return idx_smem[sl * per_tile + g * (2 * ROW_UNROLL) + (2 * j + k)]

    _each_row(lambda g, j, k: row_copy(slot, g, j, dest_of(slot, g, j, k)).start())

    @pl.when(i > 0)
    def _():
        _each_row(lambda g, j, k: row_copy(1 - slot, g, j, 0).wait())

    @pl.when(i == pl.num_programs(0) - 1)
    def _():
        _each_row(lambda g, j, k: row_copy(slot, g, j, 0).wait())


def _dispatch(pad_end, padded, x1, mod_l, dest, n_blocks, latent_only):
    mr = _mod_row(latent_only)
    return pl.pallas_call(
        _dispatch_kernel,
        grid_spec=pltpu.PrefetchScalarGridSpec(
            num_scalar_prefetch=2,
            grid=(_n_tiles(latent_only),),
            in_specs=[
                pl.BlockSpec((TM, D_MODEL), lambda i, pe, pd: (i, 0)),
                pl.BlockSpec((1, 6, D_MODEL), lambda i, pe, pd: (mr(i), 0, 0)),
                pl.BlockSpec(memory_space=pl.ANY),
            ],
            out_specs=pl.BlockSpec(memory_space=pl.ANY),
            scratch_shapes=[
                pltpu.SMEM((2 * 2 * TM,), jnp.int32),
                pltpu.VMEM((2, TM // ROW_UNROLL, ROW_UNROLL, D_MODEL), F32),
                pltpu.VMEM((MOE_ROWS, D_MODEL), F32),
                pltpu.SemaphoreType.DMA(()),
                pltpu.SemaphoreType.DMA((2,)),
                pltpu.SemaphoreType.DMA(()),
            ],
        ),
        out_shape=jax.ShapeDtypeStruct((n_blocks * MOE_ROWS, D_MODEL), F32),
        compiler_params=_cparams(("arbitrary",)),
        name="moe_dispatch",
    )(pad_end, padded, x1, mod_l, dest)


def _expert_kernel(blk_exp_ref, n_used_ref, x_ref, wg_ref, wu_ref, wd_ref, o_ref):
    used = pl.program_id(0) < n_used_ref[0]

    @pl.when(used)
    def _():
        x = x_ref[...].astype(BF16)
        gate = jnp.dot(x, wg_ref[0, 0].astype(BF16), preferred_element_type=F32)
        up = jnp.dot(x, wu_ref[0, 0].astype(BF16), preferred_element_type=F32)
        hid = (_silu(gate) * up).astype(BF16)
        o_ref[...] = jnp.dot(hid, wd_ref[0, 0].astype(BF16), preferred_element_type=F32)

    @pl.when(jnp.logical_not(used))
    def _():
        o_ref[...] = jnp.zeros_like(o_ref)


def _experts(blk_exp, n_used, xs, layer, w_gate, w_up, w_down):
    n_blocks = xs.shape[0] // MOE_ROWS
    row_in = pl.BlockSpec((MOE_ROWS, D_MODEL), lambda i, be, nu: (jnp.minimum(i, nu[0] - 1), 0))
    w_in = pl.BlockSpec((1, 1, D_MODEL, D_EXPERT), lambda i, be, nu: (layer, be[i], 0, 0))
    w_out = pl.BlockSpec((1, 1, D_EXPERT, D_MODEL), lambda i, be, nu: (layer, be[i], 0, 0))
    return pl.pallas_call(
        _expert_kernel,
        grid_spec=pltpu.PrefetchScalarGridSpec(
            num_scalar_prefetch=2,
            grid=(n_blocks,),
            in_specs=[row_in, w_in, w_in, w_out],
            out_specs=pl.BlockSpec((MOE_ROWS, D_MODEL), lambda i, be, nu: (i, 0)),
        ),
        out_shape=jax.ShapeDtypeStruct(xs.shape, F32),
        compiler_params=_cparams(("arbitrary",)),
        name="moe_experts",
    )(blk_exp, n_used, xs, w_gate, w_up, w_down)


def _combine_ln_kernel(ys_hbm, dest_hbm, mf_ref, x_ref, mod_ref, g_ref, b_ref, o_ref,
                       idx_smem, y_buf, sem_idx, sem_row):
    i = pl.program_id(0)
    slot = i % 2
    per_tile = 2 * TM

    def row_copy(sl, g, j, k, src_row):
        return pltpu.make_async_copy(ys_hbm.at[pl.ds(src_row, 1)], y_buf.at[sl, k, g, pl.ds(j, 1)], sem_row.at[sl])

    def request(tile, sl):
        idx_copy = pltpu.make_async_copy(dest_hbm.at[pl.ds(tile * per_tile, per_tile)],
                                         idx_smem.at[pl.ds(sl * per_tile, per_tile)], sem_idx)
        idx_copy.start()
        idx_copy.wait()
        _each_row(lambda g, j, k: row_copy(
            sl, g, j, k, idx_smem[sl * per_tile + g * (2 * ROW_UNROLL) + (2 * j + k)]).start())

    pl.when(i == 0)(lambda: request(0, 0))
    pl.when(i + 1 < pl.num_programs(0))(lambda: request(i + 1, 1 - slot))
    _each_row(lambda g, j, k: row_copy(slot, g, j, k, 0).wait())

    mf = mf_ref[...]
    y0 = y_buf[slot, 0].reshape(TM, D_MODEL)
    y1 = y_buf[slot, 1].reshape(TM, D_MODEL)
    y = mf[:, 0:1] * y0 + mf[:, 1:2] * y1
    r = DEEPNORM_ALPHA * x_ref[...] + mod_ref[0, 5:6, :] * y
    o_ref[...] = _layer_norm(r, g_ref[...], b_ref[...])


def _combine_ln(ys, dest, mf, x1, mod_l, ln_g, ln_b, latent_only):
    mr = _mod_row(latent_only)
    n_tiles = _n_tiles(latent_only)
    return pl.pallas_call(
        _combine_ln_kernel,
        grid=(n_tiles,),
        in_specs=[
            pl.BlockSpec(memory_space=pl.ANY),
            pl.BlockSpec(memory_space=pl.ANY),
            pl.BlockSpec((TM, LANES), lambda i: (i, 0)),
            pl.BlockSpec((TM, D_MODEL), lambda i: (i, 0)),
            pl.BlockSpec((1, 6, D_MODEL), lambda i: (mr(i), 0, 0)),
            pl.BlockSpec((1, D_MODEL), lambda i: (0, 0)),
            pl.BlockSpec((1, D_MODEL), lambda i: (0, 0)),
        ],
        out_specs=pl.BlockSpec((TM, D_MODEL), lambda i: (i, 0)),
        out_shape=jax.ShapeDtypeStruct((n_tiles * TM, D_MODEL), F32),
        scratch_shapes=[
            pltpu.SMEM((2 * 2 * TM,), jnp.int32),
            pltpu.VMEM((2, 2, TM // ROW_UNROLL, ROW_UNROLL, D_MODEL), F32),
            pltpu.SemaphoreType.DMA(()),
            pltpu.SemaphoreType.DMA((2,)),
        ],
        compiler_params=_cparams(("arbitrary",)),
        name="combine_ln",
    )(ys, dest, mf, x1, mod_l, ln_g, ln_b)


def _moe(x1, mod_l, layer, w_grp, b_grp, w_rexp, b_rexp, w_gate, w_up, w_down, ln_g, ln_b, latent_only):
    pad = LANES - N_GROUPS - N_EXPERTS
    w_route = jnp.concatenate([w_grp, w_rexp, jnp.zeros((D_MODEL, pad), F32)], axis=1)
    b_route = jnp.concatenate([b_grp, b_rexp, jnp.zeros((pad,), F32)])[None, :]
    mi, mf, cnt = _router(x1, mod_l, w_route, b_route, latent_only)

    counts = cnt[0, :N_EXPERTS].astype(jnp.int32)
    padded = (counts + MOE_ROWS - 1) // MOE_ROWS * MOE_ROWS
    pad_end = jnp.cumsum(padded)
    pad_start = pad_end - padded
    n_tok = x1.shape[0]
    n_blocks = (2 * n_tok) // MOE_ROWS + N_EXPERTS
    experts = jnp.arange(N_EXPERTS, dtype=jnp.int32)
    start_of = jnp.sum(jnp.where(mi[:, 0:2, None] == experts, pad_start, 0), axis=-1)
    dest = (start_of + mi[:, 2:4]).reshape(-1)
    blk_start = jnp.arange(n_blocks, dtype=jnp.int32) * MOE_ROWS
    blk_exp = jnp.minimum(jnp.sum((pad_end[None, :] <= blk_start[:, None]).astype(jnp.int32), axis=1),
                          N_EXPERTS - 1)
    n_used = pad_end[-1:] // MOE_ROWS
    xs = _dispatch(pad_end, padded, x1, mod_l, dest, n_blocks, latent_only)
    ys = _experts(blk_exp, n_used, xs, layer, w_gate, w_up, w_down)
    return _combine_ln(ys, dest, mf, x1, mod_l, ln_g, ln_b, latent_only)


def kernel(x, c, ctx, c_ctx, ada_w, ada_b, ln1_g, ln1_b, ln2_g, ln2_b, ab_w_in, ab_w_lr_f, ab_b_lr_f, ab_w_lr_b, ab_b_lr_b, ab_gn_a, ab_gn_b, ab_w_out, c_w_qkv, c_lq1, c_lk1, c_lq2, c_lk2, c_subln_g, c_w_out, moe_w_grp, moe_b_grp, moe_w_rexp, moe_b_rexp, moe_w_gate, moe_w_up, moe_w_down):
    assert x.shape == (BATCH, SEQ, D_MODEL) and ctx.shape == (BATCH, CTX_LEN, D_MODEL)
    xs = jnp.concatenate([ctx, x], axis=1).reshape(N_ALL, D_MODEL)
    c_all = jnp.concatenate([c, c_ctx[None, :], jnp.zeros((16 - BATCH - 1, D_MODEL), F32)], axis=0)
    mod = _ada_tables(c_all, ada_w, ada_b).reshape(DEPTH, 16, 6, D_MODEL)

    rope_a = _rope_tables(DK_A, 1)
    rope_c = _rope_tables(DH_C, 2)
    dec_a = _retention_tables()

    for l in range(DEPTH):
        last = l == DEPTH - 1
        i = l // 2
        mod_l = mod[l]
        row = lambda v: v[None, :]
        if l % 2 == 0:
            assert not last
            w_in = jnp.pad(ab_w_in[i], ((0, 0), (0, AB_COLS - ab_w_in.shape[2]))).astype(BF16)
            z = _mod_matmul(xs, mod_l, w_in)
            ya = _scan_a(z, *rope_a, dec_a, row(ab_gn_a[i]))
            wf = ab_w_lr_f[i].reshape(GLA_RANK, H_B // 2, LANES)
            wb = ab_w_lr_b[i].reshape(GLA_RANK, H_B // 2, LANES)
            wlr = jnp.zeros((H_B // 2, LANES, 2 * LANES), F32)
            wlr = wlr.at[:, 0:GLA_RANK, 0:LANES].set(jnp.swapaxes(wf, 0, 1))
            wlr = wlr.at[:, GLA_RANK:2 * GLA_RANK, LANES:].set(jnp.swapaxes(wb, 0, 1))
            blr = jnp.concatenate([ab_b_lr_f[i].reshape(H_B // 2, 1, LANES),
                                   ab_b_lr_b[i].reshape(H_B // 2, 1, LANES)], axis=-1)
            yb = _scan_b(z, wlr, blr, row(ab_gn_b[i]))
            x1 = _proj_ln(ya, yb, 0, ab_w_out[i].astype(BF16), xs, mod_l, row(ln1_g[l]), row(ln1_b[l]), False)
        else:
            lam_init = 0.8 - 0.6 * math.exp(-0.3 * l)
            lam = (jnp.exp(jnp.sum(c_lq1[i] * c_lk1[i], axis=-1))
                   - jnp.exp(jnp.sum(c_lq2[i] * c_lk2[i], axis=-1))).astype(F32) + lam_init
            z = _mod_matmul(xs, mod_l, c_w_qkv[i].astype(BF16))
            gsub = row(c_subln_g[i])
            y = _attention(z, lam, *rope_c, gsub, 1.0 - lam_init, last)
            x1 = _proj_ln(y, y, 1, c_w_out[i].astype(BF16), xs, mod_l, row(ln1_g[l]), row(ln1_b[l]), last)
        xs = _moe(x1, mod_l, l, moe_w_grp[l], moe_b_grp[l], moe_w_rexp[l], moe_b_rexp[l],
                  moe_w_gate, moe_w_up, moe_w_down, row(ln2_g[l]), row(ln2_b[l]), last)
    return xs.reshape(BATCH, SEQ, D_MODEL)
```

```python
import functools
import math

import numpy as np
import jax
import jax.numpy as jnp
from jax import lax
from jax.experimental import pallas as pl
from jax.experimental.pallas import tpu as pltpu

F32 = jnp.float32
BF16 = jnp.bfloat16

D_MODEL = 1024
BATCH = 8
SEQ = 2048
DEPTH = 4
GRID_W = 64
CTX_LEN = 256
ROPE_BASE = 10000.0
LN_EPS = 1e-5
DEEPNORM_ALPHA = (2 * DEPTH) ** 0.25
H_A = 4
DK_A = 128
DV_A = 128
CHUNK_A = 128
RET_EXP_FWD = 5.0
RET_EXP_BWD = 5.5
H_B = 4
DK_B = 64
DV_B = 128
GLA_RANK = 16
GLA_TAU = 16.0
CHUNK_B = 64
H_C = 8
DH_C = 64
DV_C = 128
N_GROUPS = 4
EXPERTS_PER_GROUP = 8
N_EXPERTS = 32
D_EXPERT = 512

LANES = 128
T_ALL = CTX_LEN + SEQ
N_ALL = BATCH * T_ALL
TM = 256
TILES_PER_BATCH = T_ALL // TM
LATENT_TILES_PER_BATCH = SEQ // TM
AB_COLS = 29 * LANES
MOE_ROWS = TM
PACKED = D_MODEL // 2
ROW_UNROLL = 8
SCAN_UNROLL = 4
GLA_GROUP = 256
TQ = 256
VMEM_LIMIT = 56 * 1024 * 1024


def _cparams(sem):
    return pltpu.CompilerParams(dimension_semantics=sem, vmem_limit_bytes=VMEM_LIMIT)


def _silu(v):
    return v * (1.0 / (1.0 + jnp.exp(-v)))


def _n_tiles(latent_only):
    return BATCH * (LATENT_TILES_PER_BATCH if latent_only else TILES_PER_BATCH)


def _row_tile(latent_only):
    if latent_only:
        return lambda i: (i // LATENT_TILES_PER_BATCH) * TILES_PER_BATCH + 1 + i % LATENT_TILES_PER_BATCH
    return lambda i: i


def _mod_row(latent_only):
    if latent_only:
        return lambda i: i // LATENT_TILES_PER_BATCH
    return lambda i: jnp.where(i % TILES_PER_BATCH == 0, BATCH, i // TILES_PER_BATCH)


def _ada_kernel(c_ref, w_ref, b_ref, o_ref):
    sc = _silu(c_ref[...])
    o_ref[0] = jnp.dot(sc.astype(BF16), w_ref[0].astype(BF16), preferred_element_type=F32) + b_ref[0]


def _ada_tables(c_all, ada_w, ada_b):
    tn = 1536
    n_out = 6 * D_MODEL
    return pl.pallas_call(
        _ada_kernel,
        grid=(DEPTH, n_out // tn),
        in_specs=[
            pl.BlockSpec((16, D_MODEL), lambda l, j: (0, 0)),
            pl.BlockSpec((1, D_MODEL, tn), lambda l, j: (l, 0, j)),
            pl.BlockSpec((1, 1, tn), lambda l, j: (l, 0, j)),
        ],
        out_specs=pl.BlockSpec((1, 16, tn), lambda l, j: (l, 0, j)),
        out_shape=jax.ShapeDtypeStruct((DEPTH, 16, n_out), F32),
        compiler_params=_cparams(("arbitrary", "arbitrary")),
        name="ada_tables",
    )(c_all, ada_w, ada_b.reshape(DEPTH, 1, n_out))


def _modmm_kernel(x_ref, mod_ref, w_ref, o_ref):
    u = x_ref[...] * (1.0 + mod_ref[0, 1:2, :]) + mod_ref[0, 0:1, :]
    o_ref[...] = jnp.dot(u.astype(BF16), w_ref[...], preferred_element_type=F32)


def _mod_matmul(x, mod_l, w_bf16):
    n_out = w_bf16.shape[1]
    return pl.pallas_call(
        _modmm_kernel,
        grid=(N_ALL // TM,),
        in_specs=[
            pl.BlockSpec((TM, D_MODEL), lambda i: (i, 0)),
            pl.BlockSpec((1, 6, D_MODEL), lambda i: (_mod_row(False)(i), 0, 0)),
            pl.BlockSpec((D_MODEL, n_out), lambda i: (0, 0)),
        ],
        out_specs=pl.BlockSpec((TM, n_out), lambda i: (i, 0)),
        out_shape=jax.ShapeDtypeStruct((N_ALL, n_out), F32),
        compiler_params=_cparams(("arbitrary",)),
        name="mod_matmul",
    )(x, mod_l, w_bf16)


def _rope_tables(head_dim, reps):
    rows = SEQ // GRID_W
    row = np.repeat(np.arange(rows, dtype=np.float32), GRID_W)
    col = np.tile(np.arange(GRID_W, dtype=np.float32), rows)
    quarter = head_dim // 4
    inv = (ROPE_BASE ** (-np.arange(quarter, dtype=np.float32) / quarter)).astype(np.float32)
    ang_r = row[:, None] * inv
    ang_c = col[:, None] * inv
    ang = np.concatenate([ang_r, ang_r, ang_c, ang_c], axis=-1)
    cos = np.cos(ang).astype(np.float32)
    sin = np.sin(ang).astype(np.float32)
    q_idx = (np.arange(head_dim) // quarter) % 2
    sin_up = np.where(q_idx == 1, sin, 0.0).astype(np.float32)
    sin_dn = np.where(q_idx == 0, -sin, 0.0).astype(np.float32)

    def full(t, ctx_val):
        t = np.tile(t, (1, reps))
        return jnp.asarray(np.concatenate([np.full((CTX_LEN, t.shape[1]), ctx_val, np.float32), t], axis=0))

    return full(cos, 1.0), full(sin_up, 0.0), full(sin_dn, 0.0)


def _rope(x, cos, sin_up, sin_dn, quarter):
    width = x.shape[-1]
    return x * cos + pltpu.roll(x, quarter, 1) * sin_up + pltpu.roll(x, width - quarter, 1) * sin_dn


def _dot_tb(a, b):
    return lax.dot_general(a, b, (((1,), (1,)), ((), ())), preferred_element_type=F32)


def _dot_ta(a, b):
    return lax.dot_general(a, b, (((0,), (0,)), ((), ())), preferred_element_type=F32)


def _split_bf16(x, parts):
    out = []
    for _ in range(parts):
        t = x.astype(BF16)
        out.append(t)
        x = x - t.astype(F32)
    return out


def _dot_split(a, b):
    a_hi, a_lo = _split_bf16(a, 2)
    b_hi, b_lo = _split_bf16(b, 2)
    dot = lambda u, v: jnp.dot(u, v, preferred_element_type=F32)
    return dot(a_hi, b_hi) + (dot(a_hi, b_lo) + dot(a_lo, b_hi))


def _dot_mask(mask_bf16, x):
    return sum(jnp.dot(mask_bf16, t, preferred_element_type=F32) for t in reversed(_split_bf16(x, 3)))


def _retention_tables():
    c = CHUNK_A
    i = np.arange(c, dtype=np.float64)
    out = np.zeros((H_A, 7, c, LANES), np.float64)
    for h in range(H_A):
        lgf = np.log1p(-np.exp2(-(RET_EXP_FWD + h)))
        lgb = np.log1p(-np.exp2(-(RET_EXP_BWD + h)))
        d = i[:, None] - i[None, :]
        out[h, 0] = np.where(d >= 0, np.exp(lgf * d), np.exp(lgb * (-d - 1)))
        out[h, 1] = np.exp(lgf * (i + 1))[:, None]
        out[h, 2] = np.exp(lgb * (c - 1 - i))[:, None]
        out[h, 3] = np.exp(lgf * (c - 1 - i))[:, None]
        out[h, 4] = np.exp(lgb * i)[:, None]
        out[h, 5] = np.exp(lgf * c)
        out[h, 6] = np.exp(lgb * c)
    return jnp.asarray(out.astype(np.float32))


def _scan_a_kernel(q_ref, k_ref, v_ref, g_ref, cos_ref, sup_ref, sdn_ref, dec_ref, gn_ref, o_ref,
                   kr_scr, sb_scr):
    c = CHUNK_A
    n_ctx = CTX_LEN // c
    n_all = T_ALL // c
    scale = DK_A ** -0.5
    kr_scr[...] = _rope(k_ref[...], cos_ref[...], sup_ref[...], sdn_ref[...], DK_A // 4)
    dmat = dec_ref[0, 0]
    q_f, q_b, k_f, k_b = dec_ref[0, 1], dec_ref[0, 2], dec_ref[0, 3], dec_ref[0, 4]
    g_fc, g_bc = dec_ref[0, 5], dec_ref[0, 6]
    gn = gn_ref[...]
    zero = jnp.zeros((DK_A, DV_A), F32)

    def chunk(ci):
        return pl.ds(pl.multiple_of(ci * c, c), c)

    def kv_state(ci, k_dec):
        sl = chunk(ci)
        return _dot_ta((kr_scr[sl, :] * k_dec).astype(BF16), v_ref[sl, :].astype(BF16))

    def run(lo, hi, sf0, sb0):
        def bwd(j, sb):
            ci = hi - 1 - j
            sb_scr[ci] = sb
            return g_bc * sb + kv_state(ci, k_b)

        sb_fin = lax.fori_loop(0, hi - lo, bwd, sb0, unroll=SCAN_UNROLL)

        def fwd(j, sf):
            ci = lo + j
            sl = chunk(ci)
            q = _rope(q_ref[sl, :], cos_ref[sl, :], sup_ref[sl, :], sdn_ref[sl, :], DK_A // 4) * scale
            k = kr_scr[sl, :]
            vb = v_ref[sl, :].astype(BF16)
            att = _dot_tb(q.astype(BF16), k.astype(BF16)) * dmat
            o = jnp.dot(att.astype(BF16), vb, preferred_element_type=F32)
            o = o + jnp.dot((q * q_f).astype(BF16), sf.astype(BF16), preferred_element_type=F32)
            o = o + jnp.dot((q * q_b).astype(BF16), sb_scr[ci].astype(BF16), preferred_element_type=F32)
            o = o - jnp.mean(o, axis=-1, keepdims=True)
            o = o * lax.rsqrt(jnp.mean(o * o, axis=-1, keepdims=True) + LN_EPS)
            o_ref[sl, :] = _silu(g_ref[sl, :]) * (o * gn)
            return g_fc * sf + _dot_ta((k * k_f).astype(BF16), vb)

        sf_fin = lax.fori_loop(0, hi - lo, fwd, sf0, unroll=SCAN_UNROLL)
        return sf_fin, sb_fin

    sf_c, sb_c = run(0, n_ctx, zero, zero)
    run(n_ctx, n_all, sf_c, sb_c)


def _scan_a(z, cos, sup, sdn, dec, gn_a):
    blk = lambda col0: pl.BlockSpec((T_ALL, LANES), lambda b, h: (b, col0 + h))
    tbl = pl.BlockSpec((T_ALL, LANES), lambda b, h: (0, 0))
    return pl.pallas_call(
        _scan_a_kernel,
        grid=(BATCH, H_A),
        in_specs=[blk(0), blk(4), blk(8), blk(12), tbl, tbl, tbl,
                  pl.BlockSpec((1, 7, CHUNK_A, LANES), lambda b, h: (h, 0, 0, 0)),
                  pl.BlockSpec((1, LANES), lambda b, h: (0, h))],
        out_specs=pl.BlockSpec((T_ALL, LANES), lambda b, h: (b, h)),
        out_shape=jax.ShapeDtypeStruct((N_ALL, H_A * DV_A), F32),
        scratch_shapes=[pltpu.VMEM((T_ALL, LANES), F32),
                        pltpu.VMEM((T_ALL // CHUNK_A, DK_A, DV_A), F32)],
        compiler_params=_cparams(("arbitrary", "arbitrary")),
        name="scan_retention",
    )(z, z, z, z, cos, sup, sdn, dec, gn_a)


def _log_sigmoid(g):
    return jnp.minimum(g, 0.0) - jnp.log1p(jnp.exp(-jnp.abs(g)))


def _scan_b_kernel(q_ref, k_ref, v_ref, g_ref, lr_ref, wlr_ref, blr_ref, gn_ref, o_ref,
                   qf_scr, kf_scr, qb_scr, kb_scr, ktf_scr, ktb_scr, ef_scr, eb_scr, sb_scr):
    c = CHUNK_B
    n_ctx = CTX_LEN // c
    n_all = T_ALL // c
    per_group = GLA_GROUP // c
    scale = DK_B ** -0.5

    gi_r = lax.broadcasted_iota(jnp.int32, (GLA_GROUP, GLA_GROUP), 0)
    gi_c = lax.broadcasted_iota(jnp.int32, (GLA_GROUP, GLA_GROUP), 1)
    same_chunk = (gi_r // c) == (gi_c // c)
    prefix = (same_chunk & (gi_c <= gi_r)).astype(BF16)
    suffix = (same_chunk & (gi_c >= gi_r)).astype(BF16)

    def prepare(gi, carry):
        sl = pl.ds(pl.multiple_of(gi * GLA_GROUP, GLA_GROUP), GLA_GROUP)
        gates = _dot_split(lr_ref[sl, :], wlr_ref[0]) + blr_ref[0]
        laf = _log_sigmoid(gates[:, :LANES]) * (1.0 / GLA_TAU)
        lab = _log_sigmoid(gates[:, LANES:]) * (1.0 / GLA_TAU)
        b = _dot_mask(prefix, laf)
        rb = _dot_mask(suffix, lab)
        q = q_ref[sl, :] * scale
        k = k_ref[sl, :]
        qf_scr[sl, :] = (q * jnp.exp(b)).astype(BF16)
        kf_scr[sl, :] = (k * jnp.exp(-b)).astype(BF16)
        qb_scr[sl, :] = (q * jnp.exp(rb - lab)).astype(BF16)
        kb_scr[sl, :] = (k * jnp.exp(-rb)).astype(BF16)
        b3 = b.reshape(per_group, c, LANES)
        rb3 = rb.reshape(per_group, c, LANES)
        k3 = k.reshape(per_group, c, LANES)
        b_tot = b3[:, c - 1:c, :]
        rb_tot = rb3[:, 0:1, :]
        ktf_scr[sl, :] = (k3 * jnp.exp(b_tot - b3)).reshape(GLA_GROUP, LANES).astype(BF16)
        ktb_scr[sl, :] = (k3 * jnp.exp(rb_tot - rb3)).reshape(GLA_GROUP, LANES).astype(BF16)
        for m in range(per_group):
            ef_scr[gi * per_group + m] = jnp.broadcast_to(jnp.exp(b_tot[m]), (8, LANES))
            eb_scr[gi * per_group + m] = jnp.broadcast_to(jnp.exp(rb_tot[m]), (8, LANES))
        return carry

    lax.fori_loop(0, T_ALL // GLA_GROUP, prepare, 0)

    lane = lax.broadcasted_iota(jnp.int32, (1, LANES), 1)
    masks = [lane < DK_B, lane >= DK_B]
    ri = lax.broadcasted_iota(jnp.int32, (c, c), 0)
    cj = lax.broadcasted_iota(jnp.int32, (c, c), 1)
    lower = cj <= ri
    gn = gn_ref[...]
    zero = jnp.zeros((DV_B, LANES), F32)
    zero_b = jnp.zeros((), BF16)

    def chunk(ci):
        return pl.ds(pl.multiple_of(ci * c, c), c)

    def run(lo, hi, sf0, sb0):
        def bwd(j, sb):
            ci = hi - 1 - j
            sl = chunk(ci)
            e_tot = eb_scr[ci][0:1, :]
            kt = ktb_scr[sl, :]
            v = v_ref[sl, :]
            new = []
            for h in range(2):
                sb_scr[ci, h] = sb[h]
                vh = v[:, h * DV_B:(h + 1) * DV_B].astype(BF16)
                new.append(sb[h] * e_tot + _dot_ta(vh, jnp.where(masks[h], kt, zero_b)))
            return tuple(new)

        sb_fin = lax.fori_loop(0, hi - lo, bwd, sb0, unroll=SCAN_UNROLL)

        def fwd(j, sf):
            ci = lo + j
            sl = chunk(ci)
            e_tot = ef_scr[ci][0:1, :]
            qf, kf, qb, kb, kt = qf_scr[sl, :], kf_scr[sl, :], qb_scr[sl, :], kb_scr[sl, :], ktf_scr[sl, :]
            v = v_ref[sl, :]
            g = g_ref[sl, :]
            new = []
            for h in range(2):
                pick = lambda t: jnp.where(masks[h], t, zero_b)
                vh = v[:, h * DV_B:(h + 1) * DV_B].astype(BF16)
                qfh, qbh = pick(qf), pick(qb)
                att = jnp.where(lower, _dot_tb(qfh, pick(kf)), _dot_tb(qbh, pick(kb)))
                o = jnp.dot(att.astype(BF16), vh, preferred_element_type=F32)
                o = o + _dot_tb(qfh, sf[h].astype(BF16))
                o = o + _dot_tb(qbh, sb_scr[ci, h].astype(BF16))
                o = o * lax.rsqrt(jnp.mean(o * o, axis=-1, keepdims=True) + LN_EPS)
                cols = slice(h * DV_B, (h + 1) * DV_B)
                o_ref[sl, cols] = _silu(g[:, cols]) * (o * gn[:, cols])
                new.append(sf[h] * e_tot + _dot_ta(vh, pick(kt)))
            return tuple(new)

        sf_fin = lax.fori_loop(0, hi - lo, fwd, sf0, unroll=SCAN_UNROLL)
        return sf_fin, sb_fin

    sf_c, sb_c = run(0, n_ctx, (zero, zero), (zero, zero))
    run(n_ctx, n_all, sf_c, sb_c)


def _scan_b(z, wlr, blr, gn_b):
    pairs = H_B // 2
    return pl.pallas_call(
        _scan_b_kernel,
        grid=(BATCH, pairs),
        in_specs=[
            pl.BlockSpec((T_ALL, LANES), lambda b, p: (b, 16 + p)),
            pl.BlockSpec((T_ALL, LANES), lambda b, p: (b, 18 + p)),
            pl.BlockSpec((T_ALL, 2 * DV_B), lambda b, p: (b, 10 + p)),
            pl.BlockSpec((T_ALL, 2 * DV_B), lambda b, p: (b, 12 + p)),
            pl.BlockSpec((T_ALL, LANES), lambda b, p: (b, 28)),
            pl.BlockSpec((1, LANES, 2 * LANES), lambda b, p: (p, 0, 0)),
            pl.BlockSpec((1, 1, 2 * LANES), lambda b, p: (p, 0, 0)),
            pl.BlockSpec((1, 2 * DV_B), lambda b, p: (0, p)),
        ],
        out_specs=pl.BlockSpec((T_ALL, 2 * DV_B), lambda b, p: (b, p)),
        out_shape=jax.ShapeDtypeStruct((N_ALL, H_B * DV_B), F32),
        scratch_shapes=[pltpu.VMEM((T_ALL, LANES), BF16)] * 6
                       + [pltpu.VMEM((T_ALL // CHUNK_B, 8, LANES), F32)] * 2
                       + [pltpu.VMEM((T_ALL // CHUNK_B, 2, DV_B, LANES), F32)],
        compiler_params=_cparams(("arbitrary", "arbitrary")),
        name="scan_gla",
    )(z, z, z, z, z, wlr, blr, gn_b)


def _attn_kernel(lam_ref, q_ref, k_ref, v_ref, qcos_ref, qsup_ref, qsdn_ref, kcos_ref, ksup_ref, ksdn_ref,
                 gsub_ref, o_ref, k_scr, v_scr, *, post_scale, tile0):
    h = pl.program_id(1)
    t = pl.program_id(2)
    quarter = DH_C // 4
    scale = DH_C ** -0.5

    @pl.when(t == 0)
    def _():
        k_scr[...] = _rope(k_ref[...], kcos_ref[...], ksup_ref[...], ksdn_ref[...], quarter).astype(BF16)
        v_scr[...] = v_ref[...].astype(BF16)

    lam = lam_ref[h]
    lane = lax.broadcasted_iota(jnp.int32, (1, LANES), 1)
    m1 = (lane < DH_C).astype(F32)
    m2 = (lane >= DH_C).astype(F32)

    def attend(n_keys):
        q = _rope(q_ref[...], qcos_ref[...], qsup_ref[...], qsdn_ref[...], quarter) * scale
        kb = k_scr[0:n_keys, :]
        s1 = _dot_tb((q * m1).astype(BF16), kb)
        s2 = _dot_tb((q * m2).astype(BF16), kb)
        e1 = jnp.exp(s1 - jnp.max(s1, axis=-1, keepdims=True))
        e2 = jnp.exp(s2 - jnp.max(s2, axis=-1, keepdims=True))
        r1 = 1.0 / jnp.sum(e1, axis=-1, keepdims=True)
        r2 = lam / jnp.sum(e2, axis=-1, keepdims=True)
        a = e1 * r1 - e2 * r2
        o = jnp.dot(a.astype(BF16), v_scr[0:n_keys, :], preferred_element_type=F32)
        o = o * lax.rsqrt(jnp.mean(o * o, axis=-1, keepdims=True) + LN_EPS)
        o_ref[...] = o * (gsub_ref[...] * post_scale)

    if tile0 == 0:
        pl.when(t == 0)(lambda: attend(CTX_LEN))
        pl.when(t > 0)(lambda: attend(T_ALL))
    else:
        attend(T_ALL)


def _attention(z, lam, cos, sup, sdn, gsub, post_scale, latent_only):
    tile0 = 1 if latent_only else 0
    n_qt = TILES_PER_BATCH - tile0
    kern = functools.partial(_attn_kernel, post_scale=post_scale, tile0=tile0)
    kv = lambda col0: pl.BlockSpec((T_ALL, LANES), lambda b, h, t, lam_r: (b, col0 + h))
    q_tbl = pl.BlockSpec((TQ, LANES), lambda b, h, t, lam_r: (tile0 + t, 0))
    k_tbl = pl.BlockSpec((T_ALL, LANES), lambda b, h, t, lam_r: (0, 0))
    return pl.pallas_call(
        kern,
        grid_spec=pltpu.PrefetchScalarGridSpec(
            num_scalar_prefetch=1,
            grid=(BATCH, H_C, n_qt),
            in_specs=[pl.BlockSpec((TQ, LANES), lambda b, h, t, lam_r: (b * TILES_PER_BATCH + tile0 + t, h)),
                      kv(H_C), kv(2 * H_C), q_tbl, q_tbl, q_tbl, k_tbl, k_tbl, k_tbl,
                      pl.BlockSpec((1, LANES), lambda b, h, t, lam_r: (0, h))],
            out_specs=pl.BlockSpec((TQ, LANES), lambda b, h, t, lam_r: (b * n_qt + t, h)),
            scratch_shapes=[pltpu.VMEM((T_ALL, LANES), BF16), pltpu.VMEM((T_ALL, LANES), BF16)],
        ),
        out_shape=jax.ShapeDtypeStruct((BATCH * n_qt * TQ, H_C * DV_C), F32),
        compiler_params=_cparams(("arbitrary", "arbitrary", "arbitrary")),
        name="diff_attention",
    )(lam, z, z, z, cos, sup, sdn, cos, sup, sdn, gsub)


def _layer_norm(r, g, b):
    mu = jnp.mean(r, axis=-1, keepdims=True)
    d = r - mu
    var = jnp.mean(d * d, axis=-1, keepdims=True)
    return d * lax.rsqrt(var + LN_EPS) * g + b


def _route_tile(u, w_ref, b_ref, mi_ref, mf_ref, cnt_ref, carry):
    @pl.when(pl.program_id(0) == 0)
    def _():
        carry[...] = jnp.zeros_like(carry)

    logits = _dot_split(u, w_ref[...]) + b_ref[...]
    lane = lax.broadcasted_iota(jnp.int32, (TM, LANES), 1)
    lane_f = lane.astype(F32)
    neg = -jnp.inf
    big = 1e9

    gmask = lane < N_GROUPS
    gl = jnp.where(gmask, logits, neg)
    gmax = jnp.max(gl, axis=-1, keepdims=True)
    gidx = jnp.min(jnp.where(gl == gmax, lane_f, big), axis=-1, keepdims=True)
    gw = 1.0 / jnp.sum(jnp.where(gmask, jnp.exp(logits - gmax), 0.0), axis=-1, keepdims=True)

    e_lane = lane - N_GROUPS
    in_grp = (e_lane >= 0) & (e_lane < N_EXPERTS) & ((e_lane >> 3) == gidx.astype(jnp.int32))
    el = jnp.where(in_grp, logits, neg)
    v1 = jnp.max(el, axis=-1, keepdims=True)
    i1 = jnp.min(jnp.where(el == v1, lane_f, big), axis=-1, keepdims=True)
    el2 = jnp.where(lane_f == i1, neg, el)
    v2 = jnp.max(el2, axis=-1, keepdims=True)
    i2 = jnp.min(jnp.where(el2 == v2, lane_f, big), axis=-1, keepdims=True)
    t = jnp.exp(v2 - v1)
    c0 = gw / (1.0 + t)
    c1 = gw * t / (1.0 + t)
    e0 = i1 - N_GROUPS
    e1 = i2 - N_GROUPS

    oh0 = lane_f == e0
    oh1 = lane_f == e1
    cnt = oh0.astype(F32) + oh1.astype(F32)
    ri = lax.broadcasted_iota(jnp.int32, (TM, TM), 0)
    cj = lax.broadcasted_iota(jnp.int32, (TM, TM), 1)
    strict = (cj < ri).astype(BF16)
    before = jnp.dot(strict, cnt.astype(BF16), preferred_element_type=F32) + carry[0:1, :]
    r0 = jnp.sum(jnp.where(oh0, before, 0.0), axis=-1, keepdims=True)
    r1 = jnp.sum(jnp.where(oh1, before, 0.0), axis=-1, keepdims=True)
    carry[0:1, :] = carry[0:1, :] + jnp.sum(cnt, axis=0, keepdims=True)

    mi = jnp.where(lane == 0, e0, jnp.where(lane == 1, e1, jnp.where(lane == 2, r0, jnp.where(lane == 3, r1, 0.0))))
    mi_ref[...] = mi.astype(jnp.int32)
    mf_ref[...] = jnp.where(lane == 0, c0, jnp.where(lane == 1, c1, 0.0))
    cnt_ref[...] = carry[...]


def _proj_ln_kernel(y1_ref, y2_ref, w1_ref, w2_ref, x_ref, mod_ref, g_ref, b_ref, wr_ref, br_ref,
                    o_ref, mi_ref, mf_ref, cnt_ref, carry):
    y = jnp.dot(y1_ref[...].astype(BF16), w1_ref[...], preferred_element_type=F32)
    y = y + jnp.dot(y2_ref[...].astype(BF16), w2_ref[...], preferred_element_type=F32)
    r = DEEPNORM_ALPHA * x_ref[...] + mod_ref[0, 2:3, :] * y
    x1 = _layer_norm(r, g_ref[...], b_ref[...])
    o_ref[...] = x1
    u = x1 * (1.0 + mod_ref[0, 4:5, :]) + mod_ref[0, 3:4, :]
    _route_tile(u, wr_ref, br_ref, mi_ref, mf_ref, cnt_ref, carry)


def _proj_ln(y1, y2, col2, w_out_bf16, x, mod_l, ln_g, ln_b, w_route, b_route, latent_only):
    half = D_MODEL // 2
    rt = _row_tile(latent_only)
    mr = _mod_row(latent_only)
    n_tiles = _n_tiles(latent_only)
    n_tok = n_tiles * TM
    row_blk = lambda w: pl.BlockSpec((TM, w), lambda i: (i, 0))
    const = lambda shape: pl.BlockSpec(shape, lambda i: (0, 0))
    return pl.pallas_call(
        _proj_ln_kernel,
        grid=(n_tiles,),
        in_specs=[
            pl.BlockSpec((TM, half), lambda i: (i, 0)),
            pl.BlockSpec((TM, half), lambda i: (i, col2)),
            pl.BlockSpec((half, D_MODEL), lambda i: (0, 0)),
            pl.BlockSpec((half, D_MODEL), lambda i: (1, 0)),
            pl.BlockSpec((TM, D_MODEL), lambda i: (rt(i), 0)),
            pl.BlockSpec((1, 6, D_MODEL), lambda i: (mr(i), 0, 0)),
            const((1, D_MODEL)), const((1, D_MODEL)), const((D_MODEL, LANES)), const((1, LANES)),
        ],
        out_specs=[row_blk(D_MODEL), row_blk(LANES), row_blk(LANES), const((8, LANES))],
        out_shape=[jax.ShapeDtypeStruct((n_tok, D_MODEL), F32),
                   jax.ShapeDtypeStruct((n_tok, LANES), jnp.int32),
                   jax.ShapeDtypeStruct((n_tok, LANES), F32),
                   jax.ShapeDtypeStruct((8, LANES), F32)],
        scratch_shapes=[pltpu.VMEM((8, LANES), F32)],
        compiler_params=_cparams(("arbitrary",)),
        name="proj_ln_route",
    )(y1, y2, w_out_bf16, w_out_bf16, x, mod_l, ln_g, ln_b, w_route, b_route)


def _pack_rows(x):
    half = x.shape[-1] // 2
    bits = lambda t: lax.bitcast_convert_type(t.astype(BF16).astype(F32), jnp.uint32)
    return (bits(x[:, :half]) >> 16) | (bits(x[:, half:]) & jnp.uint32(0xFFFF0000))


def _unpack_rows(w):
    lo = lax.bitcast_convert_type(w << 16, F32)
    hi = lax.bitcast_convert_type(w & jnp.uint32(0xFFFF0000), F32)
    return jnp.concatenate([lo, hi], axis=-1)


def _each_row(fn):
    def body(g, carry):
        for j in range(ROW_UNROLL):
            for k in range(2):
                fn(g, j, k)
        return carry
    lax.fori_loop(0, TM // ROW_UNROLL, body, 0)


def _dispatch_kernel(pad_end_ref, padded_ref, x_ref, mod_ref, dest_hbm, xs_hbm,
                     idx_smem, u_scr, zero_scr, sem_idx, sem_row, sem_zero):
    i = pl.program_id(0)
    slot = i % 2
    per_tile = 2 * TM

    @pl.when(i == 0)
    def _():
        zero_scr[...] = jnp.zeros_like(zero_scr)

        def zero_block(first_row):
            rows = pl.ds(pl.multiple_of(first_row, MOE_ROWS), MOE_ROWS)
            return pltpu.make_async_copy(zero_scr, xs_hbm.at[rows], sem_zero)

        n_rows = xs_hbm.shape[0]
        total = pad_end_ref[N_EXPERTS - 1]
        for e in range(N_EXPERTS):
            pl.when(padded_ref[e] > 0)(lambda e=e: zero_block(pad_end_ref[e] - MOE_ROWS).start())
            pl.when(total + e * MOE_ROWS < n_rows)(lambda e=e: zero_block(total + e * MOE_ROWS).start())
        for e in range(N_EXPERTS):
            pl.when(padded_ref[e] > 0)(lambda e=e: zero_block(0).wait())
            pl.when(total + e * MOE_ROWS < n_rows)(lambda e=e: zero_block(0).wait())

    idx_copy = pltpu.make_async_copy(dest_hbm.at[pl.ds(i * per_tile, per_tile)],
                                     idx_smem.at[pl.ds(slot * per_tile, per_tile)], sem_idx)
    idx_copy.start()
    u = x_ref[...] * (1.0 + mod_ref[0, 4:5, :]) + mod_ref[0, 3:4, :]
    u_scr[slot] = _pack_rows(u).reshape(TM // ROW_UNROLL, ROW_UNROLL, PACKED)
    idx_copy.wait()

    def row_copy(sl, g, j, dst_row):
        return pltpu.make_async_copy(u_scr.at[sl, g, pl.ds(j, 1)], xs_hbm.at[pl.ds(dst_row, 1)], sem_row.at[sl])

    def dest_of(sl, g, j, k):
        return idx_smem[sl * per_tile + g * (2 * ROW_UNROLL) + (2 * j + k)]

    _each_row(lambda g, j, k: row_copy(slot, g, j, dest_of(slot, g, j, k)).start())

    @pl.when(i > 0)
    def _():
        _each_row(lambda g, j, k: row_copy(1 - slot, g, j, 0).wait())

    @pl.when(i == pl.num_programs(0) - 1)
    def _():
        _each_row(lambda g, j, k: row_copy(slot, g, j, 0).wait())


def _dispatch(pad_end, padded, x1, mod_l, dest, n_blocks, latent_only):
    mr = _mod_row(latent_only)
    return pl.pallas_call(
        _dispatch_kernel,
        grid_spec=pltpu.PrefetchScalarGridSpec(
            num_scalar_prefetch=2,
            grid=(_n_tiles(latent_only),),
            in_specs=[
                pl.BlockSpec((TM, D_MODEL), lambda i, pe, pd: (i, 0)),
                pl.BlockSpec((1, 6, D_MODEL), lambda i, pe, pd: (mr(i), 0, 0)),
                pl.BlockSpec(memory_space=pl.ANY),
            ],
            out_specs=pl.BlockSpec(memory_space=pl.ANY),
            scratch_shapes=[
                pltpu.SMEM((2 * 2 * TM,), jnp.int32),
                pltpu.VMEM((2, TM // ROW_UNROLL, ROW_UNROLL, PACKED), jnp.uint32),
                pltpu.VMEM((MOE_ROWS, PACKED), jnp.uint32),
                pltpu.SemaphoreType.DMA(()),
                pltpu.SemaphoreType.DMA((2,)),
                pltpu.SemaphoreType.DMA(()),
            ],
        ),
        out_shape=jax.ShapeDtypeStruct((n_blocks * MOE_ROWS, PACKED), jnp.uint32),
        compiler_params=_cparams(("arbitrary",)),
        name="moe_dispatch",
    )(pad_end, padded, x1, mod_l, dest)


def _expert_kernel(blk_exp_ref, n_used_ref, x_ref, wg_ref, wu_ref, wd_ref, o_ref):
    used = pl.program_id(0) < n_used_ref[0]

    @pl.when(used)
    def _():
        x = _unpack_rows(x_ref[...]).astype(BF16)
        gate = jnp.dot(x, wg_ref[0, 0].astype(BF16), preferred_element_type=F32)
        up = jnp.dot(x, wu_ref[0, 0].astype(BF16), preferred_element_type=F32)
        hid = (_silu(gate) * up).astype(BF16)
        o_ref[...] = _pack_rows(jnp.dot(hid, wd_ref[0, 0].astype(BF16), preferred_element_type=F32))

    @pl.when(jnp.logical_not(used))
    def _():
        o_ref[...] = jnp.zeros_like(o_ref)


def _experts(blk_exp, n_used, xs, layer, w_gate, w_up, w_down):
    n_blocks = xs.shape[0] // MOE_ROWS
    row_in = pl.BlockSpec((MOE_ROWS, PACKED), lambda i, be, nu: (jnp.minimum(i, nu[0] - 1), 0))
    w_in = pl.BlockSpec((1, 1, D_MODEL, D_EXPERT), lambda i, be, nu: (layer, be[i], 0, 0))
    w_out = pl.BlockSpec((1, 1, D_EXPERT, D_MODEL), lambda i, be, nu: (layer, be[i], 0, 0))
    return pl.pallas_call(
        _expert_kernel,
        grid_spec=pltpu.PrefetchScalarGridSpec(
            num_scalar_prefetch=2,
            grid=(n_blocks,),
            in_specs=[row_in, w_in, w_in, w_out],
            out_specs=pl.BlockSpec((MOE_ROWS, PACKED), lambda i, be, nu: (i, 0)),
        ),
        out_shape=jax.ShapeDtypeStruct(xs.shape, jnp.uint32),
        compiler_params=_cparams(("arbitrary",)),
        name="moe_experts",
    )(blk_exp, n_used, xs, w_gate, w_up, w_down)


def _combine_ln_kernel(ys_hbm, dest_hbm, mf_ref, x_ref, mod_ref, g_ref, b_ref, o_ref,
                       idx_smem, y_buf, sem_idx, sem_row):
    i = pl.program_id(0)
    slot = i % 2
    per_tile = 2 * TM

    def row_copy(sl, g, j, k, src_row):
        return pltpu.make_async_copy(ys_hbm.at[pl.ds(src_row, 1)], y_buf.at[sl, k, g, pl.ds(j, 1)], sem_row.at[sl])

    def request(tile, sl):
        idx_copy = pltpu.make_async_copy(dest_hbm.at[pl.ds(tile * per_tile, per_tile)],
                                         idx_smem.at[pl.ds(sl * per_tile, per_tile)], sem_idx)
        idx_copy.start()
        idx_copy.wait()
        _each_row(lambda g, j, k: row_copy(
            sl, g, j, k, idx_smem[sl * per_tile + g * (2 * ROW_UNROLL) + (2 * j + k)]).start())

    pl.when(i == 0)(lambda: request(0, 0))
    pl.when(i + 1 < pl.num_programs(0))(lambda: request(i + 1, 1 - slot))
    _each_row(lambda g, j, k: row_copy(slot, g, j, k, 0).wait())

    mf = mf_ref[...]
    y0 = _unpack_rows(y_buf[slot, 0].reshape(TM, PACKED))
    y1 = _unpack_rows(y_buf[slot, 1].reshape(TM, PACKED))
    y = mf[:, 0:1] * y0 + mf[:, 1:2] * y1
    r = DEEPNORM_ALPHA * x_ref[...] + mod_ref[0, 5:6, :] * y
    o_ref[...] = _layer_norm(r, g_ref[...], b_ref[...])


def _combine_ln(ys, dest, mf, x1, mod_l, ln_g, ln_b, latent_only):
    mr = _mod_row(latent_only)
    n_tiles = _n_tiles(latent_only)
    return pl.pallas_call(
        _combine_ln_kernel,
        grid=(n_tiles,),
        in_specs=[
            pl.BlockSpec(memory_space=pl.ANY),
            pl.BlockSpec(memory_space=pl.ANY),
            pl.BlockSpec((TM, LANES), lambda i: (i, 0)),
            pl.BlockSpec((TM, D_MODEL), lambda i: (i, 0)),
            pl.BlockSpec((1, 6, D_MODEL), lambda i: (mr(i), 0, 0)),
            pl.BlockSpec((1, D_MODEL), lambda i: (0, 0)),
            pl.BlockSpec((1, D_MODEL), lambda i: (0, 0)),
        ],
        out_specs=pl.BlockSpec((TM, D_MODEL), lambda i: (i, 0)),
        out_shape=jax.ShapeDtypeStruct((n_tiles * TM, D_MODEL), F32),
        scratch_shapes=[
            pltpu.SMEM((2 * 2 * TM,), jnp.int32),
            pltpu.VMEM((2, 2, TM // ROW_UNROLL, ROW_UNROLL, PACKED), jnp.uint32),
            pltpu.SemaphoreType.DMA(()),
            pltpu.SemaphoreType.DMA((2,)),
        ],
        compiler_params=_cparams(("arbitrary",)),
        name="combine_ln",
    )(ys, dest, mf, x1, mod_l, ln_g, ln_b)


def _route_params(w_grp, b_grp, w_rexp, b_rexp):
    pad = LANES - N_GROUPS - N_EXPERTS
    w_route = jnp.concatenate([w_grp, w_rexp, jnp.zeros((D_MODEL, pad), F32)], axis=1)
    b_route = jnp.concatenate([b_grp, b_rexp, jnp.zeros((pad,), F32)])[None, :]
    return w_route, b_route


def _moe(x1, mi, mf, cnt, mod_l, layer, w_gate, w_up, w_down, ln_g, ln_b, latent_only):
    counts = cnt[0, :N_EXPERTS].astype(jnp.int32)
    padded = (counts + MOE_ROWS - 1) // MOE_ROWS * MOE_ROWS
    pad_end = jnp.cumsum(padded)
    pad_start = pad_end - padded
    n_tok = x1.shape[0]
    n_blocks = (2 * n_tok) // MOE_ROWS + N_EXPERTS
    experts = jnp.arange(N_EXPERTS, dtype=jnp.int32)
    start_of = jnp.sum(jnp.where(mi[:, 0:2, None] == experts, pad_start, 0), axis=-1)
    dest = (start_of + mi[:, 2:4]).reshape(-1)
    blk_start = jnp.arange(n_blocks, dtype=jnp.int32) * MOE_ROWS
    blk_exp = jnp.minimum(jnp.sum((pad_end[None, :] <= blk_start[:, None]).astype(jnp.int32), axis=1),
                          N_EXPERTS - 1)
    n_used = pad_end[-1:] // MOE_ROWS
    xs = _dispatch(pad_end, padded, x1, mod_l, dest, n_blocks, latent_only)
    ys = _experts(blk_exp, n_used, xs, layer, w_gate, w_up, w_down)
    return _combine_ln(ys, dest, mf, x1, mod_l, ln_g, ln_b, latent_only)


def kernel(x, c, ctx, c_ctx, ada_w, ada_b, ln1_g, ln1_b, ln2_g, ln2_b, ab_w_in, ab_w_lr_f, ab_b_lr_f, ab_w_lr_b, ab_b_lr_b, ab_gn_a, ab_gn_b, ab_w_out, c_w_qkv, c_lq1, c_lk1, c_lq2, c_lk2, c_subln_g, c_w_out, moe_w_grp, moe_b_grp, moe_w_rexp, moe_b_rexp, moe_w_gate, moe_w_up, moe_w_down):
    assert x.shape == (BATCH, SEQ, D_MODEL) and ctx.shape == (BATCH, CTX_LEN, D_MODEL)
    xs = jnp.concatenate([ctx, x], axis=1).reshape(N_ALL, D_MODEL)
    c_all = jnp.concatenate([c, c_ctx[None, :], jnp.zeros((16 - BATCH - 1, D_MODEL), F32)], axis=0)
    mod = _ada_tables(c_all, ada_w, ada_b).reshape(DEPTH, 16, 6, D_MODEL)

    rope_a = _rope_tables(DK_A, 1)
    rope_c = _rope_tables(DH_C, 2)
    dec_a = _retention_tables()

    for l in range(DEPTH):
        last = l == DEPTH - 1
        i = l // 2
        mod_l = mod[l]
        row = lambda v: v[None, :]
        route = _route_params(moe_w_grp[l], moe_b_grp[l], moe_w_rexp[l], moe_b_rexp[l])
        if l % 2 == 0:
            assert not last
            w_in = jnp.pad(ab_w_in[i], ((0, 0), (0, AB_COLS - ab_w_in.shape[2]))).astype(BF16)
            z = _mod_matmul(xs, mod_l, w_in)
            ya = _scan_a(z, *rope_a, dec_a, row(ab_gn_a[i]))
            wf = ab_w_lr_f[i].reshape(GLA_RANK, H_B // 2, LANES)
            wb = ab_w_lr_b[i].reshape(GLA_RANK, H_B // 2, LANES)
            wlr = jnp.zeros((H_B // 2, LANES, 2 * LANES), F32)
            wlr = wlr.at[:, 0:GLA_RANK, 0:LANES].set(jnp.swapaxes(wf, 0, 1))
            wlr = wlr.at[:, GLA_RANK:2 * GLA_RANK, LANES:].set(jnp.swapaxes(wb, 0, 1))
            blr = jnp.concatenate([ab_b_lr_f[i].reshape(H_B // 2, 1, LANES),
                                   ab_b_lr_b[i].reshape(H_B // 2, 1, LANES)], axis=-1)
            yb = _scan_b(z, wlr, blr, row(ab_gn_b[i]))
            x1, mi, mf, cnt = _proj_ln(ya, yb, 0, ab_w_out[i].astype(BF16), xs, mod_l, row(ln1_g[l]),
                                       row(ln1_b[l]), *route, False)
        else:
            lam_init = 0.8 - 0.6 * math.exp(-0.3 * l)
            lam = (jnp.exp(jnp.sum(c_lq1[i] * c_lk1[i], axis=-1))
                   - jnp.exp(jnp.sum(c_lq2[i] * c_lk2[i], axis=-1))).astype(F32) + lam_init
            z = _mod_matmul(xs, mod_l, c_w_qkv[i].astype(BF16))
            gsub = row(c_subln_g[i])
            y = _attention(z, lam, *rope_c, gsub, 1.0 - lam_init, last)
            x1, mi, mf, cnt = _proj_ln(y, y, 1, c_w_out[i].astype(BF16), xs, mod_l, row(ln1_g[l]),
                                       row(ln1_b[l]), *route, last)
        xs = _moe(x1, mi, mf, cnt, mod_l, l, moe_w_gate, moe_w_up, moe_w_down, row(ln2_g[l]), row(ln2_b[l]), last)
    return xs.reshape(BATCH, SEQ, D_MODEL)
```

```python
import functools
import math

import numpy as np
import jax
import jax.numpy as jnp
from jax import lax
from jax.experimental import pallas as pl
from jax.experimental.pallas import tpu as pltpu

F32 = jnp.float32
BF16 = jnp.bfloat16

D_MODEL = 1024
BATCH = 8
SEQ = 2048
DEPTH = 4
GRID_W = 64
CTX_LEN = 256
ROPE_BASE = 10000.0
LN_EPS = 1e-5
DEEPNORM_ALPHA = (2 * DEPTH) ** 0.25
H_A = 4
DK_A = 128
DV_A = 128
CHUNK_A = 128
RET_EXP_FWD = 5.0
RET_EXP_BWD = 5.5
H_B = 4
DK_B = 64
DV_B = 128
GLA_RANK = 16
GLA_TAU = 16.0
CHUNK_B = 64
H_C = 8
DH_C = 64
DV_C = 128
N_GROUPS = 4
EXPERTS_PER_GROUP = 8
N_EXPERTS = 32
D_EXPERT = 512

LANES = 128
T_ALL = CTX_LEN + SEQ
N_ALL = BATCH * T_ALL
TM = 256
TILES_PER_BATCH = T_ALL // TM
LATENT_TILES_PER_BATCH = SEQ // TM
AB_COLS = 29 * LANES
MOE_ROWS = TM
PACKED = D_MODEL // 2
ROW_UNROLL = 8
SCAN_UNROLL = 8
GLA_GROUP = 256
TQ = 256
VMEM_LIMIT = 56 * 1024 * 1024


def _cparams(sem):
    return pltpu.CompilerParams(dimension_semantics=sem, vmem_limit_bytes=VMEM_LIMIT)


def _silu(v):
    return v * (1.0 / (1.0 + jnp.exp(-v)))


def _n_tiles(latent_only):
    return BATCH * (LATENT_TILES_PER_BATCH if latent_only else TILES_PER_BATCH)


def _row_tile(latent_only):
    if latent_only:
        return lambda i: (i // LATENT_TILES_PER_BATCH) * TILES_PER_BATCH + 1 + i % LATENT_TILES_PER_BATCH
    return lambda i: i


def _mod_row(latent_only):
    if latent_only:
        return lambda i: i // LATENT_TILES_PER_BATCH
    return lambda i: jnp.where(i % TILES_PER_BATCH == 0, BATCH, i // TILES_PER_BATCH)


def _ada_kernel(c_ref, w_ref, b_ref, o_ref):
    sc = _silu(c_ref[...])
    o_ref[0] = jnp.dot(sc.astype(BF16), w_ref[0].astype(BF16), preferred_element_type=F32) + b_ref[0]


def _ada_tables(c_all, ada_w, ada_b):
    tn = 1536
    n_out = 6 * D_MODEL
    return pl.pallas_call(
        _ada_kernel,
        grid=(DEPTH, n_out // tn),
        in_specs=[
            pl.BlockSpec((16, D_MODEL), lambda l, j: (0, 0)),
            pl.BlockSpec((1, D_MODEL, tn), lambda l, j: (l, 0, j)),
            pl.BlockSpec((1, 1, tn), lambda l, j: (l, 0, j)),
        ],
        out_specs=pl.BlockSpec((1, 16, tn), lambda l, j: (l, 0, j)),
        out_shape=jax.ShapeDtypeStruct((DEPTH, 16, n_out), F32),
        compiler_params=_cparams(("arbitrary", "arbitrary")),
        name="ada_tables",
    )(c_all, ada_w, ada_b.reshape(DEPTH, 1, n_out))


def _modmm_kernel(x_ref, mod_ref, w_ref, o_ref):
    u = x_ref[...] * (1.0 + mod_ref[0, 1:2, :]) + mod_ref[0, 0:1, :]
    o_ref[...] = jnp.dot(u.astype(BF16), w_ref[...], preferred_element_type=F32)


def _mod_matmul(x, mod_l, w_bf16):
    n_out = w_bf16.shape[1]
    return pl.pallas_call(
        _modmm_kernel,
        grid=(N_ALL // TM,),
        in_specs=[
            pl.BlockSpec((TM, D_MODEL), lambda i: (i, 0)),
            pl.BlockSpec((1, 6, D_MODEL), lambda i: (_mod_row(False)(i), 0, 0)),
            pl.BlockSpec((D_MODEL, n_out), lambda i: (0, 0)),
        ],
        out_specs=pl.BlockSpec((TM, n_out), lambda i: (i, 0)),
        out_shape=jax.ShapeDtypeStruct((N_ALL, n_out), F32),
        compiler_params=_cparams(("arbitrary",)),
        name="mod_matmul",
    )(x, mod_l, w_bf16)


def _rope_tables(head_dim, reps):
    rows = SEQ // GRID_W
    row = np.repeat(np.arange(rows, dtype=np.float32), GRID_W)
    col = np.tile(np.arange(GRID_W, dtype=np.float32), rows)
    quarter = head_dim // 4
    inv = (ROPE_BASE ** (-np.arange(quarter, dtype=np.float32) / quarter)).astype(np.float32)
    ang_r = row[:, None] * inv
    ang_c = col[:, None] * inv
    ang = np.concatenate([ang_r, ang_r, ang_c, ang_c], axis=-1)
    cos = np.cos(ang).astype(np.float32)
    sin = np.sin(ang).astype(np.float32)
    q_idx = (np.arange(head_dim) // quarter) % 2
    sin_up = np.where(q_idx == 1, sin, 0.0).astype(np.float32)
    sin_dn = np.where(q_idx == 0, -sin, 0.0).astype(np.float32)

    def full(t, ctx_val):
        t = np.tile(t, (1, reps))
        return jnp.asarray(np.concatenate([np.full((CTX_LEN, t.shape[1]), ctx_val, np.float32), t], axis=0))

    return full(cos, 1.0), full(sin_up, 0.0), full(sin_dn, 0.0)


def _rope(x, cos, sin_up, sin_dn, quarter):
    width = x.shape[-1]
    return x * cos + pltpu.roll(x, quarter, 1) * sin_up + pltpu.roll(x, width - quarter, 1) * sin_dn


def _dot_tb(a, b):
    return lax.dot_general(a, b, (((1,), (1,)), ((), ())), preferred_element_type=F32)


def _dot_ta(a, b):
    return lax.dot_general(a, b, (((0,), (0,)), ((), ())), preferred_element_type=F32)


def _split_bf16(x, parts):
    out = []
    for _ in range(parts):
        t = x.astype(BF16)
        out.append(t)
        x = x - t.astype(F32)
    return out


def _dot_split(a, b):
    a_hi, a_lo = _split_bf16(a, 2)
    b_hi, b_lo = _split_bf16(b, 2)
    dot = lambda u, v: jnp.dot(u, v, preferred_element_type=F32)
    return dot(a_hi, b_hi) + (dot(a_hi, b_lo) + dot(a_lo, b_hi))


def _dot_mask(mask_bf16, x):
    return sum(jnp.dot(mask_bf16, t, preferred_element_type=F32) for t in reversed(_split_bf16(x, 3)))


def _retention_tables():
    c = CHUNK_A
    i = np.arange(c, dtype=np.float64)
    out = np.zeros((H_A, 7, c, LANES), np.float64)
    for h in range(H_A):
        lgf = np.log1p(-np.exp2(-(RET_EXP_FWD + h)))
        lgb = np.log1p(-np.exp2(-(RET_EXP_BWD + h)))
        d = i[:, None] - i[None, :]
        out[h, 0] = np.where(d >= 0, np.exp(lgf * d), np.exp(lgb * (-d - 1)))
        out[h, 1] = np.exp(lgf * (i + 1))[:, None]
        out[h, 2] = np.exp(lgb * (c - 1 - i))[:, None]
        out[h, 3] = np.exp(lgf * (c - 1 - i))[:, None]
        out[h, 4] = np.exp(lgb * i)[:, None]
        out[h, 5] = np.exp(lgf * c)
        out[h, 6] = np.exp(lgb * c)
    return jnp.asarray(out.astype(np.float32))


def _scan_a_kernel(q_ref, k_ref, v_ref, g_ref, cos_ref, sup_ref, sdn_ref, dec_ref, gn_ref, o_ref,
                   kr_scr, sb_scr):
    c = CHUNK_A
    n_ctx = CTX_LEN // c
    n_all = T_ALL // c
    scale = DK_A ** -0.5
    kr_scr[...] = _rope(k_ref[...], cos_ref[...], sup_ref[...], sdn_ref[...], DK_A // 4)
    dmat = dec_ref[0, 0]
    q_f, q_b, k_f, k_b = dec_ref[0, 1], dec_ref[0, 2], dec_ref[0, 3], dec_ref[0, 4]
    g_fc, g_bc = dec_ref[0, 5], dec_ref[0, 6]
    gn = gn_ref[...]
    zero = jnp.zeros((DK_A, DV_A), F32)

    def chunk(ci):
        return pl.ds(pl.multiple_of(ci * c, c), c)

    def kv_state(ci, k_dec):
        sl = chunk(ci)
        return _dot_ta((kr_scr[sl, :] * k_dec).astype(BF16), v_ref[sl, :].astype(BF16))

    def run(lo, hi, sf0, sb0):
        def bwd(j, sb):
            ci = hi - 1 - j
            sb_scr[ci] = sb
            return g_bc * sb + kv_state(ci, k_b)

        sb_fin = lax.fori_loop(0, hi - lo, bwd, sb0, unroll=SCAN_UNROLL)

        def fwd(j, sf):
            ci = lo + j
            sl = chunk(ci)
            q = _rope(q_ref[sl, :], cos_ref[sl, :], sup_ref[sl, :], sdn_ref[sl, :], DK_A // 4) * scale
            k = kr_scr[sl, :]
            vb = v_ref[sl, :].astype(BF16)
            att = _dot_tb(q.astype(BF16), k.astype(BF16)) * dmat
            o = jnp.dot(att.astype(BF16), vb, preferred_element_type=F32)
            o = o + jnp.dot((q * q_f).astype(BF16), sf.astype(BF16), preferred_element_type=F32)
            o = o + jnp.dot((q * q_b).astype(BF16), sb_scr[ci].astype(BF16), preferred_element_type=F32)
            o = o - jnp.mean(o, axis=-1, keepdims=True)
            o = o * lax.rsqrt(jnp.mean(o * o, axis=-1, keepdims=True) + LN_EPS)
            o_ref[sl, :] = _silu(g_ref[sl, :]) * (o * gn)
            return g_fc * sf + _dot_ta((k * k_f).astype(BF16), vb)

        sf_fin = lax.fori_loop(0, hi - lo, fwd, sf0, unroll=SCAN_UNROLL)
        return sf_fin, sb_fin

    sf_c, sb_c = run(0, n_ctx, zero, zero)
    run(n_ctx, n_all, sf_c, sb_c)


def _scan_a(z, cos, sup, sdn, dec, gn_a):
    blk = lambda col0: pl.BlockSpec((T_ALL, LANES), lambda b, h: (b, col0 + h))
    tbl = pl.BlockSpec((T_ALL, LANES), lambda b, h: (0, 0))
    return pl.pallas_call(
        _scan_a_kernel,
        grid=(BATCH, H_A),
        in_specs=[blk(0), blk(4), blk(8), blk(12), tbl, tbl, tbl,
                  pl.BlockSpec((1, 7, CHUNK_A, LANES), lambda b, h: (h, 0, 0, 0)),
                  pl.BlockSpec((1, LANES), lambda b, h: (0, h))],
        out_specs=pl.BlockSpec((T_ALL, LANES), lambda b, h: (b, h)),
        out_shape=jax.ShapeDtypeStruct((N_ALL, H_A * DV_A), F32),
        scratch_shapes=[pltpu.VMEM((T_ALL, LANES), F32),
                        pltpu.VMEM((T_ALL // CHUNK_A, DK_A, DV_A), F32)],
        compiler_params=_cparams(("arbitrary", "arbitrary")),
        name="scan_retention",
    )(z, z, z, z, cos, sup, sdn, dec, gn_a)


def _log_sigmoid(g):
    return jnp.minimum(g, 0.0) - jnp.log1p(jnp.exp(-jnp.abs(g)))


def _scan_b_kernel(q_ref, k_ref, v_ref, g_ref, lr_ref, wlr_ref, blr_ref, gn_ref, o_ref,
                   qf_scr, kf_scr, qb_scr, kb_scr, ktf_scr, ktb_scr, ef_scr, eb_scr, sb_scr):
    c = CHUNK_B
    n_ctx = CTX_LEN // c
    n_all = T_ALL // c
    per_group = GLA_GROUP // c
    scale = DK_B ** -0.5

    gi_r = lax.broadcasted_iota(jnp.int32, (GLA_GROUP, GLA_GROUP), 0)
    gi_c = lax.broadcasted_iota(jnp.int32, (GLA_GROUP, GLA_GROUP), 1)
    same_chunk = (gi_r // c) == (gi_c // c)
    prefix = (same_chunk & (gi_c <= gi_r)).astype(BF16)
    suffix = (same_chunk & (gi_c >= gi_r)).astype(BF16)

    def prepare(gi, carry):
        sl = pl.ds(pl.multiple_of(gi * GLA_GROUP, GLA_GROUP), GLA_GROUP)
        gates = _dot_split(lr_ref[sl, :], wlr_ref[0]) + blr_ref[0]
        laf = _log_sigmoid(gates[:, :LANES]) * (1.0 / GLA_TAU)
        lab = _log_sigmoid(gates[:, LANES:]) * (1.0 / GLA_TAU)
        b = _dot_mask(prefix, laf)
        rb = _dot_mask(suffix, lab)
        q = q_ref[sl, :] * scale
        k = k_ref[sl, :]
        qf_scr[sl, :] = (q * jnp.exp(b)).astype(BF16)
        kf_scr[sl, :] = (k * jnp.exp(-b)).astype(BF16)
        qb_scr[sl, :] = (q * jnp.exp(rb - lab)).astype(BF16)
        kb_scr[sl, :] = (k * jnp.exp(-rb)).astype(BF16)
        b3 = b.reshape(per_group, c, LANES)
        rb3 = rb.reshape(per_group, c, LANES)
        k3 = k.reshape(per_group, c, LANES)
        b_tot = b3[:, c - 1:c, :]
        rb_tot = rb3[:, 0:1, :]
        ktf_scr[sl, :] = (k3 * jnp.exp(b_tot - b3)).reshape(GLA_GROUP, LANES).astype(BF16)
        ktb_scr[sl, :] = (k3 * jnp.exp(rb_tot - rb3)).reshape(GLA_GROUP, LANES).astype(BF16)
        for m in range(per_group):
            ef_scr[gi * per_group + m] = jnp.broadcast_to(jnp.exp(b_tot[m]), (8, LANES))
            eb_scr[gi * per_group + m] = jnp.broadcast_to(jnp.exp(rb_tot[m]), (8, LANES))
        return carry

    lax.fori_loop(0, T_ALL // GLA_GROUP, prepare, 0)

    lane = lax.broadcasted_iota(jnp.int32, (1, LANES), 1)
    masks = [lane < DK_B, lane >= DK_B]
    ri = lax.broadcasted_iota(jnp.int32, (c, c), 0)
    cj = lax.broadcasted_iota(jnp.int32, (c, c), 1)
    lower = cj <= ri
    gn = gn_ref[...]
    zero = jnp.zeros((DV_B, LANES), F32)
    zero_b = jnp.zeros((), BF16)

    def chunk(ci):
        return pl.ds(pl.multiple_of(ci * c, c), c)

    def run(lo, hi, sf0, sb0):
        def bwd(j, sb):
            ci = hi - 1 - j
            sl = chunk(ci)
            e_tot = eb_scr[ci][0:1, :]
            kt = ktb_scr[sl, :]
            v = v_ref[sl, :]
            new = []
            for h in range(2):
                sb_scr[ci, h] = sb[h]
                vh = v[:, h * DV_B:(h + 1) * DV_B].astype(BF16)
                new.append(sb[h] * e_tot + _dot_ta(vh, jnp.where(masks[h], kt, zero_b)))
            return tuple(new)

        sb_fin = lax.fori_loop(0, hi - lo, bwd, sb0, unroll=SCAN_UNROLL)

        def fwd(j, sf):
            ci = lo + j
            sl = chunk(ci)
            e_tot = ef_scr[ci][0:1, :]
            qf, kf, qb, kb, kt = qf_scr[sl, :], kf_scr[sl, :], qb_scr[sl, :], kb_scr[sl, :], ktf_scr[sl, :]
            v = v_ref[sl, :]
            g = g_ref[sl, :]
            new = []
            for h in range(2):
                pick = lambda t: jnp.where(masks[h], t, zero_b)
                vh = v[:, h * DV_B:(h + 1) * DV_B].astype(BF16)
                qfh, qbh = pick(qf), pick(qb)
                att = jnp.where(lower, _dot_tb(qfh, pick(kf)), _dot_tb(qbh, pick(kb)))
                o = jnp.dot(att.astype(BF16), vh, preferred_element_type=F32)
                o = o + _dot_tb(qfh, sf[h].astype(BF16))
                o = o + _dot_tb(qbh, sb_scr[ci, h].astype(BF16))
                o = o * lax.rsqrt(jnp.mean(o * o, axis=-1, keepdims=True) + LN_EPS)
                cols = slice(h * DV_B, (h + 1) * DV_B)
                o_ref[sl, cols] = _silu(g[:, cols]) * (o * gn[:, cols])
                new.append(sf[h] * e_tot + _dot_ta(vh, pick(kt)))
            return tuple(new)

        sf_fin = lax.fori_loop(0, hi - lo, fwd, sf0, unroll=SCAN_UNROLL)
        return sf_fin, sb_fin

    sf_c, sb_c = run(0, n_ctx, (zero, zero), (zero, zero))
    run(n_ctx, n_all, sf_c, sb_c)


def _scan_b(z, wlr, blr, gn_b):
    pairs = H_B // 2
    return pl.pallas_call(
        _scan_b_kernel,
        grid=(BATCH, pairs),
        in_specs=[
            pl.BlockSpec((T_ALL, LANES), lambda b, p: (b, 16 + p)),
            pl.BlockSpec((T_ALL, LANES), lambda b, p: (b, 18 + p)),
            pl.BlockSpec((T_ALL, 2 * DV_B), lambda b, p: (b, 10 + p)),
            pl.BlockSpec((T_ALL, 2 * DV_B), lambda b, p: (b, 12 + p)),
            pl.BlockSpec((T_ALL, LANES), lambda b, p: (b, 28)),
            pl.BlockSpec((1, LANES, 2 * LANES), lambda b, p: (p, 0, 0)),
            pl.BlockSpec((1, 1, 2 * LANES), lambda b, p: (p, 0, 0)),
            pl.BlockSpec((1, 2 * DV_B), lambda b, p: (0, p)),
        ],
        out_specs=pl.BlockSpec((T_ALL, 2 * DV_B), lambda b, p: (b, p)),
        out_shape=jax.ShapeDtypeStruct((N_ALL, H_B * DV_B), F32),
        scratch_shapes=[pltpu.VMEM((T_ALL, LANES), BF16)] * 6
                       + [pltpu.VMEM((T_ALL // CHUNK_B, 8, LANES), F32)] * 2
                       + [pltpu.VMEM((T_ALL // CHUNK_B, 2, DV_B, LANES), F32)],
        compiler_params=_cparams(("arbitrary", "arbitrary")),
        name="scan_gla",
    )(z, z, z, z, z, wlr, blr, gn_b)


def _attn_kernel(lam_ref, q_ref, k_ref, v_ref, qcos_ref, qsup_ref, qsdn_ref, kcos_ref, ksup_ref, ksdn_ref,
                 gsub_ref, o_ref, k_scr, v_scr, *, post_scale, tile0):
    h = pl.program_id(1)
    t = pl.program_id(2)
    quarter = DH_C // 4
    scale = DH_C ** -0.5

    @pl.when(t == 0)
    def _():
        k_scr[...] = _rope(k_ref[...], kcos_ref[...], ksup_ref[...], ksdn_ref[...], quarter).astype(BF16)
        v_scr[...] = v_ref[...].astype(BF16)

    lam = lam_ref[h]
    lane = lax.broadcasted_iota(jnp.int32, (1, LANES), 1)
    m1 = (lane < DH_C).astype(F32)
    m2 = (lane >= DH_C).astype(F32)

    def attend(n_keys):
        q = _rope(q_ref[...], qcos_ref[...], qsup_ref[...], qsdn_ref[...], quarter) * scale
        kb = k_scr[0:n_keys, :]
        s1 = _dot_tb((q * m1).astype(BF16), kb)
        s2 = _dot_tb((q * m2).astype(BF16), kb)
        e1 = jnp.exp(s1 - jnp.max(s1, axis=-1, keepdims=True))
        e2 = jnp.exp(s2 - jnp.max(s2, axis=-1, keepdims=True))
        r1 = 1.0 / jnp.sum(e1, axis=-1, keepdims=True)
        r2 = lam / jnp.sum(e2, axis=-1, keepdims=True)
        a = e1 * r1 - e2 * r2
        o = jnp.dot(a.astype(BF16), v_scr[0:n_keys, :], preferred_element_type=F32)
        o = o * lax.rsqrt(jnp.mean(o * o, axis=-1, keepdims=True) + LN_EPS)
        o_ref[...] = o * (gsub_ref[...] * post_scale)

    if tile0 == 0:
        pl.when(t == 0)(lambda: attend(CTX_LEN))
        pl.when(t > 0)(lambda: attend(T_ALL))
    else:
        attend(T_ALL)


def _attention(z, lam, cos, sup, sdn, gsub, post_scale, latent_only):
    tile0 = 1 if latent_only else 0
    n_qt = TILES_PER_BATCH - tile0
    kern = functools.partial(_attn_kernel, post_scale=post_scale, tile0=tile0)
    kv = lambda col0: pl.BlockSpec((T_ALL, LANES), lambda b, h, t, lam_r: (b, col0 + h))
    q_tbl = pl.BlockSpec((TQ, LANES), lambda b, h, t, lam_r: (tile0 + t, 0))
    k_tbl = pl.BlockSpec((T_ALL, LANES), lambda b, h, t, lam_r: (0, 0))
    return pl.pallas_call(
        kern,
        grid_spec=pltpu.PrefetchScalarGridSpec(
            num_scalar_prefetch=1,
            grid=(BATCH, H_C, n_qt),
            in_specs=[pl.BlockSpec((TQ, LANES), lambda b, h, t, lam_r: (b * TILES_PER_BATCH + tile0 + t, h)),
                      kv(H_C), kv(2 * H_C), q_tbl, q_tbl, q_tbl, k_tbl, k_tbl, k_tbl,
                      pl.BlockSpec((1, LANES), lambda b, h, t, lam_r: (0, h))],
            out_specs=pl.BlockSpec((TQ, LANES), lambda b, h, t, lam_r: (b * n_qt + t, h)),
            scratch_shapes=[pltpu.VMEM((T_ALL, LANES), BF16), pltpu.VMEM((T_ALL, LANES), BF16)],
        ),
        out_shape=jax.ShapeDtypeStruct((BATCH * n_qt * TQ, H_C * DV_C), F32),
        compiler_params=_cparams(("arbitrary", "arbitrary", "arbitrary")),
        name="diff_attention",
    )(lam, z, z, z, cos, sup, sdn, cos, sup, sdn, gsub)


def _layer_norm(r, g, b):
    mu = jnp.mean(r, axis=-1, keepdims=True)
    d = r - mu
    var = jnp.mean(d * d, axis=-1, keepdims=True)
    return d * lax.rsqrt(var + LN_EPS) * g + b


def _route_tile(u, w_ref, b_ref, mi_ref, mf_ref, cnt_ref, carry):
    @pl.when(pl.program_id(0) == 0)
    def _():
        carry[...] = jnp.zeros_like(carry)

    logits = _dot_split(u, w_ref[...]) + b_ref[...]
    lane = lax.broadcasted_iota(jnp.int32, (TM, LANES), 1)
    lane_f = lane.astype(F32)
    neg = -jnp.inf
    big = 1e9

    gmask = lane < N_GROUPS
    gl = jnp.where(gmask, logits, neg)
    gmax = jnp.max(gl, axis=-1, keepdims=True)
    gidx = jnp.min(jnp.where(gl == gmax, lane_f, big), axis=-1, keepdims=True)
    gw = 1.0 / jnp.sum(jnp.where(gmask, jnp.exp(logits - gmax), 0.0), axis=-1, keepdims=True)

    e_lane = lane - N_GROUPS
    in_grp = (e_lane >= 0) & (e_lane < N_EXPERTS) & ((e_lane >> 3) == gidx.astype(jnp.int32))
    el = jnp.where(in_grp, logits, neg)
    v1 = jnp.max(el, axis=-1, keepdims=True)
    i1 = jnp.min(jnp.where(el == v1, lane_f, big), axis=-1, keepdims=True)
    el2 = jnp.where(lane_f == i1, neg, el)
    v2 = jnp.max(el2, axis=-1, keepdims=True)
    i2 = jnp.min(jnp.where(el2 == v2, lane_f, big), axis=-1, keepdims=True)
    t = jnp.exp(v2 - v1)
    c0 = gw / (1.0 + t)
    c1 = gw * t / (1.0 + t)
    e0 = i1 - N_GROUPS
    e1 = i2 - N_GROUPS

    oh0 = lane_f == e0
    oh1 = lane_f == e1
    cnt = oh0.astype(F32) + oh1.astype(F32)
    ri = lax.broadcasted_iota(jnp.int32, (TM, TM), 0)
    cj = lax.broadcasted_iota(jnp.int32, (TM, TM), 1)
    strict = (cj < ri).astype(BF16)
    before = jnp.dot(strict, cnt.astype(BF16), preferred_element_type=F32) + carry[0:1, :]
    r0 = jnp.sum(jnp.where(oh0, before, 0.0), axis=-1, keepdims=True)
    r1 = jnp.sum(jnp.where(oh1, before, 0.0), axis=-1, keepdims=True)
    carry[0:1, :] = carry[0:1, :] + jnp.sum(cnt, axis=0, keepdims=True)

    mi = jnp.where(lane == 0, e0, jnp.where(lane == 1, e1, jnp.where(lane == 2, r0, jnp.where(lane == 3, r1, 0.0))))
    mi_ref[...] = mi.astype(jnp.int32)
    mf_ref[...] = jnp.where(lane == 0, c0, jnp.where(lane == 1, c1, 0.0))
    cnt_ref[...] = carry[...]


def _proj_ln_kernel(y1_ref, y2_ref, w1_ref, w2_ref, x_ref, mod_ref, g_ref, b_ref, wr_ref, br_ref,
                    o_ref, mi_ref, mf_ref, cnt_ref, carry):
    y = jnp.dot(y1_ref[...].astype(BF16), w1_ref[...], preferred_element_type=F32)
    y = y + jnp.dot(y2_ref[...].astype(BF16), w2_ref[...], preferred_element_type=F32)
    r = DEEPNORM_ALPHA * x_ref[...] + mod_ref[0, 2:3, :] * y
    x1 = _layer_norm(r, g_ref[...], b_ref[...])
    o_ref[...] = x1
    u = x1 * (1.0 + mod_ref[0, 4:5, :]) + mod_ref[0, 3:4, :]
    _route_tile(u, wr_ref, br_ref, mi_ref, mf_ref, cnt_ref, carry)


def _proj_ln(y1, y2, col2, w_out_bf16, x, mod_l, ln_g, ln_b, w_route, b_route, latent_only):
    half = D_MODEL // 2
    rt = _row_tile(latent_only)
    mr = _mod_row(latent_only)
    n_tiles = _n_tiles(latent_only)
    n_tok = n_tiles * TM
    row_blk = lambda w: pl.BlockSpec((TM, w), lambda i: (i, 0))
    const = lambda shape: pl.BlockSpec(shape, lambda i: (0, 0))
    return pl.pallas_call(
        _proj_ln_kernel,
        grid=(n_tiles,),
        in_specs=[
            pl.BlockSpec((TM, half), lambda i: (i, 0)),
            pl.BlockSpec((TM, half), lambda i: (i, col2)),
            pl.BlockSpec((half, D_MODEL), lambda i: (0, 0)),
            pl.BlockSpec((half, D_MODEL), lambda i: (1, 0)),
            pl.BlockSpec((TM, D_MODEL), lambda i: (rt(i), 0)),
            pl.BlockSpec((1, 6, D_MODEL), lambda i: (mr(i), 0, 0)),
            const((1, D_MODEL)), const((1, D_MODEL)), const((D_MODEL, LANES)), const((1, LANES)),
        ],
        out_specs=[row_blk(D_MODEL), row_blk(LANES), row_blk(LANES), const((8, LANES))],
        out_shape=[jax.ShapeDtypeStruct((n_tok, D_MODEL), F32),
                   jax.ShapeDtypeStruct((n_tok, LANES), jnp.int32),
                   jax.ShapeDtypeStruct((n_tok, LANES), F32),
                   jax.ShapeDtypeStruct((8, LANES), F32)],
        scratch_shapes=[pltpu.VMEM((8, LANES), F32)],
        compiler_params=_cparams(("arbitrary",)),
        name="proj_ln_route",
    )(y1, y2, w_out_bf16, w_out_bf16, x, mod_l, ln_g, ln_b, w_route, b_route)


def _pack_rows(x):
    half = x.shape[-1] // 2
    bits = lambda t: lax.bitcast_convert_type(t.astype(BF16).astype(F32), jnp.uint32)
    return (bits(x[:, :half]) >> 16) | (bits(x[:, half:]) & jnp.uint32(0xFFFF0000))


def _unpack_rows(w):
    lo = lax.bitcast_convert_type(w << 16, F32)
    hi = lax.bitcast_convert_type(w & jnp.uint32(0xFFFF0000), F32)
    return jnp.concatenate([lo, hi], axis=-1)


def _each_row(fn):
    def body(g, carry):
        for j in range(ROW_UNROLL):
            for k in range(2):
                fn(g, j, k)
        return carry
    lax.fori_loop(0, TM // ROW_UNROLL, body, 0)


def _dispatch_kernel(pad_end_ref, padded_ref, x_ref, mod_ref, dest_hbm, xs_hbm,
                     idx_smem, u_scr, zero_scr, sem_idx, sem_row, sem_zero):
    i = pl.program_id(0)
    slot = i % 2
    per_tile = 2 * TM

    @pl.when(i == 0)
    def _():
        zero_scr[...] = jnp.zeros_like(zero_scr)

        def zero_block(first_row):
            rows = pl.ds(pl.multiple_of(first_row, MOE_ROWS), MOE_ROWS)
            return pltpu.make_async_copy(zero_scr, xs_hbm.at[rows], sem_zero)

        n_rows = xs_hbm.shape[0]
        total = pad_end_ref[N_EXPERTS - 1]
        for e in range(N_EXPERTS):
            pl.when(padded_ref[e] > 0)(lambda e=e: zero_block(pad_end_ref[e] - MOE_ROWS).start())
            pl.when(total + e * MOE_ROWS < n_rows)(lambda e=e: zero_block(total + e * MOE_ROWS).start())
        for e in range(N_EXPERTS):
            pl.when(padded_ref[e] > 0)(lambda e=e: zero_block(0).wait())
            pl.when(total + e * MOE_ROWS < n_rows)(lambda e=e: zero_block(0).wait())

    def idx_copy(tile, sl):
        return pltpu.make_async_copy(dest_hbm.at[pl.ds(tile * per_tile, per_tile)],
                                     idx_smem.at[pl.ds(sl * per_tile, per_tile)], sem_idx)

    pl.when(i == 0)(lambda: idx_copy(0, 0).start())
    idx_copy(i, slot).wait()
    pl.when(i + 1 < pl.num_programs(0))(lambda: idx_copy(i + 1, 1 - slot).start())
    u = x_ref[...] * (1.0 + mod_ref[0, 4:5, :]) + mod_ref[0, 3:4, :]
    u_scr[slot] = _pack_rows(u).reshape(TM // ROW_UNROLL, ROW_UNROLL, PACKED)

    def row_copy(sl, g, j, dst_row):
        return pltpu.make_async_copy(u_scr.at[sl, g, pl.ds(j, 1)], xs_hbm.at[pl.ds(dst_row, 1)], sem_row.at[sl])

    def dest_of(sl, g, j, k):
        return idx_smem[sl * per_tile + g * (2 * ROW_UNROLL) + (2 * j + k)]

    _each_row(lambda g, j, k: row_copy(slot, g, j, dest_of(slot, g, j, k)).start())

    @pl.when(i > 0)
    def _():
        _each_row(lambda g, j, k: row_copy(1 - slot, g, j, 0).wait())

    @pl.when(i == pl.num_programs(0) - 1)
    def _():
        _each_row(lambda g, j, k: row_copy(slot, g, j, 0).wait())


def _dispatch(pad_end, padded, x1, mod_l, dest, n_blocks, latent_only):
    mr = _mod_row(latent_only)
    return pl.pallas_call(
        _dispatch_kernel,
        grid_spec=pltpu.PrefetchScalarGridSpec(
            num_scalar_prefetch=2,
            grid=(_n_tiles(latent_only),),
            in_specs=[
                pl.BlockSpec((TM, D_MODEL), lambda i, pe, pd: (i, 0)),
                pl.BlockSpec((1, 6, D_MODEL), lambda i, pe, pd: (mr(i), 0, 0)),
                pl.BlockSpec(memory_space=pl.ANY),
            ],
            out_specs=pl.BlockSpec(memory_space=pl.ANY),
            scratch_shapes=[
                pltpu.SMEM((2 * 2 * TM,), jnp.int32),
                pltpu.VMEM((2, TM // ROW_UNROLL, ROW_UNROLL, PACKED), jnp.uint32),
                pltpu.VMEM((MOE_ROWS, PACKED), jnp.uint32),
                pltpu.SemaphoreType.DMA(()),
                pltpu.SemaphoreType.DMA((2,)),
                pltpu.SemaphoreType.DMA(()),
            ],
        ),
        out_shape=jax.ShapeDtypeStruct((n_blocks * MOE_ROWS, PACKED), jnp.uint32),
        compiler_params=_cparams(("arbitrary",)),
        name="moe_dispatch",
    )(pad_end, padded, x1, mod_l, dest)


def _expert_kernel(blk_exp_ref, n_used_ref, x_ref, wg_ref, wu_ref, wd_ref, o_ref):
    used = pl.program_id(0) < n_used_ref[0]

    @pl.when(used)
    def _():
        x = _unpack_rows(x_ref[...]).astype(BF16)
        gate = jnp.dot(x, wg_ref[0, 0].astype(BF16), preferred_element_type=F32)
        up = jnp.dot(x, wu_ref[0, 0].astype(BF16), preferred_element_type=F32)
        hid = (_silu(gate) * up).astype(BF16)
        o_ref[...] = _pack_rows(jnp.dot(hid, wd_ref[0, 0].astype(BF16), preferred_element_type=F32))

    @pl.when(jnp.logical_not(used))
    def _():
        o_ref[...] = jnp.zeros_like(o_ref)


def _experts(blk_exp, n_used, xs, layer, w_gate, w_up, w_down):
    n_blocks = xs.shape[0] // MOE_ROWS
    row_in = pl.BlockSpec((MOE_ROWS, PACKED), lambda i, be, nu: (jnp.minimum(i, nu[0] - 1), 0))
    w_in = pl.BlockSpec((1, 1, D_MODEL, D_EXPERT), lambda i, be, nu: (layer, be[i], 0, 0))
    w_out = pl.BlockSpec((1, 1, D_EXPERT, D_MODEL), lambda i, be, nu: (layer, be[i], 0, 0))
    return pl.pallas_call(
        _expert_kernel,
        grid_spec=pltpu.PrefetchScalarGridSpec(
            num_scalar_prefetch=2,
            grid=(n_blocks,),
            in_specs=[row_in, w_in, w_in, w_out],
            out_specs=pl.BlockSpec((MOE_ROWS, PACKED), lambda i, be, nu: (i, 0)),
        ),
        out_shape=jax.ShapeDtypeStruct(xs.shape, jnp.uint32),
        compiler_params=_cparams(("arbitrary",)),
        name="moe_experts",
    )(blk_exp, n_used, xs, w_gate, w_up, w_down)


def _combine_ln_kernel(ys_hbm, dest_hbm, mf_ref, x_ref, mod_ref, g_ref, b_ref, o_ref,
                       idx_smem, y_buf, sem_idx, sem_row):
    i = pl.program_id(0)
    slot = i % 2
    per_tile = 2 * TM

    def row_copy(sl, g, j, k, src_row):
        return pltpu.make_async_copy(ys_hbm.at[pl.ds(src_row, 1)], y_buf.at[sl, k, g, pl.ds(j, 1)], sem_row.at[sl])

    def idx_copy(tile, sl):
        return pltpu.make_async_copy(dest_hbm.at[pl.ds(tile * per_tile, per_tile)],
                                     idx_smem.at[pl.ds(sl * per_tile, per_tile)], sem_idx)

    def request(sl):
        _each_row(lambda g, j, k: row_copy(
            sl, g, j, k, idx_smem[sl * per_tile + g * (2 * ROW_UNROLL) + (2 * j + k)]).start())

    n = pl.num_programs(0)

    @pl.when(i == 0)
    def _():
        idx_copy(0, 0).start()
        idx_copy(0, 0).wait()
        request(0)
        pl.when(n > 1)(lambda: idx_copy(1, 1).start())

    @pl.when(i + 1 < n)
    def _():
        idx_copy(i + 1, 1 - slot).wait()
        request(1 - slot)
        pl.when(i + 2 < n)(lambda: idx_copy(i + 2, slot).start())

    _each_row(lambda g, j, k: row_copy(slot, g, j, k, 0).wait())

    mf = mf_ref[...]
    y0 = _unpack_rows(y_buf[slot, 0].reshape(TM, PACKED))
    y1 = _unpack_rows(y_buf[slot, 1].reshape(TM, PACKED))
    y = mf[:, 0:1] * y0 + mf[:, 1:2] * y1
    r = DEEPNORM_ALPHA * x_ref[...] + mod_ref[0, 5:6, :] * y
    o_ref[...] = _layer_norm(r, g_ref[...], b_ref[...])


def _combine_ln(ys, dest, mf, x1, mod_l, ln_g, ln_b, latent_only):
    mr = _mod_row(latent_only)
    n_tiles = _n_tiles(latent_only)
    return pl.pallas_call(
        _combine_ln_kernel,
        grid=(n_tiles,),
        in_specs=[
            pl.BlockSpec(memory_space=pl.ANY),
            pl.BlockSpec(memory_space=pl.ANY),
            pl.BlockSpec((TM, LANES), lambda i: (i, 0)),
            pl.BlockSpec((TM, D_MODEL), lambda i: (i, 0)),
            pl.BlockSpec((1, 6, D_MODEL), lambda i: (mr(i), 0, 0)),
            pl.BlockSpec((1, D_MODEL), lambda i: (0, 0)),
            pl.BlockSpec((1, D_MODEL), lambda i: (0, 0)),
        ],
        out_specs=pl.BlockSpec((TM, D_MODEL), lambda i: (i, 0)),
        out_shape=jax.ShapeDtypeStruct((n_tiles * TM, D_MODEL), F32),
        scratch_shapes=[
            pltpu.SMEM((2 * 2 * TM,), jnp.int32),
            pltpu.VMEM((2, 2, TM // ROW_UNROLL, ROW_UNROLL, PACKED), jnp.uint32),
            pltpu.SemaphoreType.DMA(()),
            pltpu.SemaphoreType.DMA((2,)),
        ],
        compiler_params=_cparams(("arbitrary",)),
        name="combine_ln",
    )(ys, dest, mf, x1, mod_l, ln_g, ln_b)


def _route_params(w_grp, b_grp, w_rexp, b_rexp):
    pad = LANES - N_GROUPS - N_EXPERTS
    w_route = jnp.concatenate([w_grp, w_rexp, jnp.zeros((D_MODEL, pad), F32)], axis=1)
    b_route = jnp.concatenate([b_grp, b_rexp, jnp.zeros((pad,), F32)])[None, :]
    return w_route, b_route


def _moe(x1, mi, mf, cnt, mod_l, layer, w_gate, w_up, w_down, ln_g, ln_b, latent_only):
    counts = cnt[0, :N_EXPERTS].astype(jnp.int32)
    padded = (counts + MOE_ROWS - 1) // MOE_ROWS * MOE_ROWS
    pad_end = jnp.cumsum(padded)
    pad_start = pad_end - padded
    n_tok = x1.shape[0]
    n_blocks = (2 * n_tok) // MOE_ROWS + N_EXPERTS
    experts = jnp.arange(N_EXPERTS, dtype=jnp.int32)
    start_of = jnp.sum(jnp.where(mi[:, 0:2, None] == experts, pad_start, 0), axis=-1)
    dest = (start_of + mi[:, 2:4]).reshape(-1)
    blk_start = jnp.arange(n_blocks, dtype=jnp.int32) * MOE_ROWS
    blk_exp = jnp.minimum(jnp.sum((pad_end[None, :] <= blk_start[:, None]).astype(jnp.int32), axis=1),
                          N_EXPERTS - 1)
    n_used = pad_end[-1:] // MOE_ROWS
    xs = _dispatch(pad_end, padded, x1, mod_l, dest, n_blocks, latent_only)
    ys = _experts(blk_exp, n_used, xs, layer, w_gate, w_up, w_down)
    return _combine_ln(ys, dest, mf, x1, mod_l, ln_g, ln_b, latent_only)


def kernel(x, c, ctx, c_ctx, ada_w, ada_b, ln1_g, ln1_b, ln2_g, ln2_b, ab_w_in, ab_w_lr_f, ab_b_lr_f, ab_w_lr_b, ab_b_lr_b, ab_gn_a, ab_gn_b, ab_w_out, c_w_qkv, c_lq1, c_lk1, c_lq2, c_lk2, c_subln_g, c_w_out, moe_w_grp, moe_b_grp, moe_w_rexp, moe_b_rexp, moe_w_gate, moe_w_up, moe_w_down):
    assert x.shape == (BATCH, SEQ, D_MODEL) and ctx.shape == (BATCH, CTX_LEN, D_MODEL)
    xs = jnp.concatenate([ctx, x], axis=1).reshape(N_ALL, D_MODEL)
    c_all = jnp.concatenate([c, c_ctx[None, :], jnp.zeros((16 - BATCH - 1, D_MODEL), F32)], axis=0)
    mod = _ada_tables(c_all, ada_w, ada_b).reshape(DEPTH, 16, 6, D_MODEL)

    rope_a = _rope_tables(DK_A, 1)
    rope_c = _rope_tables(DH_C, 2)
    dec_a = _retention_tables()

    for l in range(DEPTH):
        last = l == DEPTH - 1
        i = l // 2
        mod_l = mod[l]
        row = lambda v: v[None, :]
        route = _route_params(moe_w_grp[l], moe_b_grp[l], moe_w_rexp[l], moe_b_rexp[l])
        if l % 2 == 0:
            assert not last
            w_in = jnp.pad(ab_w_in[i], ((0, 0), (0, AB_COLS - ab_w_in.shape[2]))).astype(BF16)
            z = _mod_matmul(xs, mod_l, w_in)
            ya = _scan_a(z, *rope_a, dec_a, row(ab_gn_a[i]))
            wf = ab_w_lr_f[i].reshape(GLA_RANK, H_B // 2, LANES)
            wb = ab_w_lr_b[i].reshape(GLA_RANK, H_B // 2, LANES)
            wlr = jnp.zeros((H_B // 2, LANES, 2 * LANES), F32)
            wlr = wlr.at[:, 0:GLA_RANK, 0:LANES].set(jnp.swapaxes(wf, 0, 1))
            wlr = wlr.at[:, GLA_RANK:2 * GLA_RANK, LANES:].set(jnp.swapaxes(wb, 0, 1))
            blr = jnp.concatenate([ab_b_lr_f[i].reshape(H_B // 2, 1, LANES),
                                   ab_b_lr_b[i].reshape(H_B // 2, 1, LANES)], axis=-1)
            yb = _scan_b(z, wlr, blr, row(ab_gn_b[i]))
            x1, mi, mf, cnt = _proj_ln(ya, yb, 0, ab_w_out[i].astype(BF16), xs, mod_l, row(ln1_g[l]),
                                       row(ln1_b[l]), *route, False)
        else:
            lam_init = 0.8 - 0.6 * math.exp(-0.3 * l)
            lam = (jnp.exp(jnp.sum(c_lq1[i] * c_lk1[i], axis=-1))
                   - jnp.exp(jnp.sum(c_lq2[i] * c_lk2[i], axis=-1))).astype(F32) + lam_init
            z = _mod_matmul(xs, mod_l, c_w_qkv[i].astype(BF16))
            gsub = row(c_subln_g[i])
            y = _attention(z, lam, *rope_c, gsub, 1.0 - lam_init, last)
            x1, mi, mf, cnt = _proj_ln(y, y, 1, c_w_out[i].astype(BF16), xs, mod_l, row(ln1_g[l]),
                                       row(ln1_b[l]), *route, last)
        xs = _moe(x1, mi, mf, cnt, mod_l, l, moe_w_gate, moe_w_up, moe_w_down, row(ln2_g[l]), row(ln2_b[l]), last)
    return xs.reshape(BATCH, SEQ, D_MODEL)
```

```python
import functools
import math

import numpy as np
import jax
import jax.numpy as jnp
from jax import lax
from jax.experimental import pallas as pl
from jax.experimental.pallas import tpu as pltpu

F32 = jnp.float32
BF16 = jnp.bfloat16

D_MODEL = 1024
BATCH = 8
SEQ = 2048
DEPTH = 4
GRID_W = 64
CTX_LEN = 256
ROPE_BASE = 10000.0
LN_EPS = 1e-5
DEEPNORM_ALPHA = (2 * DEPTH) ** 0.25
H_A = 4
DK_A = 128
DV_A = 128
CHUNK_A = 128
RET_EXP_FWD = 5.0
RET_EXP_BWD = 5.5
H_B = 4
DK_B = 64
DV_B = 128
GLA_RANK = 16
GLA_TAU = 16.0
CHUNK_B = 64
H_C = 8
DH_C = 64
DV_C = 128
N_GROUPS = 4
EXPERTS_PER_GROUP = 8
N_EXPERTS = 32
D_EXPERT = 512

LANES = 128
T_ALL = CTX_LEN + SEQ
N_ALL = BATCH * T_ALL
TM = 256
TILES_PER_BATCH = T_ALL // TM
LATENT_TILES_PER_BATCH = SEQ // TM
AB_COLS = 29 * LANES
MOE_ROWS = 512
PACKED = D_MODEL // 2
ROW_UNROLL = 8
SCAN_UNROLL = 8
GLA_GROUP = 256
TQ = 256
VMEM_LIMIT = 56 * 1024 * 1024


def _cparams(sem):
    return pltpu.CompilerParams(dimension_semantics=sem, vmem_limit_bytes=VMEM_LIMIT)


def _silu(v):
    return v * (1.0 / (1.0 + jnp.exp(-v)))


def _n_tiles(latent_only):
    return BATCH * (LATENT_TILES_PER_BATCH if latent_only else TILES_PER_BATCH)


def _row_tile(latent_only):
    if latent_only:
        return lambda i: (i // LATENT_TILES_PER_BATCH) * TILES_PER_BATCH + 1 + i % LATENT_TILES_PER_BATCH
    return lambda i: i


def _mod_row(latent_only):
    if latent_only:
        return lambda i: i // LATENT_TILES_PER_BATCH
    return lambda i: jnp.where(i % TILES_PER_BATCH == 0, BATCH, i // TILES_PER_BATCH)


def _ada_kernel(c_ref, w_ref, b_ref, o_ref):
    sc = _silu(c_ref[...])
    o_ref[0] = jnp.dot(sc.astype(BF16), w_ref[0].astype(BF16), preferred_element_type=F32) + b_ref[0]


def _ada_tables(c_all, ada_w, ada_b):
    tn = 1536
    n_out = 6 * D_MODEL
    return pl.pallas_call(
        _ada_kernel,
        grid=(DEPTH, n_out // tn),
        in_specs=[
            pl.BlockSpec((16, D_MODEL), lambda l, j: (0, 0)),
            pl.BlockSpec((1, D_MODEL, tn), lambda l, j: (l, 0, j)),
            pl.BlockSpec((1, 1, tn), lambda l, j: (l, 0, j)),
        ],
        out_specs=pl.BlockSpec((1, 16, tn), lambda l, j: (l, 0, j)),
        out_shape=jax.ShapeDtypeStruct((DEPTH, 16, n_out), F32),
        compiler_params=_cparams(("arbitrary", "arbitrary")),
        name="ada_tables",
    )(c_all, ada_w, ada_b.reshape(DEPTH, 1, n_out))


def _modmm_kernel(x_ref, mod_ref, w_ref, o_ref):
    u = x_ref[...] * (1.0 + mod_ref[0, 1:2, :]) + mod_ref[0, 0:1, :]
    o_ref[...] = jnp.dot(u.astype(BF16), w_ref[...], preferred_element_type=F32)


def _mod_matmul(x, mod_l, w_bf16):
    n_out = w_bf16.shape[1]
    return pl.pallas_call(
        _modmm_kernel,
        grid=(N_ALL // TM,),
        in_specs=[
            pl.BlockSpec((TM, D_MODEL), lambda i: (i, 0)),
            pl.BlockSpec((1, 6, D_MODEL), lambda i: (_mod_row(False)(i), 0, 0)),
            pl.BlockSpec((D_MODEL, n_out), lambda i: (0, 0)),
        ],
        out_specs=pl.BlockSpec((TM, n_out), lambda i: (i, 0)),
        out_shape=jax.ShapeDtypeStruct((N_ALL, n_out), F32),
        compiler_params=_cparams(("arbitrary",)),
        name="mod_matmul",
    )(x, mod_l, w_bf16)


def _rope_tables(head_dim, reps):
    rows = SEQ // GRID_W
    row = np.repeat(np.arange(rows, dtype=np.float32), GRID_W)
    col = np.tile(np.arange(GRID_W, dtype=np.float32), rows)
    quarter = head_dim // 4
    inv = (ROPE_BASE ** (-np.arange(quarter, dtype=np.float32) / quarter)).astype(np.float32)
    ang_r = row[:, None] * inv
    ang_c = col[:, None] * inv
    ang = np.concatenate([ang_r, ang_r, ang_c, ang_c], axis=-1)
    cos = np.cos(ang).astype(np.float32)
    sin = np.sin(ang).astype(np.float32)
    q_idx = (np.arange(head_dim) // quarter) % 2
    sin_up = np.where(q_idx == 1, sin, 0.0).astype(np.float32)
    sin_dn = np.where(q_idx == 0, -sin, 0.0).astype(np.float32)

    def full(t, ctx_val):
        t = np.tile(t, (1, reps))
        return jnp.asarray(np.concatenate([np.full((CTX_LEN, t.shape[1]), ctx_val, np.float32), t], axis=0))

    return full(cos, 1.0), full(sin_up, 0.0), full(sin_dn, 0.0)


def _rope(x, cos, sin_up, sin_dn, quarter):
    width = x.shape[-1]
    return x * cos + pltpu.roll(x, quarter, 1) * sin_up + pltpu.roll(x, width - quarter, 1) * sin_dn


def _dot_tb(a, b):
    return lax.dot_general(a, b, (((1,), (1,)), ((), ())), preferred_element_type=F32)


def _dot_ta(a, b):
    return lax.dot_general(a, b, (((0,), (0,)), ((), ())), preferred_element_type=F32)


def _split_bf16(x, parts):
    out = []
    for _ in range(parts):
        t = x.astype(BF16)
        out.append(t)
        x = x - t.astype(F32)
    return out


def _dot_split(a, b):
    a_hi, a_lo = _split_bf16(a, 2)
    b_hi, b_lo = _split_bf16(b, 2)
    dot = lambda u, v: jnp.dot(u, v, preferred_element_type=F32)
    return dot(a_hi, b_hi) + (dot(a_hi, b_lo) + dot(a_lo, b_hi))


def _dot_mask(mask_bf16, x):
    return sum(jnp.dot(mask_bf16, t, preferred_element_type=F32) for t in reversed(_split_bf16(x, 3)))


def _retention_tables():
    c = CHUNK_A
    i = np.arange(c, dtype=np.float64)
    out = np.zeros((H_A, 7, c, LANES), np.float64)
    for h in range(H_A):
        lgf = np.log1p(-np.exp2(-(RET_EXP_FWD + h)))
        lgb = np.log1p(-np.exp2(-(RET_EXP_BWD + h)))
        d = i[:, None] - i[None, :]
        out[h, 0] = np.where(d >= 0, np.exp(lgf * d), np.exp(lgb * (-d - 1)))
        out[h, 1] = np.exp(lgf * (i + 1))[:, None]
        out[h, 2] = np.exp(lgb * (c - 1 - i))[:, None]
        out[h, 3] = np.exp(lgf * (c - 1 - i))[:, None]
        out[h, 4] = np.exp(lgb * i)[:, None]
        out[h, 5] = np.exp(lgf * c)
        out[h, 6] = np.exp(lgb * c)
    return jnp.asarray(out.astype(np.float32))


def _scan_a_kernel(q_ref, k_ref, v_ref, g_ref, cos_ref, sup_ref, sdn_ref, dec_ref, gn_ref, o_ref,
                   kr_scr, sb_scr):
    c = CHUNK_A
    n_ctx = CTX_LEN // c
    n_all = T_ALL // c
    scale = DK_A ** -0.5
    kr_scr[...] = _rope(k_ref[...], cos_ref[...], sup_ref[...], sdn_ref[...], DK_A // 4)
    dmat = dec_ref[0, 0]
    q_f, q_b, k_f, k_b = dec_ref[0, 1], dec_ref[0, 2], dec_ref[0, 3], dec_ref[0, 4]
    g_fc, g_bc = dec_ref[0, 5], dec_ref[0, 6]
    gn = gn_ref[...]
    zero = jnp.zeros((DK_A, DV_A), F32)

    def chunk(ci):
        return pl.ds(pl.multiple_of(ci * c, c), c)

    def kv_state(ci, k_dec):
        sl = chunk(ci)
        return _dot_ta((kr_scr[sl, :] * k_dec).astype(BF16), v_ref[sl, :].astype(BF16))

    def run(lo, hi, sf0, sb0):
        def bwd(j, sb):
            ci = hi - 1 - j
            sb_scr[ci] = sb
            return g_bc * sb + kv_state(ci, k_b)

        sb_fin = lax.fori_loop(0, hi - lo, bwd, sb0, unroll=SCAN_UNROLL)

        def fwd(j, sf):
            ci = lo + j
            sl = chunk(ci)
            q = _rope(q_ref[sl, :], cos_ref[sl, :], sup_ref[sl, :], sdn_ref[sl, :], DK_A // 4) * scale
            k = kr_scr[sl, :]
            vb = v_ref[sl, :].astype(BF16)
            att = _dot_tb(q.astype(BF16), k.astype(BF16)) * dmat
            o = jnp.dot(att.astype(BF16), vb, preferred_element_type=F32)
            o = o + jnp.dot((q * q_f).astype(BF16), sf.astype(BF16), preferred_element_type=F32)
            o = o + jnp.dot((q * q_b).astype(BF16), sb_scr[ci].astype(BF16), preferred_element_type=F32)
            o = o - jnp.mean(o, axis=-1, keepdims=True)
            o = o * lax.rsqrt(jnp.mean(o * o, axis=-1, keepdims=True) + LN_EPS)
            o_ref[sl, :] = _silu(g_ref[sl, :]) * (o * gn)
            return g_fc * sf + _dot_ta((k * k_f).astype(BF16), vb)

        sf_fin = lax.fori_loop(0, hi - lo, fwd, sf0, unroll=SCAN_UNROLL)
        return sf_fin, sb_fin

    sf_c, sb_c = run(0, n_ctx, zero, zero)
    run(n_ctx, n_all, sf_c, sb_c)


def _scan_a(z, cos, sup, sdn, dec, gn_a):
    blk = lambda col0: pl.BlockSpec((T_ALL, LANES), lambda b, h: (b, col0 + h))
    tbl = pl.BlockSpec((T_ALL, LANES), lambda b, h: (0, 0))
    return pl.pallas_call(
        _scan_a_kernel,
        grid=(BATCH, H_A),
        in_specs=[blk(0), blk(4), blk(8), blk(12), tbl, tbl, tbl,
                  pl.BlockSpec((1, 7, CHUNK_A, LANES), lambda b, h: (h, 0, 0, 0)),
                  pl.BlockSpec((1, LANES), lambda b, h: (0, h))],
        out_specs=pl.BlockSpec((T_ALL, LANES), lambda b, h: (b, h)),
        out_shape=jax.ShapeDtypeStruct((N_ALL, H_A * DV_A), F32),
        scratch_shapes=[pltpu.VMEM((T_ALL, LANES), F32),
                        pltpu.VMEM((T_ALL // CHUNK_A, DK_A, DV_A), F32)],
        compiler_params=_cparams(("arbitrary", "arbitrary")),
        name="scan_retention",
    )(z, z, z, z, cos, sup, sdn, dec, gn_a)


def _log_sigmoid(g):
    return jnp.minimum(g, 0.0) - jnp.log1p(jnp.exp(-jnp.abs(g)))


def _scan_b_kernel(q_ref, k_ref, v_ref, g_ref, lr_ref, wlr_ref, blr_ref, gn_ref, o_ref,
                   qf_scr, kf_scr, qb_scr, kb_scr, ktf_scr, ktb_scr, ef_scr, eb_scr, sb_scr):
    c = CHUNK_B
    n_ctx = CTX_LEN // c
    n_all = T_ALL // c
    per_group = GLA_GROUP // c
    scale = DK_B ** -0.5

    gi_r = lax.broadcasted_iota(jnp.int32, (GLA_GROUP, GLA_GROUP), 0)
    gi_c = lax.broadcasted_iota(jnp.int32, (GLA_GROUP, GLA_GROUP), 1)
    same_chunk = (gi_r // c) == (gi_c // c)
    prefix = (same_chunk & (gi_c <= gi_r)).astype(BF16)
    suffix = (same_chunk & (gi_c >= gi_r)).astype(BF16)

    def prepare(gi, carry):
        sl = pl.ds(pl.multiple_of(gi * GLA_GROUP, GLA_GROUP), GLA_GROUP)
        gates = _dot_split(lr_ref[sl, :], wlr_ref[0]) + blr_ref[0]
        laf = _log_sigmoid(gates[:, :LANES]) * (1.0 / GLA_TAU)
        lab = _log_sigmoid(gates[:, LANES:]) * (1.0 / GLA_TAU)
        b = _dot_mask(prefix, laf)
        rb = _dot_mask(suffix, lab)
        q = q_ref[sl, :] * scale
        k = k_ref[sl, :]
        qf_scr[sl, :] = (q * jnp.exp(b)).astype(BF16)
        kf_scr[sl, :] = (k * jnp.exp(-b)).astype(BF16)
        qb_scr[sl, :] = (q * jnp.exp(rb - lab)).astype(BF16)
        kb_scr[sl, :] = (k * jnp.exp(-rb)).astype(BF16)
        b3 = b.reshape(per_group, c, LANES)
        rb3 = rb.reshape(per_group, c, LANES)
        k3 = k.reshape(per_group, c, LANES)
        b_tot = b3[:, c - 1:c, :]
        rb_tot = rb3[:, 0:1, :]
        ktf_scr[sl, :] = (k3 * jnp.exp(b_tot - b3)).reshape(GLA_GROUP, LANES).astype(BF16)
        ktb_scr[sl, :] = (k3 * jnp.exp(rb_tot - rb3)).reshape(GLA_GROUP, LANES).astype(BF16)
        for m in range(per_group):
            ef_scr[gi * per_group + m] = jnp.broadcast_to(jnp.exp(b_tot[m]), (8, LANES))
            eb_scr[gi * per_group + m] = jnp.broadcast_to(jnp.exp(rb_tot[m]), (8, LANES))
        return carry

    lax.fori_loop(0, T_ALL // GLA_GROUP, prepare, 0)

    lane = lax.broadcasted_iota(jnp.int32, (1, LANES), 1)
    masks = [lane < DK_B, lane >= DK_B]
    ri = lax.broadcasted_iota(jnp.int32, (c, c), 0)
    cj = lax.broadcasted_iota(jnp.int32, (c, c), 1)
    lower = cj <= ri
    gn = gn_ref[...]
    zero = jnp.zeros((DV_B, LANES), F32)
    zero_b = jnp.zeros((), BF16)

    def chunk(ci):
        return pl.ds(pl.multiple_of(ci * c, c), c)

    def run(lo, hi, sf0, sb0):
        def bwd(j, sb):
            ci = hi - 1 - j
            sl = chunk(ci)
            e_tot = eb_scr[ci][0:1, :]
            kt = ktb_scr[sl, :]
            v = v_ref[sl, :]
            new = []
            for h in range(2):
                sb_scr[ci, h] = sb[h]
                vh = v[:, h * DV_B:(h + 1) * DV_B].astype(BF16)
                new.append(sb[h] * e_tot + _dot_ta(vh, jnp.where(masks[h], kt, zero_b)))
            return tuple(new)

        sb_fin = lax.fori_loop(0, hi - lo, bwd, sb0, unroll=SCAN_UNROLL)

        def fwd(j, sf):
            ci = lo + j
            sl = chunk(ci)
            e_tot = ef_scr[ci][0:1, :]
            qf, kf, qb, kb, kt = qf_scr[sl, :], kf_scr[sl, :], qb_scr[sl, :], kb_scr[sl, :], ktf_scr[sl, :]
            v = v_ref[sl, :]
            g = g_ref[sl, :]
            new = []
            for h in range(2):
                pick = lambda t: jnp.where(masks[h], t, zero_b)
                vh = v[:, h * DV_B:(h + 1) * DV_B].astype(BF16)
                qfh, qbh = pick(qf), pick(qb)
                att = jnp.where(lower, _dot_tb(qfh, pick(kf)), _dot_tb(qbh, pick(kb)))
                o = jnp.dot(att.astype(BF16), vh, preferred_element_type=F32)
                o = o + _dot_tb(qfh, sf[h].astype(BF16))
                o = o + _dot_tb(qbh, sb_scr[ci, h].astype(BF16))
                o = o * lax.rsqrt(jnp.mean(o * o, axis=-1, keepdims=True) + LN_EPS)
                cols = slice(h * DV_B, (h + 1) * DV_B)
                o_ref[sl, cols] = _silu(g[:, cols]) * (o * gn[:, cols])
                new.append(sf[h] * e_tot + _dot_ta(vh, pick(kt)))
            return tuple(new)

        sf_fin = lax.fori_loop(0, hi - lo, fwd, sf0, unroll=SCAN_UNROLL)
        return sf_fin, sb_fin

    sf_c, sb_c = run(0, n_ctx, (zero, zero), (zero, zero))
    run(n_ctx, n_all, sf_c, sb_c)


def _scan_b(z, wlr, blr, gn_b):
    pairs = H_B // 2
    return pl.pallas_call(
        _scan_b_kernel,
        grid=(BATCH, pairs),
        in_specs=[
            pl.BlockSpec((T_ALL, LANES), lambda b, p: (b, 16 + p)),
            pl.BlockSpec((T_ALL, LANES), lambda b, p: (b, 18 + p)),
            pl.BlockSpec((T_ALL, 2 * DV_B), lambda b, p: (b, 10 + p)),
            pl.BlockSpec((T_ALL, 2 * DV_B), lambda b, p: (b, 12 + p)),
            pl.BlockSpec((T_ALL, LANES), lambda b, p: (b, 28)),
            pl.BlockSpec((1, LANES, 2 * LANES), lambda b, p: (p, 0, 0)),
            pl.BlockSpec((1, 1, 2 * LANES), lambda b, p: (p, 0, 0)),
            pl.BlockSpec((1, 2 * DV_B), lambda b, p: (0, p)),
        ],
        out_specs=pl.BlockSpec((T_ALL, 2 * DV_B), lambda b, p: (b, p)),
        out_shape=jax.ShapeDtypeStruct((N_ALL, H_B * DV_B), F32),
        scratch_shapes=[pltpu.VMEM((T_ALL, LANES), BF16)] * 6
                       + [pltpu.VMEM((T_ALL // CHUNK_B, 8, LANES), F32)] * 2
                       + [pltpu.VMEM((T_ALL // CHUNK_B, 2, DV_B, LANES), F32)],
        compiler_params=_cparams(("arbitrary", "arbitrary")),
        name="scan_gla",
    )(z, z, z, z, z, wlr, blr, gn_b)


def _attn_kernel(lam_ref, q_ref, k_ref, v_ref, qcos_ref, qsup_ref, qsdn_ref, kcos_ref, ksup_ref, ksdn_ref,
                 gsub_ref, o_ref, k_scr, v_scr, *, post_scale, tile0):
    h = pl.program_id(1)
    t = pl.program_id(2)
    quarter = DH_C // 4
    scale = DH_C ** -0.5

    @pl.when(t == 0)
    def _():
        k_scr[...] = _rope(k_ref[...], kcos_ref[...], ksup_ref[...], ksdn_ref[...], quarter).astype(BF16)
        v_scr[...] = v_ref[...].astype(BF16)

    lam = lam_ref[h]
    lane = lax.broadcasted_iota(jnp.int32, (1, LANES), 1)
    m1 = (lane < DH_C).astype(F32)
    m2 = (lane >= DH_C).astype(F32)

    def attend(n_keys):
        q = _rope(q_ref[...], qcos_ref[...], qsup_ref[...], qsdn_ref[...], quarter) * scale
        kb = k_scr[0:n_keys, :]
        s1 = _dot_tb((q * m1).astype(BF16), kb)
        s2 = _dot_tb((q * m2).astype(BF16), kb)
        e1 = jnp.exp(s1 - jnp.max(s1, axis=-1, keepdims=True))
        e2 = jnp.exp(s2 - jnp.max(s2, axis=-1, keepdims=True))
        r1 = 1.0 / jnp.sum(e1, axis=-1, keepdims=True)
        r2 = lam / jnp.sum(e2, axis=-1, keepdims=True)
        a = e1 * r1 - e2 * r2
        o = jnp.dot(a.astype(BF16), v_scr[0:n_keys, :], preferred_element_type=F32)
        o = o * lax.rsqrt(jnp.mean(o * o, axis=-1, keepdims=True) + LN_EPS)
        o_ref[...] = o * (gsub_ref[...] * post_scale)

    if tile0 == 0:
        pl.when(t == 0)(lambda: attend(CTX_LEN))
        pl.when(t > 0)(lambda: attend(T_ALL))
    else:
        attend(T_ALL)


def _attention(z, lam, cos, sup, sdn, gsub, post_scale, latent_only):
    tile0 = 1 if latent_only else 0
    n_qt = TILES_PER_BATCH - tile0
    kern = functools.partial(_attn_kernel, post_scale=post_scale, tile0=tile0)
    kv = lambda col0: pl.BlockSpec((T_ALL, LANES), lambda b, h, t, lam_r: (b, col0 + h))
    q_tbl = pl.BlockSpec((TQ, LANES), lambda b, h, t, lam_r: (tile0 + t, 0))
    k_tbl = pl.BlockSpec((T_ALL, LANES), lambda b, h, t, lam_r: (0, 0))
    return pl.pallas_call(
        kern,
        grid_spec=pltpu.PrefetchScalarGridSpec(
            num_scalar_prefetch=1,
            grid=(BATCH, H_C, n_qt),
            in_specs=[pl.BlockSpec((TQ, LANES), lambda b, h, t, lam_r: (b * TILES_PER_BATCH + tile0 + t, h)),
                      kv(H_C), kv(2 * H_C), q_tbl, q_tbl, q_tbl, k_tbl, k_tbl, k_tbl,
                      pl.BlockSpec((1, LANES), lambda b, h, t, lam_r: (0, h))],
            out_specs=pl.BlockSpec((TQ, LANES), lambda b, h, t, lam_r: (b * n_qt + t, h)),
            scratch_shapes=[pltpu.VMEM((T_ALL, LANES), BF16), pltpu.VMEM((T_ALL, LANES), BF16)],
        ),
        out_shape=jax.ShapeDtypeStruct((BATCH * n_qt * TQ, H_C * DV_C), F32),
        compiler_params=_cparams(("arbitrary", "arbitrary", "arbitrary")),
        name="diff_attention",
    )(lam, z, z, z, cos, sup, sdn, cos, sup, sdn, gsub)


def _layer_norm(r, g, b):
    mu = jnp.mean(r, axis=-1, keepdims=True)
    d = r - mu
    var = jnp.mean(d * d, axis=-1, keepdims=True)
    return d * lax.rsqrt(var + LN_EPS) * g + b


def _route_tile(u, w_ref, b_ref, mi_ref, mf_ref, cnt_ref, carry):
    @pl.when(pl.program_id(0) == 0)
    def _():
        carry[...] = jnp.zeros_like(carry)

    logits = _dot_split(u, w_ref[...]) + b_ref[...]
    lane = lax.broadcasted_iota(jnp.int32, (TM, LANES), 1)
    lane_f = lane.astype(F32)
    neg = -jnp.inf
    big = 1e9

    gmask = lane < N_GROUPS
    gl = jnp.where(gmask, logits, neg)
    gmax = jnp.max(gl, axis=-1, keepdims=True)
    gidx = jnp.min(jnp.where(gl == gmax, lane_f, big), axis=-1, keepdims=True)
    gw = 1.0 / jnp.sum(jnp.where(gmask, jnp.exp(logits - gmax), 0.0), axis=-1, keepdims=True)

    e_lane = lane - N_GROUPS
    in_grp = (e_lane >= 0) & (e_lane < N_EXPERTS) & ((e_lane >> 3) == gidx.astype(jnp.int32))
    el = jnp.where(in_grp, logits, neg)
    v1 = jnp.max(el, axis=-1, keepdims=True)
    i1 = jnp.min(jnp.where(el == v1, lane_f, big), axis=-1, keepdims=True)
    el2 = jnp.where(lane_f == i1, neg, el)
    v2 = jnp.max(el2, axis=-1, keepdims=True)
    i2 = jnp.min(jnp.where(el2 == v2, lane_f, big), axis=-1, keepdims=True)
    t = jnp.exp(v2 - v1)
    c0 = gw / (1.0 + t)
    c1 = gw * t / (1.0 + t)
    e0 = i1 - N_GROUPS
    e1 = i2 - N_GROUPS

    oh0 = lane_f == e0
    oh1 = lane_f == e1
    cnt = oh0.astype(F32) + oh1.astype(F32)
    ri = lax.broadcasted_iota(jnp.int32, (TM, TM), 0)
    cj = lax.broadcasted_iota(jnp.int32, (TM, TM), 1)
    strict = (cj < ri).astype(BF16)
    before = jnp.dot(strict, cnt.astype(BF16), preferred_element_type=F32) + carry[0:1, :]
    r0 = jnp.sum(jnp.where(oh0, before, 0.0), axis=-1, keepdims=True)
    r1 = jnp.sum(jnp.where(oh1, before, 0.0), axis=-1, keepdims=True)
    carry[0:1, :] = carry[0:1, :] + jnp.sum(cnt, axis=0, keepdims=True)

    mi = jnp.where(lane == 0, e0, jnp.where(lane == 1, e1, jnp.where(lane == 2, r0, jnp.where(lane == 3, r1, 0.0))))
    mi_ref[...] = mi.astype(jnp.int32)
    mf_ref[...] = jnp.where(lane == 0, c0, jnp.where(lane == 1, c1, 0.0))
    cnt_ref[...] = carry[...]


def _proj_ln_kernel(y1_ref, y2_ref, w1_ref, w2_ref, x_ref, mod_ref, g_ref, b_ref, wr_ref, br_ref,
                    o_ref, mi_ref, mf_ref, cnt_ref, carry):
    y = jnp.dot(y1_ref[...].astype(BF16), w1_ref[...], preferred_element_type=F32)
    y = y + jnp.dot(y2_ref[...].astype(BF16), w2_ref[...], preferred_element_type=F32)
    r = DEEPNORM_ALPHA * x_ref[...] + mod_ref[0, 2:3, :] * y
    x1 = _layer_norm(r, g_ref[...], b_ref[...])
    o_ref[...] = x1
    u = x1 * (1.0 + mod_ref[0, 4:5, :]) + mod_ref[0, 3:4, :]
    _route_tile(u, wr_ref, br_ref, mi_ref, mf_ref, cnt_ref, carry)


def _proj_ln(y1, y2, col2, w_out_bf16, x, mod_l, ln_g, ln_b, w_route, b_route, latent_only):
    half = D_MODEL // 2
    rt = _row_tile(latent_only)
    mr = _mod_row(latent_only)
    n_tiles = _n_tiles(latent_only)
    n_tok = n_tiles * TM
    row_blk = lambda w: pl.BlockSpec((TM, w), lambda i: (i, 0))
    const = lambda shape: pl.BlockSpec(shape, lambda i: (0, 0))
    return pl.pallas_call(
        _proj_ln_kernel,
        grid=(n_tiles,),
        in_specs=[
            pl.BlockSpec((TM, half), lambda i: (i, 0)),
            pl.BlockSpec((TM, half), lambda i: (i, col2)),
            pl.BlockSpec((half, D_MODEL), lambda i: (0, 0)),
            pl.BlockSpec((half, D_MODEL), lambda i: (1, 0)),
            pl.BlockSpec((TM, D_MODEL), lambda i: (rt(i), 0)),
            pl.BlockSpec((1, 6, D_MODEL), lambda i: (mr(i), 0, 0)),
            const((1, D_MODEL)), const((1, D_MODEL)), const((D_MODEL, LANES)), const((1, LANES)),
        ],
        out_specs=[row_blk(D_MODEL), row_blk(LANES), row_blk(LANES), const((8, LANES))],
        out_shape=[jax.ShapeDtypeStruct((n_tok, D_MODEL), F32),
                   jax.ShapeDtypeStruct((n_tok, LANES), jnp.int32),
                   jax.ShapeDtypeStruct((n_tok, LANES), F32),
                   jax.ShapeDtypeStruct((8, LANES), F32)],
        scratch_shapes=[pltpu.VMEM((8, LANES), F32)],
        compiler_params=_cparams(("arbitrary",)),
        name="proj_ln_route",
    )(y1, y2, w_out_bf16, w_out_bf16, x, mod_l, ln_g, ln_b, w_route, b_route)


def _pack_rows(x):
    half = x.shape[-1] // 2
    bits = lambda t: lax.bitcast_convert_type(t.astype(BF16).astype(F32), jnp.uint32)
    return (bits(x[:, :half]) >> 16) | (bits(x[:, half:]) & jnp.uint32(0xFFFF0000))


def _unpack_rows(w):
    lo = lax.bitcast_convert_type(w << 16, F32)
    hi = lax.bitcast_convert_type(w & jnp.uint32(0xFFFF0000), F32)
    return jnp.concatenate([lo, hi], axis=-1)


def _each_row(fn):
    def body(g, carry):
        for j in range(ROW_UNROLL):
            for k in range(2):
                fn(g, j, k)
        return carry
    lax.fori_loop(0, TM // ROW_UNROLL, body, 0)


def _each_row_inline(fn):
    for g in range(TM // ROW_UNROLL):
        for j in range(ROW_UNROLL):
            for k in range(2):
                fn(g, j, k)


def _dispatch_kernel(pad_end_ref, padded_ref, x_ref, mod_ref, dest_hbm, xs_hbm,
                     idx_smem, u_scr, zero_scr, sem_idx, sem_row, sem_zero):
    i = pl.program_id(0)
    slot = i % 2
    per_tile = 2 * TM

    @pl.when(i == 0)
    def _():
        zero_scr[...] = jnp.zeros_like(zero_scr)

        def zero_block(first_row):
            rows = pl.ds(pl.multiple_of(first_row, MOE_ROWS), MOE_ROWS)
            return pltpu.make_async_copy(zero_scr, xs_hbm.at[rows], sem_zero)

        n_rows = xs_hbm.shape[0]
        total = pad_end_ref[N_EXPERTS - 1]
        for e in range(N_EXPERTS):
            pl.when(padded_ref[e] > 0)(lambda e=e: zero_block(pad_end_ref[e] - MOE_ROWS).start())
            pl.when(total + e * MOE_ROWS < n_rows)(lambda e=e: zero_block(total + e * MOE_ROWS).start())
        for e in range(N_EXPERTS):
            pl.when(padded_ref[e] > 0)(lambda e=e: zero_block(0).wait())
            pl.when(total + e * MOE_ROWS < n_rows)(lambda e=e: zero_block(0).wait())

    def idx_copy(tile, sl):
        return pltpu.make_async_copy(dest_hbm.at[pl.ds(tile * per_tile, per_tile)],
                                     idx_smem.at[pl.ds(sl * per_tile, per_tile)], sem_idx)

    pl.when(i == 0)(lambda: idx_copy(0, 0).start())
    idx_copy(i, slot).wait()
    pl.when(i + 1 < pl.num_programs(0))(lambda: idx_copy(i + 1, 1 - slot).start())
    u = x_ref[...] * (1.0 + mod_ref[0, 4:5, :]) + mod_ref[0, 3:4, :]
    u_scr[slot] = _pack_rows(u).reshape(TM // ROW_UNROLL, ROW_UNROLL, PACKED)

    def row_copy(sl, g, j, dst_row):
        return pltpu.make_async_copy(u_scr.at[sl, g, pl.ds(j, 1)], xs_hbm.at[pl.ds(dst_row, 1)], sem_row.at[sl])

    def dest_of(sl, g, j, k):
        return idx_smem[sl * per_tile + g * (2 * ROW_UNROLL) + (2 * j + k)]

    _each_row(lambda g, j, k: row_copy(slot, g, j, dest_of(slot, g, j, k)).start())

    @pl.when(i > 0)
    def _():
        _each_row(lambda g, j, k: row_copy(1 - slot, g, j, 0).wait())

    @pl.when(i == pl.num_programs(0) - 1)
    def _():
        _each_row(lambda g, j, k: row_copy(slot, g, j, 0).wait())


def _dispatch(pad_end, padded, x1, mod_l, dest, n_blocks, latent_only):
    mr = _mod_row(latent_only)
    return pl.pallas_call(
        _dispatch_kernel,
        grid_spec=pltpu.PrefetchScalarGridSpec(
            num_scalar_prefetch=2,
            grid=(_n_tiles(latent_only),),
            in_specs=[
                pl.BlockSpec((TM, D_MODEL), lambda i, pe, pd: (i, 0)),
                pl.BlockSpec((1, 6, D_MODEL), lambda i, pe, pd: (mr(i), 0, 0)),
                pl.BlockSpec(memory_space=pl.ANY),
            ],
            out_specs=pl.BlockSpec(memory_space=pl.ANY),
            scratch_shapes=[
                pltpu.SMEM((2 * 2 * TM,), jnp.int32),
                pltpu.VMEM((2, TM // ROW_UNROLL, ROW_UNROLL, PACKED), jnp.uint32),
                pltpu.VMEM((MOE_ROWS, PACKED), jnp.uint32),
                pltpu.SemaphoreType.DMA(()),
                pltpu.SemaphoreType.DMA((2,)),
                pltpu.SemaphoreType.DMA(()),
            ],
        ),
        out_shape=jax.ShapeDtypeStruct((n_blocks * MOE_ROWS, PACKED), jnp.uint32),
        compiler_params=_cparams(("arbitrary",)),
        name="moe_dispatch",
    )(pad_end, padded, x1, mod_l, dest)


def _expert_kernel(blk_exp_ref, n_used_ref, x_ref, wg_ref, wu_ref, wd_ref, o_ref):
    used = pl.program_id(0) < n_used_ref[0]

    @pl.when(used)
    def _():
        x = _unpack_rows(x_ref[...]).astype(BF16)
        gate = jnp.dot(x, wg_ref[0, 0].astype(BF16), preferred_element_type=F32)
        up = jnp.dot(x, wu_ref[0, 0].astype(BF16), preferred_element_type=F32)
        hid = (_silu(gate) * up).astype(BF16)
        o_ref[...] = _pack_rows(jnp.dot(hid, wd_ref[0, 0].astype(BF16), preferred_element_type=F32))

    @pl.when(jnp.logical_not(used))
    def _():
        o_ref[...] = jnp.zeros_like(o_ref)


def _experts(blk_exp, n_used, xs, layer, w_gate, w_up, w_down):
    n_blocks = xs.shape[0] // MOE_ROWS
    row_in = pl.BlockSpec((MOE_ROWS, PACKED), lambda i, be, nu: (jnp.minimum(i, nu[0] - 1), 0))
    w_in = pl.BlockSpec((1, 1, D_MODEL, D_EXPERT), lambda i, be, nu: (layer, be[i], 0, 0))
    w_out = pl.BlockSpec((1, 1, D_EXPERT, D_MODEL), lambda i, be, nu: (layer, be[i], 0, 0))
    return pl.pallas_call(
        _expert_kernel,
        grid_spec=pltpu.PrefetchScalarGridSpec(
            num_scalar_prefetch=2,
            grid=(n_blocks,),
            in_specs=[row_in, w_in, w_in, w_out],
            out_specs=pl.BlockSpec((MOE_ROWS, PACKED), lambda i, be, nu: (i, 0)),
        ),
        out_shape=jax.ShapeDtypeStruct(xs.shape, jnp.uint32),
        compiler_params=_cparams(("arbitrary",)),
        name="moe_experts",
    )(blk_exp, n_used, xs, w_gate, w_up, w_down)


def _combine_ln_kernel(ys_hbm, dest_hbm, mf_ref, x_ref, mod_ref, g_ref, b_ref, o_ref,
                       idx_smem, y_buf, sem_idx, sem_row):
    i = pl.program_id(0)
    slot = i % 2
    per_tile = 2 * TM

    def row_copy(sl, g, j, k, src_row):
        return pltpu.make_async_copy(ys_hbm.at[pl.ds(src_row, 1)], y_buf.at[sl, k, g, pl.ds(j, 1)], sem_row.at[sl])

    def idx_copy(tile, sl):
        return pltpu.make_async_copy(dest_hbm.at[pl.ds(tile * per_tile, per_tile)],
                                     idx_smem.at[pl.ds(sl * per_tile, per_tile)], sem_idx)

    def request(sl, each_row):
        each_row(lambda g, j, k: row_copy(
            sl, g, j, k, idx_smem[sl * per_tile + g * (2 * ROW_UNROLL) + (2 * j + k)]).start())

    def reduce_tile():
        mf = mf_ref[...]
        y0 = _unpack_rows(y_buf[slot, 0].reshape(TM, PACKED))
        y1 = _unpack_rows(y_buf[slot, 1].reshape(TM, PACKED))
        y = mf[:, 0:1] * y0 + mf[:, 1:2] * y1
        r = DEEPNORM_ALPHA * x_ref[...] + mod_ref[0, 5:6, :] * y
        o_ref[...] = _layer_norm(r, g_ref[...], b_ref[...])

    n = pl.num_programs(0)

    @pl.when(i == 0)
    def _():
        idx_copy(0, 0).start()
        idx_copy(0, 0).wait()
        request(0, _each_row)
        pl.when(n > 1)(lambda: idx_copy(1, 1).start())

    _each_row(lambda g, j, k: row_copy(slot, g, j, k, 0).wait())

    @pl.when(i + 1 < n)
    def _():
        idx_copy(i + 1, 1 - slot).wait()
        request(1 - slot, _each_row_inline)
        reduce_tile()

    pl.when(i + 1 >= n)(reduce_tile)
    pl.when(i + 2 < n)(lambda: idx_copy(i + 2, slot).start())


def _combine_ln(ys, dest, mf, x1, mod_l, ln_g, ln_b, latent_only):
    mr = _mod_row(latent_only)
    n_tiles = _n_tiles(latent_only)
    return pl.pallas_call(
        _combine_ln_kernel,
        grid=(n_tiles,),
        in_specs=[
            pl.BlockSpec(memory_space=pl.ANY),
            pl.BlockSpec(memory_space=pl.ANY),
            pl.BlockSpec((TM, LANES), lambda i: (i, 0)),
            pl.BlockSpec((TM, D_MODEL), lambda i: (i, 0)),
            pl.BlockSpec((1, 6, D_MODEL), lambda i: (mr(i), 0, 0)),
            pl.BlockSpec((1, D_MODEL), lambda i: (0, 0)),
            pl.BlockSpec((1, D_MODEL), lambda i: (0, 0)),
        ],
        out_specs=pl.BlockSpec((TM, D_MODEL), lambda i: (i, 0)),
        out_shape=jax.ShapeDtypeStruct((n_tiles * TM, D_MODEL), F32),
        scratch_shapes=[
            pltpu.SMEM((2 * 2 * TM,), jnp.int32),
            pltpu.VMEM((2, 2, TM // ROW_UNROLL, ROW_UNROLL, PACKED), jnp.uint32),
            pltpu.SemaphoreType.DMA(()),
            pltpu.SemaphoreType.DMA((2,)),
        ],
        compiler_params=_cparams(("arbitrary",)),
        name="combine_ln",
    )(ys, dest, mf, x1, mod_l, ln_g, ln_b)


def _route_params(w_grp, b_grp, w_rexp, b_rexp):
    pad = LANES - N_GROUPS - N_EXPERTS
    w_route = jnp.concatenate([w_grp, w_rexp, jnp.zeros((D_MODEL, pad), F32)], axis=1)
    b_route = jnp.concatenate([b_grp, b_rexp, jnp.zeros((pad,), F32)])[None, :]
    return w_route, b_route


def _moe(x1, mi, mf, cnt, mod_l, layer, w_gate, w_up, w_down, ln_g, ln_b, latent_only):
    counts = cnt[0, :N_EXPERTS].astype(jnp.int32)
    padded = (counts + MOE_ROWS - 1) // MOE_ROWS * MOE_ROWS
    pad_end = jnp.cumsum(padded)
    pad_start = pad_end - padded
    n_tok = x1.shape[0]
    n_blocks = (2 * n_tok) // MOE_ROWS + N_EXPERTS
    experts = jnp.arange(N_EXPERTS, dtype=jnp.int32)
    start_of = jnp.sum(jnp.where(mi[:, 0:2, None] == experts, pad_start, 0), axis=-1)
    dest = (start_of + mi[:, 2:4]).reshape(-1)
    blk_start = jnp.arange(n_blocks, dtype=jnp.int32) * MOE_ROWS
    blk_exp = jnp.minimum(jnp.sum((pad_end[None, :] <= blk_start[:, None]).astype(jnp.int32), axis=1),
                          N_EXPERTS - 1)
    n_used = pad_end[-1:] // MOE_ROWS
    xs = _dispatch(pad_end, padded, x1, mod_l, dest, n_blocks, latent_only)
    ys = _experts(blk_exp, n_used, xs, layer, w_gate, w_up, w_down)
    return _combine_ln(ys, dest, mf, x1, mod_l, ln_g, ln_b, latent_only)


def kernel(x, c, ctx, c_ctx, ada_w, ada_b, ln1_g, ln1_b, ln2_g, ln2_b, ab_w_in, ab_w_lr_f, ab_b_lr_f, ab_w_lr_b, ab_b_lr_b, ab_gn_a, ab_gn_b, ab_w_out, c_w_qkv, c_lq1, c_lk1, c_lq2, c_lk2, c_subln_g, c_w_out, moe_w_grp, moe_b_grp, moe_w_rexp, moe_b_rexp, moe_w_gate, moe_w_up, moe_w_down):
    assert x.shape == (BATCH, SEQ, D_MODEL) and ctx.shape == (BATCH, CTX_LEN, D_MODEL)
    xs = jnp.concatenate([ctx, x], axis=1).reshape(N_ALL, D_MODEL)
    c_all = jnp.concatenate([c, c_ctx[None, :], jnp.zeros((16 - BATCH - 1, D_MODEL), F32)], axis=0)
    mod = _ada_tables(c_all, ada_w, ada_b).reshape(DEPTH, 16, 6, D_MODEL)

    rope_a = _rope_tables(DK_A, 1)
    rope_c = _rope_tables(DH_C, 2)
    dec_a = _retention_tables()

    for l in range(DEPTH):
        last = l == DEPTH - 1
        i = l // 2
        mod_l = mod[l]
        row = lambda v: v[None, :]
        route = _route_params(moe_w_grp[l], moe_b_grp[l], moe_w_rexp[l], moe_b_rexp[l])
        if l % 2 == 0:
            assert not last
            w_in = jnp.pad(ab_w_in[i], ((0, 0), (0, AB_COLS - ab_w_in.shape[2]))).astype(BF16)
            z = _mod_matmul(xs, mod_l, w_in)
            ya = _scan_a(z, *rope_a, dec_a, row(ab_gn_a[i]))
            wf = ab_w_lr_f[i].reshape(GLA_RANK, H_B // 2, LANES)
            wb = ab_w_lr_b[i].reshape(GLA_RANK, H_B // 2, LANES)
            wlr = jnp.zeros((H_B // 2, LANES, 2 * LANES), F32)
            wlr = wlr.at[:, 0:GLA_RANK, 0:LANES].set(jnp.swapaxes(wf, 0, 1))
            wlr = wlr.at[:, GLA_RANK:2 * GLA_RANK, LANES:].set(jnp.swapaxes(wb, 0, 1))
            blr = jnp.concatenate([ab_b_lr_f[i].reshape(H_B // 2, 1, LANES),
                                   ab_b_lr_b[i].reshape(H_B // 2, 1, LANES)], axis=-1)
            yb = _scan_b(z, wlr, blr, row(ab_gn_b[i]))
            x1, mi, mf, cnt = _proj_ln(ya, yb, 0, ab_w_out[i].astype(BF16), xs, mod_l, row(ln1_g[l]),
                                       row(ln1_b[l]), *route, False)
        else:
            lam_init = 0.8 - 0.6 * math.exp(-0.3 * l)
            lam = (jnp.exp(jnp.sum(c_lq1[i] * c_lk1[i], axis=-1))
                   - jnp.exp(jnp.sum(c_lq2[i] * c_lk2[i], axis=-1))).astype(F32) + lam_init
            z = _mod_matmul(xs, mod_l, c_w_qkv[i].astype(BF16))
            gsub = row(c_subln_g[i])
            y = _attention(z, lam, *rope_c, gsub, 1.0 - lam_init, last)
            x1, mi, mf, cnt = _proj_ln(y, y, 1, c_w_out[i].astype(BF16), xs, mod_l, row(ln1_g[l]),
                                       row(ln1_b[l]), *route, last)
        xs = _moe(x1, mi, mf, cnt, mod_l, l, moe_w_gate, moe_w_up, moe_w_down, row(ln2_g[l]), row(ln2_b[l]), last)
    return xs.reshape(BATCH, SEQ, D_MODEL)
```

```python
import functools
import math

import numpy as np
import jax
import jax.numpy as jnp
from jax import lax
from jax.experimental import pallas as pl
from jax.experimental.pallas import tpu as pltpu

F32 = jnp.float32
BF16 = jnp.bfloat16

D_MODEL = 1024
BATCH = 8
SEQ = 2048
DEPTH = 4
GRID_W = 64
CTX_LEN = 256
ROPE_BASE = 10000.0
LN_EPS = 1e-5
DEEPNORM_ALPHA = (2 * DEPTH) ** 0.25
H_A = 4
DK_A = 128
DV_A = 128
CHUNK_A = 128
RET_EXP_FWD = 5.0
RET_EXP_BWD = 5.5
H_B = 4
DK_B = 64
DV_B = 128
GLA_RANK = 16
GLA_TAU = 16.0
CHUNK_B = 64
H_C = 8
DH_C = 64
DV_C = 128
N_GROUPS = 4
EXPERTS_PER_GROUP = 8
N_EXPERTS = 32
D_EXPERT = 512

LANES = 128
T_ALL = CTX_LEN + SEQ
N_ALL = BATCH * T_ALL
TM = 256
TILES_PER_BATCH = T_ALL // TM
LATENT_TILES_PER_BATCH = SEQ // TM
AB_COLS = 29 * LANES
MOE_ROWS = 512
PACKED = D_MODEL // 2
ROW_UNROLL = 8
SCAN_UNROLL = 8
GLA_GROUP = 256
TQ = 256
VMEM_LIMIT = 56 * 1024 * 1024


def _cparams(sem):
    return pltpu.CompilerParams(dimension_semantics=sem, vmem_limit_bytes=VMEM_LIMIT)


def _silu(v):
    return v * (1.0 / (1.0 + jnp.exp(-v)))


def _n_tiles(latent_only):
    return BATCH * (LATENT_TILES_PER_BATCH if latent_only else TILES_PER_BATCH)


def _row_tile(latent_only):
    if latent_only:
        return lambda i: (i // LATENT_TILES_PER_BATCH) * TILES_PER_BATCH + 1 + i % LATENT_TILES_PER_BATCH
    return lambda i: i


def _mod_row(latent_only):
    if latent_only:
        return lambda i: i // LATENT_TILES_PER_BATCH
    return lambda i: jnp.where(i % TILES_PER_BATCH == 0, BATCH, i // TILES_PER_BATCH)


def _ada_kernel(c_ref, w_ref, b_ref, o_ref):
    sc = _silu(c_ref[...])
    o_ref[0] = jnp.dot(sc.astype(BF16), w_ref[0].astype(BF16), preferred_element_type=F32) + b_ref[0]


def _ada_tables(c_all, ada_w, ada_b):
    tn = 1536
    n_out = 6 * D_MODEL
    return pl.pallas_call(
        _ada_kernel,
        grid=(DEPTH, n_out // tn),
        in_specs=[
            pl.BlockSpec((16, D_MODEL), lambda l, j: (0, 0)),
            pl.BlockSpec((1, D_MODEL, tn), lambda l, j: (l, 0, j)),
            pl.BlockSpec((1, 1, tn), lambda l, j: (l, 0, j)),
        ],
        out_specs=pl.BlockSpec((1, 16, tn), lambda l, j: (l, 0, j)),
        out_shape=jax.ShapeDtypeStruct((DEPTH, 16, n_out), F32),
        compiler_params=_cparams(("arbitrary", "arbitrary")),
        name="ada_tables",
    )(c_all, ada_w, ada_b.reshape(DEPTH, 1, n_out))


def _modmm_kernel(x_ref, mod_ref, w_ref, o_ref):
    u = x_ref[...] * (1.0 + mod_ref[0, 1:2, :]) + mod_ref[0, 0:1, :]
    o_ref[...] = jnp.dot(u.astype(BF16), w_ref[...], preferred_element_type=F32)


def _mod_matmul(x, mod_l, w_bf16):
    n_out = w_bf16.shape[1]
    return pl.pallas_call(
        _modmm_kernel,
        grid=(N_ALL // TM,),
        in_specs=[
            pl.BlockSpec((TM, D_MODEL), lambda i: (i, 0)),
            pl.BlockSpec((1, 6, D_MODEL), lambda i: (_mod_row(False)(i), 0, 0)),
            pl.BlockSpec((D_MODEL, n_out), lambda i: (0, 0)),
        ],
        out_specs=pl.BlockSpec((TM, n_out), lambda i: (i, 0)),
        out_shape=jax.ShapeDtypeStruct((N_ALL, n_out), F32),
        compiler_params=_cparams(("arbitrary",)),
        name="mod_matmul",
    )(x, mod_l, w_bf16)


def _rope_tables(head_dim, reps):
    rows = SEQ // GRID_W
    row = np.repeat(np.arange(rows, dtype=np.float32), GRID_W)
    col = np.tile(np.arange(GRID_W, dtype=np.float32), rows)
    quarter = head_dim // 4
    inv = (ROPE_BASE ** (-np.arange(quarter, dtype=np.float32) / quarter)).astype(np.float32)
    ang_r = row[:, None] * inv
    ang_c = col[:, None] * inv
    ang = np.concatenate([ang_r, ang_r, ang_c, ang_c], axis=-1)
    cos = np.cos(ang).astype(np.float32)
    sin = np.sin(ang).astype(np.float32)
    q_idx = (np.arange(head_dim) // quarter) % 2
    sin_up = np.where(q_idx == 1, sin, 0.0).astype(np.float32)
    sin_dn = np.where(q_idx == 0, -sin, 0.0).astype(np.float32)

    def full(t, ctx_val):
        t = np.tile(t, (1, reps))
        return jnp.asarray(np.concatenate([np.full((CTX_LEN, t.shape[1]), ctx_val, np.float32), t], axis=0))

    return full(cos, 1.0), full(sin_up, 0.0), full(sin_dn, 0.0)


def _rope(x, cos, sin_up, sin_dn, quarter):
    width = x.shape[-1]
    return x * cos + pltpu.roll(x, quarter, 1) * sin_up + pltpu.roll(x, width - quarter, 1) * sin_dn


def _dot_tb(a, b):
    return lax.dot_general(a, b, (((1,), (1,)), ((), ())), preferred_element_type=F32)


def _dot_ta(a, b):
    return lax.dot_general(a, b, (((0,), (0,)), ((), ())), preferred_element_type=F32)


def _split_bf16(x, parts):
    out = []
    for _ in range(parts):
        t = x.astype(BF16)
        out.append(t)
        x = x - t.astype(F32)
    return out


def _dot_split(a, b):
    a_hi, a_lo = _split_bf16(a, 2)
    b_hi, b_lo = _split_bf16(b, 2)
    dot = lambda u, v: jnp.dot(u, v, preferred_element_type=F32)
    return dot(a_hi, b_hi) + (dot(a_hi, b_lo) + dot(a_lo, b_hi))


def _dot_mask(mask_bf16, x):
    return sum(jnp.dot(mask_bf16, t, preferred_element_type=F32) for t in reversed(_split_bf16(x, 3)))


def _retention_tables():
    c = CHUNK_A
    i = np.arange(c, dtype=np.float64)
    out = np.zeros((H_A, 7, c, LANES), np.float64)
    for h in range(H_A):
        lgf = np.log1p(-np.exp2(-(RET_EXP_FWD + h)))
        lgb = np.log1p(-np.exp2(-(RET_EXP_BWD + h)))
        d = i[:, None] - i[None, :]
        out[h, 0] = np.where(d >= 0, np.exp(lgf * d), np.exp(lgb * (-d - 1)))
        out[h, 1] = np.exp(lgf * (i + 1))[:, None]
        out[h, 2] = np.exp(lgb * (c - 1 - i))[:, None]
        out[h, 3] = np.exp(lgf * (c - 1 - i))[:, None]
        out[h, 4] = np.exp(lgb * i)[:, None]
        out[h, 5] = np.exp(lgf * c)
        out[h, 6] = np.exp(lgb * c)
    return jnp.asarray(out.astype(np.float32))


def _scan_a_kernel(q_ref, k_ref, v_ref, g_ref, cos_ref, sup_ref, sdn_ref, dec_ref, gn_ref, o_ref,
                   kr_scr, sb_scr):
    c = CHUNK_A
    n_ctx = CTX_LEN // c
    n_all = T_ALL // c
    scale = DK_A ** -0.5
    kr_scr[...] = _rope(k_ref[...], cos_ref[...], sup_ref[...], sdn_ref[...], DK_A // 4)
    dmat = dec_ref[0, 0]
    q_f, q_b, k_f, k_b = dec_ref[0, 1], dec_ref[0, 2], dec_ref[0, 3], dec_ref[0, 4]
    g_fc, g_bc = dec_ref[0, 5], dec_ref[0, 6]
    gn = gn_ref[...]
    zero = jnp.zeros((DK_A, DV_A), F32)

    def chunk(ci):
        return pl.ds(pl.multiple_of(ci * c, c), c)

    def kv_state(ci, k_dec):
        sl = chunk(ci)
        return _dot_ta((kr_scr[sl, :] * k_dec).astype(BF16), v_ref[sl, :].astype(BF16))

    def run(lo, hi, sf0, sb0):
        def bwd(j, sb):
            ci = hi - 1 - j
            sb_scr[ci] = sb
            return g_bc * sb + kv_state(ci, k_b)

        sb_fin = lax.fori_loop(0, hi - lo, bwd, sb0, unroll=SCAN_UNROLL)

        def fwd(j, sf):
            ci = lo + j
            sl = chunk(ci)
            q = _rope(q_ref[sl, :], cos_ref[sl, :], sup_ref[sl, :], sdn_ref[sl, :], DK_A // 4) * scale
            k = kr_scr[sl, :]
            vb = v_ref[sl, :].astype(BF16)
            att = _dot_tb(q.astype(BF16), k.astype(BF16)) * dmat
            o = jnp.dot(att.astype(BF16), vb, preferred_element_type=F32)
            o = o + jnp.dot((q * q_f).astype(BF16), sf.astype(BF16), preferred_element_type=F32)
            o = o + jnp.dot((q * q_b).astype(BF16), sb_scr[ci].astype(BF16), preferred_element_type=F32)
            o = o - jnp.mean(o, axis=-1, keepdims=True)
            o = o * lax.rsqrt(jnp.mean(o * o, axis=-1, keepdims=True) + LN_EPS)
            o_ref[sl, :] = _silu(g_ref[sl, :]) * (o * gn)
            return g_fc * sf + _dot_ta((k * k_f).astype(BF16), vb)

        sf_fin = lax.fori_loop(0, hi - lo, fwd, sf0, unroll=SCAN_UNROLL)
        return sf_fin, sb_fin

    sf_c, sb_c = run(0, n_ctx, zero, zero)
    run(n_ctx, n_all, sf_c, sb_c)


def _scan_a(z, cos, sup, sdn, dec, gn_a):
    blk = lambda col0: pl.BlockSpec((T_ALL, LANES), lambda b, h: (b, col0 + h))
    tbl = pl.BlockSpec((T_ALL, LANES), lambda b, h: (0, 0))
    return pl.pallas_call(
        _scan_a_kernel,
        grid=(BATCH, H_A),
        in_specs=[blk(0), blk(4), blk(8), blk(12), tbl, tbl, tbl,
                  pl.BlockSpec((1, 7, CHUNK_A, LANES), lambda b, h: (h, 0, 0, 0)),
                  pl.BlockSpec((1, LANES), lambda b, h: (0, h))],
        out_specs=pl.BlockSpec((T_ALL, LANES), lambda b, h: (b, h)),
        out_shape=jax.ShapeDtypeStruct((N_ALL, H_A * DV_A), F32),
        scratch_shapes=[pltpu.VMEM((T_ALL, LANES), F32),
                        pltpu.VMEM((T_ALL // CHUNK_A, DK_A, DV_A), F32)],
        compiler_params=_cparams(("arbitrary", "arbitrary")),
        name="scan_retention",
    )(z, z, z, z, cos, sup, sdn, dec, gn_a)


def _log_sigmoid(g):
    return jnp.minimum(g, 0.0) - jnp.log1p(jnp.exp(-jnp.abs(g)))


def _scan_b_kernel(q_ref, k_ref, v_ref, g_ref, lr_ref, wlr_ref, blr_ref, gn_ref, o_ref,
                   qf_scr, kf_scr, qb_scr, kb_scr, ktf_scr, ktb_scr, ef_scr, eb_scr, sb_scr):
    c = CHUNK_B
    n_ctx = CTX_LEN // c
    n_all = T_ALL // c
    per_group = GLA_GROUP // c
    scale = DK_B ** -0.5

    gi_r = lax.broadcasted_iota(jnp.int32, (GLA_GROUP, GLA_GROUP), 0)
    gi_c = lax.broadcasted_iota(jnp.int32, (GLA_GROUP, GLA_GROUP), 1)
    same_chunk = (gi_r // c) == (gi_c // c)
    prefix = (same_chunk & (gi_c <= gi_r)).astype(BF16)
    suffix = (same_chunk & (gi_c >= gi_r)).astype(BF16)

    def prepare(gi, carry):
        sl = pl.ds(pl.multiple_of(gi * GLA_GROUP, GLA_GROUP), GLA_GROUP)
        gates = _dot_split(lr_ref[sl, :], wlr_ref[0]) + blr_ref[0]
        laf = _log_sigmoid(gates[:, :LANES]) * (1.0 / GLA_TAU)
        lab = _log_sigmoid(gates[:, LANES:]) * (1.0 / GLA_TAU)
        b = _dot_mask(prefix, laf)
        rb = _dot_mask(suffix, lab)
        q = q_ref[sl, :] * scale
        k = k_ref[sl, :]
        qf_scr[sl, :] = (q * jnp.exp(b)).astype(BF16)
        kf_scr[sl, :] = (k * jnp.exp(-b)).astype(BF16)
        qb_scr[sl, :] = (q * jnp.exp(rb - lab)).astype(BF16)
        kb_scr[sl, :] = (k * jnp.exp(-rb)).astype(BF16)
        b3 = b.reshape(per_group, c, LANES)
        rb3 = rb.reshape(per_group, c, LANES)
        k3 = k.reshape(per_group, c, LANES)
        b_tot = b3[:, c - 1:c, :]
        rb_tot = rb3[:, 0:1, :]
        ktf_scr[sl, :] = (k3 * jnp.exp(b_tot - b3)).reshape(GLA_GROUP, LANES).astype(BF16)
        ktb_scr[sl, :] = (k3 * jnp.exp(rb_tot - rb3)).reshape(GLA_GROUP, LANES).astype(BF16)
        for m in range(per_group):
            ef_scr[gi * per_group + m] = jnp.broadcast_to(jnp.exp(b_tot[m]), (8, LANES))
            eb_scr[gi * per_group + m] = jnp.broadcast_to(jnp.exp(rb_tot[m]), (8, LANES))
        return carry

    lax.fori_loop(0, T_ALL // GLA_GROUP, prepare, 0)

    lane = lax.broadcasted_iota(jnp.int32, (1, LANES), 1)
    masks = [lane < DK_B, lane >= DK_B]
    ri = lax.broadcasted_iota(jnp.int32, (c, c), 0)
    cj = lax.broadcasted_iota(jnp.int32, (c, c), 1)
    lower = cj <= ri
    gn = gn_ref[...]
    zero = jnp.zeros((DV_B, LANES), F32)
    zero_b = jnp.zeros((), BF16)

    def chunk(ci):
        return pl.ds(pl.multiple_of(ci * c, c), c)

    def run(lo, hi, sf0, sb0):
        def bwd(j, sb):
            ci = hi - 1 - j
            sl = chunk(ci)
            e_tot = eb_scr[ci][0:1, :]
            kt = ktb_scr[sl, :]
            v = v_ref[sl, :]
            new = []
            for h in range(2):
                sb_scr[ci, h] = sb[h]
                vh = v[:, h * DV_B:(h + 1) * DV_B].astype(BF16)
                new.append(sb[h] * e_tot + _dot_ta(vh, jnp.where(masks[h], kt, zero_b)))
            return tuple(new)

        sb_fin = lax.fori_loop(0, hi - lo, bwd, sb0, unroll=SCAN_UNROLL)

        def fwd(j, sf):
            ci = lo + j
            sl = chunk(ci)
            e_tot = ef_scr[ci][0:1, :]
            qf, kf, qb, kb, kt = qf_scr[sl, :], kf_scr[sl, :], qb_scr[sl, :], kb_scr[sl, :], ktf_scr[sl, :]
            v = v_ref[sl, :]
            g = g_ref[sl, :]
            new = []
            for h in range(2):
                pick = lambda t: jnp.where(masks[h], t, zero_b)
                vh = v[:, h * DV_B:(h + 1) * DV_B].astype(BF16)
                qfh, qbh = pick(qf), pick(qb)
                att = jnp.where(lower, _dot_tb(qfh, pick(kf)), _dot_tb(qbh, pick(kb)))
                o = jnp.dot(att.astype(BF16), vh, preferred_element_type=F32)
                o = o + _dot_tb(qfh, sf[h].astype(BF16))
                o = o + _dot_tb(qbh, sb_scr[ci, h].astype(BF16))
                o = o * lax.rsqrt(jnp.mean(o * o, axis=-1, keepdims=True) + LN_EPS)
                cols = slice(h * DV_B, (h + 1) * DV_B)
                o_ref[sl, cols] = _silu(g[:, cols]) * (o * gn[:, cols])
                new.append(sf[h] * e_tot + _dot_ta(vh, pick(kt)))
            return tuple(new)

        sf_fin = lax.fori_loop(0, hi - lo, fwd, sf0, unroll=SCAN_UNROLL)
        return sf_fin, sb_fin

    sf_c, sb_c = run(0, n_ctx, (zero, zero), (zero, zero))
    run(n_ctx, n_all, sf_c, sb_c)


def _scan_b(z, wlr, blr, gn_b):
    pairs = H_B // 2
    return pl.pallas_call(
        _scan_b_kernel,
        grid=(BATCH, pairs),
        in_specs=[
            pl.BlockSpec((T_ALL, LANES), lambda b, p: (b, 16 + p)),
            pl.BlockSpec((T_ALL, LANES), lambda b, p: (b, 18 + p)),
            pl.BlockSpec((T_ALL, 2 * DV_B), lambda b, p: (b, 10 + p)),
            pl.BlockSpec((T_ALL, 2 * DV_B), lambda b, p: (b, 12 + p)),
            pl.BlockSpec((T_ALL, LANES), lambda b, p: (b, 28)),
            pl.BlockSpec((1, LANES, 2 * LANES), lambda b, p: (p, 0, 0)),
            pl.BlockSpec((1, 1, 2 * LANES), lambda b, p: (p, 0, 0)),
            pl.BlockSpec((1, 2 * DV_B), lambda b, p: (0, p)),
        ],
        out_specs=pl.BlockSpec((T_ALL, 2 * DV_B), lambda b, p: (b, p)),
        out_shape=jax.ShapeDtypeStruct((N_ALL, H_B * DV_B), F32),
        scratch_shapes=[pltpu.VMEM((T_ALL, LANES), BF16)] * 6
                       + [pltpu.VMEM((T_ALL // CHUNK_B, 8, LANES), F32)] * 2
                       + [pltpu.VMEM((T_ALL // CHUNK_B, 2, DV_B, LANES), F32)],
        compiler_params=_cparams(("arbitrary", "arbitrary")),
        name="scan_gla",
    )(z, z, z, z, z, wlr, blr, gn_b)


def _attn_kernel(lam_ref, q_ref, k_ref, v_ref, qcos_ref, qsup_ref, qsdn_ref, kcos_ref, ksup_ref, ksdn_ref,
                 gsub_ref, o_ref, k_scr, v_scr, *, post_scale, tile0):
    h = pl.program_id(1)
    t = pl.program_id(2)
    quarter = DH_C // 4
    scale = DH_C ** -0.5

    @pl.when(t == 0)
    def _():
        k_scr[...] = _rope(k_ref[...], kcos_ref[...], ksup_ref[...], ksdn_ref[...], quarter).astype(BF16)
        v_scr[...] = v_ref[...].astype(BF16)

    lam = lam_ref[h]
    lane = lax.broadcasted_iota(jnp.int32, (1, LANES), 1)
    m1 = (lane < DH_C).astype(F32)
    m2 = (lane >= DH_C).astype(F32)

    def attend(n_keys):
        q = _rope(q_ref[...], qcos_ref[...], qsup_ref[...], qsdn_ref[...], quarter) * scale
        kb = k_scr[0:n_keys, :]
        s1 = _dot_tb((q * m1).astype(BF16), kb)
        s2 = _dot_tb((q * m2).astype(BF16), kb)
        e1 = jnp.exp(s1 - jnp.max(s1, axis=-1, keepdims=True))
        e2 = jnp.exp(s2 - jnp.max(s2, axis=-1, keepdims=True))
        r1 = 1.0 / jnp.sum(e1, axis=-1, keepdims=True)
        r2 = lam / jnp.sum(e2, axis=-1, keepdims=True)
        a = e1 * r1 - e2 * r2
        o = jnp.dot(a.astype(BF16), v_scr[0:n_keys, :], preferred_element_type=F32)
        o = o * lax.rsqrt(jnp.mean(o * o, axis=-1, keepdims=True) + LN_EPS)
        o_ref[...] = o * (gsub_ref[...] * post_scale)

    if tile0 == 0:
        pl.when(t == 0)(lambda: attend(CTX_LEN))
        pl.when(t > 0)(lambda: attend(T_ALL))
    else:
        attend(T_ALL)


def _attention(z, lam, cos, sup, sdn, gsub, post_scale, latent_only):
    tile0 = 1 if latent_only else 0
    n_qt = TILES_PER_BATCH - tile0
    kern = functools.partial(_attn_kernel, post_scale=post_scale, tile0=tile0)
    kv = lambda col0: pl.BlockSpec((T_ALL, LANES), lambda b, h, t, lam_r: (b, col0 + h))
    q_tbl = pl.BlockSpec((TQ, LANES), lambda b, h, t, lam_r: (tile0 + t, 0))
    k_tbl = pl.BlockSpec((T_ALL, LANES), lambda b, h, t, lam_r: (0, 0))
    return pl.pallas_call(
        kern,
        grid_spec=pltpu.PrefetchScalarGridSpec(
            num_scalar_prefetch=1,
            grid=(BATCH, H_C, n_qt),
            in_specs=[pl.BlockSpec((TQ, LANES), lambda b, h, t, lam_r: (b * TILES_PER_BATCH + tile0 + t, h)),
                      kv(H_C), kv(2 * H_C), q_tbl, q_tbl, q_tbl, k_tbl, k_tbl, k_tbl,
                      pl.BlockSpec((1, LANES), lambda b, h, t, lam_r: (0, h))],
            out_specs=pl.BlockSpec((TQ, LANES), lambda b, h, t, lam_r: (b * n_qt + t, h)),
            scratch_shapes=[pltpu.VMEM((T_ALL, LANES), BF16), pltpu.VMEM((T_ALL, LANES), BF16)],
        ),
        out_shape=jax.ShapeDtypeStruct((BATCH * n_qt * TQ, H_C * DV_C), F32),
        compiler_params=_cparams(("arbitrary", "arbitrary", "arbitrary")),
        name="diff_attention",
    )(lam, z, z, z, cos, sup, sdn, cos, sup, sdn, gsub)


def _layer_norm(r, g, b):
    mu = jnp.mean(r, axis=-1, keepdims=True)
    d = r - mu
    var = jnp.mean(d * d, axis=-1, keepdims=True)
    return d * lax.rsqrt(var + LN_EPS) * g + b


def _route_tile(u, w_ref, b_ref, mi_ref, mf_ref, cnt_ref, carry):
    @pl.when(pl.program_id(0) == 0)
    def _():
        carry[...] = jnp.zeros_like(carry)

    logits = _dot_split(u, w_ref[...]) + b_ref[...]
    lane = lax.broadcasted_iota(jnp.int32, (TM, LANES), 1)
    lane_f = lane.astype(F32)
    neg = -jnp.inf
    big = 1e9

    gmask = lane < N_GROUPS
    gl = jnp.where(gmask, logits, neg)
    gmax = jnp.max(gl, axis=-1, keepdims=True)
    gidx = jnp.min(jnp.where(gl == gmax, lane_f, big), axis=-1, keepdims=True)
    gw = 1.0 / jnp.sum(jnp.where(gmask, jnp.exp(logits - gmax), 0.0), axis=-1, keepdims=True)

    e_lane = lane - N_GROUPS
    in_grp = (e_lane >= 0) & (e_lane < N_EXPERTS) & ((e_lane >> 3) == gidx.astype(jnp.int32))
    el = jnp.where(in_grp, logits, neg)
    v1 = jnp.max(el, axis=-1, keepdims=True)
    i1 = jnp.min(jnp.where(el == v1, lane_f, big), axis=-1, keepdims=True)
    el2 = jnp.where(lane_f == i1, neg, el)
    v2 = jnp.max(el2, axis=-1, keepdims=True)
    i2 = jnp.min(jnp.where(el2 == v2, lane_f, big), axis=-1, keepdims=True)
    t = jnp.exp(v2 - v1)
    c0 = gw / (1.0 + t)
    c1 = gw * t / (1.0 + t)
    e0 = i1 - N_GROUPS
    e1 = i2 - N_GROUPS

    oh0 = lane_f == e0
    oh1 = lane_f == e1
    cnt = oh0.astype(F32) + oh1.astype(F32)
    ri = lax.broadcasted_iota(jnp.int32, (TM, TM), 0)
    cj = lax.broadcasted_iota(jnp.int32, (TM, TM), 1)
    strict = (cj < ri).astype(BF16)
    before = jnp.dot(strict, cnt.astype(BF16), preferred_element_type=F32) + carry[0:1, :]
    r0 = jnp.sum(jnp.where(oh0, before, 0.0), axis=-1, keepdims=True)
    r1 = jnp.sum(jnp.where(oh1, before, 0.0), axis=-1, keepdims=True)
    carry[0:1, :] = carry[0:1, :] + jnp.sum(cnt, axis=0, keepdims=True)

    mi = jnp.where(lane == 0, e0, jnp.where(lane == 1, e1, jnp.where(lane == 2, r0, jnp.where(lane == 3, r1, 0.0))))
    mi_ref[...] = mi.astype(jnp.int32)
    mf_ref[...] = jnp.where(lane == 0, c0, jnp.where(lane == 1, c1, 0.0))
    cnt_ref[...] = carry[...]


def _proj_ln_kernel(y1_ref, y2_ref, w1_ref, w2_ref, x_ref, mod_ref, g_ref, b_ref, wr_ref, br_ref,
                    o_ref, mi_ref, mf_ref, cnt_ref, carry):
    y = jnp.dot(y1_ref[...].astype(BF16), w1_ref[...], preferred_element_type=F32)
    y = y + jnp.dot(y2_ref[...].astype(BF16), w2_ref[...], preferred_element_type=F32)
    r = DEEPNORM_ALPHA * x_ref[...] + mod_ref[0, 2:3, :] * y
    x1 = _layer_norm(r, g_ref[...], b_ref[...])
    o_ref[...] = x1
    u = x1 * (1.0 + mod_ref[0, 4:5, :]) + mod_ref[0, 3:4, :]
    _route_tile(u, wr_ref, br_ref, mi_ref, mf_ref, cnt_ref, carry)


def _proj_ln(y1, y2, col2, w_out_bf16, x, mod_l, ln_g, ln_b, w_route, b_route, latent_only):
    half = D_MODEL // 2
    rt = _row_tile(latent_only)
    mr = _mod_row(latent_only)
    n_tiles = _n_tiles(latent_only)
    n_tok = n_tiles * TM
    row_blk = lambda w: pl.BlockSpec((TM, w), lambda i: (i, 0))
    const = lambda shape: pl.BlockSpec(shape, lambda i: (0, 0))
    return pl.pallas_call(
        _proj_ln_kernel,
        grid=(n_tiles,),
        in_specs=[
            pl.BlockSpec((TM, half), lambda i: (i, 0)),
            pl.BlockSpec((TM, half), lambda i: (i, col2)),
            pl.BlockSpec((half, D_MODEL), lambda i: (0, 0)),
            pl.BlockSpec((half, D_MODEL), lambda i: (1, 0)),
            pl.BlockSpec((TM, D_MODEL), lambda i: (rt(i), 0)),
            pl.BlockSpec((1, 6, D_MODEL), lambda i: (mr(i), 0, 0)),
            const((1, D_MODEL)), const((1, D_MODEL)), const((D_MODEL, LANES)), const((1, LANES)),
        ],
        out_specs=[row_blk(D_MODEL), row_blk(LANES), row_blk(LANES), const((8, LANES))],
        out_shape=[jax.ShapeDtypeStruct((n_tok, D_MODEL), F32),
                   jax.ShapeDtypeStruct((n_tok, LANES), jnp.int32),
                   jax.ShapeDtypeStruct((n_tok, LANES), F32),
                   jax.ShapeDtypeStruct((8, LANES), F32)],
        scratch_shapes=[pltpu.VMEM((8, LANES), F32)],
        compiler_params=_cparams(("arbitrary",)),
        name="proj_ln_route",
    )(y1, y2, w_out_bf16, w_out_bf16, x, mod_l, ln_g, ln_b, w_route, b_route)


def _pack_rows(x):
    half = x.shape[-1] // 2
    bits = lambda t: lax.bitcast_convert_type(t.astype(BF16).astype(F32), jnp.uint32)
    return (bits(x[:, :half]) >> 16) | (bits(x[:, half:]) & jnp.uint32(0xFFFF0000))


def _unpack_rows(w):
    lo = lax.bitcast_convert_type(w << 16, F32)
    hi = lax.bitcast_convert_type(w & jnp.uint32(0xFFFF0000), F32)
    return jnp.concatenate([lo, hi], axis=-1)


def _each_row(fn):
    def body(g, carry):
        for j in range(ROW_UNROLL):
            for k in range(2):
                fn(g, j, k)
        return carry
    lax.fori_loop(0, TM // ROW_UNROLL, body, 0)


def _dispatch_kernel(pad_end_ref, padded_ref, x_ref, mod_ref, dest_hbm, xs_hbm,
                     idx_smem, u_scr, zero_scr, sem_idx, sem_row, sem_zero):
    i = pl.program_id(0)
    slot = i % 2
    per_tile = 2 * TM

    @pl.when(i == 0)
    def _():
        zero_scr[...] = jnp.zeros_like(zero_scr)

        def zero_block(first_row):
            rows = pl.ds(pl.multiple_of(first_row, MOE_ROWS), MOE_ROWS)
            return pltpu.make_async_copy(zero_scr, xs_hbm.at[rows], sem_zero)

        n_rows = xs_hbm.shape[0]
        total = pad_end_ref[N_EXPERTS - 1]
        for e in range(N_EXPERTS):
            pl.when(padded_ref[e] > 0)(lambda e=e: zero_block(pad_end_ref[e] - MOE_ROWS).start())
            pl.when(total + e * MOE_ROWS < n_rows)(lambda e=e: zero_block(total + e * MOE_ROWS).start())
        for e in range(N_EXPERTS):
            pl.when(padded_ref[e] > 0)(lambda e=e: zero_block(0).wait())
            pl.when(total + e * MOE_ROWS < n_rows)(lambda e=e: zero_block(0).wait())

    def idx_copy(tile, sl):
        return pltpu.make_async_copy(dest_hbm.at[pl.ds(tile * per_tile, per_tile)],
                                     idx_smem.at[pl.ds(sl * per_tile, per_tile)], sem_idx)

    pl.when(i == 0)(lambda: idx_copy(0, 0).start())
    idx_copy(i, slot).wait()
    pl.when(i + 1 < pl.num_programs(0))(lambda: idx_copy(i + 1, 1 - slot).start())
    u = x_ref[...] * (1.0 + mod_ref[0, 4:5, :]) + mod_ref[0, 3:4, :]
    u_scr[slot] = _pack_rows(u).reshape(TM // ROW_UNROLL, ROW_UNROLL, PACKED)

    def row_copy(sl, g, j, dst_row):
        return pltpu.make_async_copy(u_scr.at[sl, g, pl.ds(j, 1)], xs_hbm.at[pl.ds(dst_row, 1)], sem_row.at[sl])

    def dest_of(sl, g, j, k):
        return idx_smem[sl * per_tile + g * (2 * ROW_UNROLL) + (2 * j + k)]

    _each_row(lambda g, j, k: row_copy(slot, g, j, dest_of(slot, g, j, k)).start())

    @pl.when(i > 0)
    def _():
        _each_row(lambda g, j, k: row_copy(1 - slot, g, j, 0).wait())

    @pl.when(i == pl.num_programs(0) - 1)
    def _():
        _each_row(lambda g, j, k: row_copy(slot, g, j, 0).wait())


def _dispatch(pad_end, padded, x1, mod_l, dest, n_blocks, latent_only):
    mr = _mod_row(latent_only)
    return pl.pallas_call(
        _dispatch_kernel,
        grid_spec=pltpu.PrefetchScalarGridSpec(
            num_scalar_prefetch=2,
            grid=(_n_tiles(latent_only),),
            in_specs=[
                pl.BlockSpec((TM, D_MODEL), lambda i, pe, pd: (i, 0)),
                pl.BlockSpec((1, 6, D_MODEL), lambda i, pe, pd: (mr(i), 0, 0)),
                pl.BlockSpec(memory_space=pl.ANY),
            ],
            out_specs=pl.BlockSpec(memory_space=pl.ANY),
            scratch_shapes=[
                pltpu.SMEM((2 * 2 * TM,), jnp.int32),
                pltpu.VMEM((2, TM // ROW_UNROLL, ROW_UNROLL, PACKED), jnp.uint32),
                pltpu.VMEM((MOE_ROWS, PACKED), jnp.uint32),
                pltpu.SemaphoreType.DMA(()),
                pltpu.SemaphoreType.DMA((2,)),
                pltpu.SemaphoreType.DMA(()),
            ],
        ),
        out_shape=jax.ShapeDtypeStruct((n_blocks * MOE_ROWS, PACKED), jnp.uint32),
        compiler_params=_cparams(("arbitrary",)),
        name="moe_dispatch",
    )(pad_end, padded, x1, mod_l, dest)


def _expert_kernel(blk_exp_ref, n_used_ref, x_ref, wg_ref, wu_ref, wd_ref, o_ref):
    used = pl.program_id(0) < n_used_ref[0]

    @pl.when(used)
    def _():
        x = _unpack_rows(x_ref[...]).astype(BF16)
        gate = jnp.dot(x, wg_ref[0, 0].astype(BF16), preferred_element_type=F32)
        up = jnp.dot(x, wu_ref[0, 0].astype(BF16), preferred_element_type=F32)
        hid = (_silu(gate) * up).astype(BF16)
        o_ref[...] = _pack_rows(jnp.dot(hid, wd_ref[0, 0].astype(BF16), preferred_element_type=F32))

    @pl.when(jnp.logical_not(used))
    def _():
        o_ref[...] = jnp.zeros_like(o_ref)


def _experts(blk_exp, n_used, xs, layer, w_gate, w_up, w_down):
    n_blocks = xs.shape[0] // MOE_ROWS
    row_in = pl.BlockSpec((MOE_ROWS, PACKED), lambda i, be, nu: (jnp.minimum(i, nu[0] - 1), 0))
    w_in = pl.BlockSpec((1, 1, D_MODEL, D_EXPERT), lambda i, be, nu: (layer, be[i], 0, 0))
    w_out = pl.BlockSpec((1, 1, D_EXPERT, D_MODEL), lambda i, be, nu: (layer, be[i], 0, 0))
    return pl.pallas_call(
        _expert_kernel,
        grid_spec=pltpu.PrefetchScalarGridSpec(
            num_scalar_prefetch=2,
            grid=(n_blocks,),
            in_specs=[row_in, w_in, w_in, w_out],
            out_specs=pl.BlockSpec((MOE_ROWS, PACKED), lambda i, be, nu: (i, 0)),
        ),
        out_shape=jax.ShapeDtypeStruct(xs.shape, jnp.uint32),
        compiler_params=_cparams(("arbitrary",)),
        name="moe_experts",
    )(blk_exp, n_used, xs, w_gate, w_up, w_down)


def _combine_ln_kernel(ys_hbm, dest_hbm, mf_ref, x_ref, mod_ref, g_ref, b_ref, o_ref,
                       idx_smem, y_buf, sem_idx, sem_row):
    i = pl.program_id(0)
    slot = i % 2
    per_tile = 2 * TM

    def row_copy(sl, g, j, k, src_row):
        return pltpu.make_async_copy(ys_hbm.at[pl.ds(src_row, 1)], y_buf.at[sl, k, g, pl.ds(j, 1)], sem_row.at[sl])

    def idx_copy(tile, sl):
        return pltpu.make_async_copy(dest_hbm.at[pl.ds(tile * per_tile, per_tile)],
                                     idx_smem.at[pl.ds(sl * per_tile, per_tile)], sem_idx)

    def request(sl):
        _each_row(lambda g, j, k: row_copy(
            sl, g, j, k, idx_smem[sl * per_tile + g * (2 * ROW_UNROLL) + (2 * j + k)]).start())

    n = pl.num_programs(0)

    @pl.when(i == 0)
    def _():
        idx_copy(0, 0).start()
        idx_copy(0, 0).wait()
        request(0)
        pl.when(n > 1)(lambda: idx_copy(1, 1).start())

    @pl.when(i + 1 < n)
    def _():
        idx_copy(i + 1, 1 - slot).wait()
        request(1 - slot)
        pl.when(i + 2 < n)(lambda: idx_copy(i + 2, slot).start())

    _each_row(lambda g, j, k: row_copy(slot, g, j, k, 0).wait())

    mf = mf_ref[...]
    y0 = _unpack_rows(y_buf[slot, 0].reshape(TM, PACKED))
    y1 = _unpack_rows(y_buf[slot, 1].reshape(TM, PACKED))
    y = mf[:, 0:1] * y0 + mf[:, 1:2] * y1
    r = DEEPNORM_ALPHA * x_ref[...] + mod_ref[0, 5:6, :] * y
    o_ref[...] = _layer_norm(r, g_ref[...], b_ref[...])


def _combine_ln(ys, dest, mf, x1, mod_l, ln_g, ln_b, latent_only):
    mr = _mod_row(latent_only)
    n_tiles = _n_tiles(latent_only)
    return pl.pallas_call(
        _combine_ln_kernel,
        grid=(n_tiles,),
        in_specs=[
            pl.BlockSpec(memory_space=pl.ANY),
            pl.BlockSpec(memory_space=pl.ANY),
            pl.BlockSpec((TM, LANES), lambda i: (i, 0)),
            pl.BlockSpec((TM, D_MODEL), lambda i: (i, 0)),
            pl.BlockSpec((1, 6, D_MODEL), lambda i: (mr(i), 0, 0)),
            pl.BlockSpec((1, D_MODEL), lambda i: (0, 0)),
            pl.BlockSpec((1, D_MODEL), lambda i: (0, 0)),
        ],
        out_specs=pl.BlockSpec((TM, D_MODEL), lambda i: (i, 0)),
        out_shape=jax.ShapeDtypeStruct((n_tiles * TM, D_MODEL), F32),
        scratch_shapes=[
            pltpu.SMEM((2 * 2 * TM,), jnp.int32),
            pltpu.VMEM((2, 2, TM // ROW_UNROLL, ROW_UNROLL, PACKED), jnp.uint32),
            pltpu.SemaphoreType.DMA(()),
            pltpu.SemaphoreType.DMA((2,)),
        ],
        compiler_params=_cparams(("arbitrary",)),
        name="combine_ln",
    )(ys, dest, mf, x1, mod_l, ln_g, ln_b)


def _route_params(w_grp, b_grp, w_rexp, b_rexp):
    pad = LANES - N_GROUPS - N_EXPERTS
    w_route = jnp.concatenate([w_grp, w_rexp, jnp.zeros((D_MODEL, pad), F32)], axis=1)
    b_route = jnp.concatenate([b_grp, b_rexp, jnp.zeros((pad,), F32)])[None, :]
    return w_route, b_route


def _moe(x1, mi, mf, cnt, mod_l, layer, w_gate, w_up, w_down, ln_g, ln_b, latent_only):
    counts = cnt[0, :N_EXPERTS].astype(jnp.int32)
    padded = (counts + MOE_ROWS - 1) // MOE_ROWS * MOE_ROWS
    pad_end = jnp.cumsum(padded)
    pad_start = pad_end - padded
    n_tok = x1.shape[0]
    n_blocks = (2 * n_tok) // MOE_ROWS + N_EXPERTS
    experts = jnp.arange(N_EXPERTS, dtype=jnp.int32)
    start_of = jnp.sum(jnp.where(mi[:, 0:2, None] == experts, pad_start, 0), axis=-1)
    dest = (start_of + mi[:, 2:4]).reshape(-1)
    blk_start = jnp.arange(n_blocks, dtype=jnp.int32) * MOE_ROWS
    blk_exp = jnp.minimum(jnp.sum((pad_end[None, :] <= blk_start[:, None]).astype(jnp.int32), axis=1),
                          N_EXPERTS - 1)
    n_used = pad_end[-1:] // MOE_ROWS
    xs = _dispatch(pad_end, padded, x1, mod_l, dest, n_blocks, latent_only)
    ys = _experts(blk_exp, n_used, xs, layer, w_gate, w_up, w_down)
    return _combine_ln(ys, dest, mf, x1, mod_l, ln_g, ln_b, latent_only)


def kernel(x, c, ctx, c_ctx, ada_w, ada_b, ln1_g, ln1_b, ln2_g, ln2_b, ab_w_in, ab_w_lr_f, ab_b_lr_f, ab_w_lr_b, ab_b_lr_b, ab_gn_a, ab_gn_b, ab_w_out, c_w_qkv, c_lq1, c_lk1, c_lq2, c_lk2, c_subln_g, c_w_out, moe_w_grp, moe_b_grp, moe_w_rexp, moe_b_rexp, moe_w_gate, moe_w_up, moe_w_down):
    assert x.shape == (BATCH, SEQ, D_MODEL) and ctx.shape == (BATCH, CTX_LEN, D_MODEL)
    xs = jnp.concatenate([ctx, x], axis=1).reshape(N_ALL, D_MODEL)
    c_all = jnp.concatenate([c, c_ctx[None, :], jnp.zeros((16 - BATCH - 1, D_MODEL), F32)], axis=0)
    mod = _ada_tables(c_all, ada_w, ada_b).reshape(DEPTH, 16, 6, D_MODEL)

    rope_a = _rope_tables(DK_A, 1)
    rope_c = _rope_tables(DH_C, 2)
    dec_a = _retention_tables()

    for l in range(DEPTH):
        last = l == DEPTH - 1
        i = l // 2
        mod_l = mod[l]
        row = lambda v: v[None, :]
        route = _route_params(moe_w_grp[l], moe_b_grp[l], moe_w_rexp[l], moe_b_rexp[l])
        if l % 2 == 0:
            assert not last
            w_in = jnp.pad(ab_w_in[i], ((0, 0), (0, AB_COLS - ab_w_in.shape[2]))).astype(BF16)
            z = _mod_matmul(xs, mod_l, w_in)
            ya = _scan_a(z, *rope_a, dec_a, row(ab_gn_a[i]))
            wf = ab_w_lr_f[i].reshape(GLA_RANK, H_B // 2, LANES)
            wb = ab_w_lr_b[i].reshape(GLA_RANK, H_B // 2, LANES)
            wlr = jnp.zeros((H_B // 2, LANES, 2 * LANES), F32)
            wlr = wlr.at[:, 0:GLA_RANK, 0:LANES].set(jnp.swapaxes(wf, 0, 1))
            wlr = wlr.at[:, GLA_RANK:2 * GLA_RANK, LANES:].set(jnp.swapaxes(wb, 0, 1))
            blr = jnp.concatenate([ab_b_lr_f[i].reshape(H_B // 2, 1, LANES),
                                   ab_b_lr_b[i].reshape(H_B // 2, 1, LANES)], axis=-1)
            yb = _scan_b(z, wlr, blr, row(ab_gn_b[i]))
            x1, mi, mf, cnt = _proj_ln(ya, yb, 0, ab_w_out[i].astype(BF16), xs, mod_l, row(ln1_g[l]),
                                       row(ln1_b[l]), *route, False)
        else:
            lam_init = 0.8 - 0.6 * math.exp(-0.3 * l)
            lam = (jnp.exp(jnp.sum(c_lq1[i] * c_lk1[i], axis=-1))
                   - jnp.exp(jnp.sum(c_lq2[i] * c_lk2[i], axis=-1))).astype(F32) + lam_init
            z = _mod_matmul(xs, mod_l, c_w_qkv[i].astype(BF16))
            gsub = row(c_subln_g[i])
            y = _attention(z, lam, *rope_c, gsub, 1.0 - lam_init, last)
            x1, mi, mf, cnt = _proj_ln(y, y, 1, c_w_out[i].astype(BF16), xs, mod_l, row(ln1_g[l]),
                                       row(ln1_b[l]), *route, last)
        xs = _moe(x1, mi, mf, cnt, mod_l, l, moe_w_gate, moe_w_up, moe_w_down, row(ln2_g[l]), row(ln2_b[l]), last)
    return xs.reshape(BATCH, SEQ, D_MODEL)
```

```python
import functools
import math

import numpy as np
import jax
import jax.numpy as jnp
from jax import lax
from jax.experimental import pallas as pl
from jax.experimental.pallas import tpu as pltpu

F32 = jnp.float32
BF16 = jnp.bfloat16

D_MODEL = 1024
BATCH = 8
SEQ = 2048
DEPTH = 4
GRID_W = 64
CTX_LEN = 256
ROPE_BASE = 10000.0
LN_EPS = 1e-5
DEEPNORM_ALPHA = (2 * DEPTH) ** 0.25
H_A = 4
DK_A = 128
DV_A = 128
CHUNK_A = 128
RET_EXP_FWD = 5.0
RET_EXP_BWD = 5.5
H_B = 4
DK_B = 64
DV_B = 128
GLA_RANK = 16
GLA_TAU = 16.0
CHUNK_B = 64
H_C = 8
DH_C = 64
DV_C = 128
N_GROUPS = 4
EXPERTS_PER_GROUP = 8
N_EXPERTS = 32
D_EXPERT = 512

LANES = 128
T_ALL = CTX_LEN + SEQ
N_ALL = BATCH * T_ALL
TM = 256
TILES_PER_BATCH = T_ALL // TM
LATENT_TILES_PER_BATCH = SEQ // TM
AB_COLS = 29 * LANES
MOE_ROWS = 512
PACKED = D_MODEL // 2
ROW_UNROLL = 8
SCAN_UNROLL = 8
GLA_GROUP = 256
TQ = 256
VMEM_LIMIT = 56 * 1024 * 1024


def _cparams(sem):
    return pltpu.CompilerParams(dimension_semantics=sem, vmem_limit_bytes=VMEM_LIMIT)


def _silu(v):
    return v * (1.0 / (1.0 + jnp.exp(-v)))


def _n_tiles(latent_only):
    return BATCH * (LATENT_TILES_PER_BATCH if latent_only else TILES_PER_BATCH)


def _row_tile(latent_only):
    if latent_only:
        return lambda i: (i // LATENT_TILES_PER_BATCH) * TILES_PER_BATCH + 1 + i % LATENT_TILES_PER_BATCH
    return lambda i: i


def _mod_row(latent_only):
    if latent_only:
        return lambda i: i // LATENT_TILES_PER_BATCH
    return lambda i: jnp.where(i % TILES_PER_BATCH == 0, BATCH, i // TILES_PER_BATCH)


def _ada_kernel(c_ref, w_ref, b_ref, o_ref):
    sc = _silu(c_ref[...])
    o_ref[0] = jnp.dot(sc.astype(BF16), w_ref[0].astype(BF16), preferred_element_type=F32) + b_ref[0]


def _ada_tables(c_all, ada_w, ada_b):
    tn = 1536
    n_out = 6 * D_MODEL
    return pl.pallas_call(
        _ada_kernel,
        grid=(DEPTH, n_out // tn),
        in_specs=[
            pl.BlockSpec((16, D_MODEL), lambda l, j: (0, 0)),
            pl.BlockSpec((1, D_MODEL, tn), lambda l, j: (l, 0, j)),
            pl.BlockSpec((1, 1, tn), lambda l, j: (l, 0, j)),
        ],
        out_specs=pl.BlockSpec((1, 16, tn), lambda l, j: (l, 0, j)),
        out_shape=jax.ShapeDtypeStruct((DEPTH, 16, n_out), F32),
        compiler_params=_cparams(("arbitrary", "arbitrary")),
        name="ada_tables",
    )(c_all, ada_w, ada_b.reshape(DEPTH, 1, n_out))


def _modmm_kernel(x_ref, mod_ref, w_ref, cos_ref, sup_ref, sdn_ref, o_ref, *, rope_blocks, quarter):
    u = x_ref[...] * (1.0 + mod_ref[0, 1:2, :]) + mod_ref[0, 0:1, :]
    z = jnp.dot(u.astype(BF16), w_ref[...], preferred_element_type=F32)
    cos, sup, sdn = cos_ref[...], sup_ref[...], sdn_ref[...]
    for c in range(rope_blocks):
        cols = slice(c * LANES, (c + 1) * LANES)
        o_ref[:, cols] = _rope(z[:, cols], cos, sup, sdn, quarter)
    o_ref[:, rope_blocks * LANES:] = z[:, rope_blocks * LANES:]


def _mod_matmul(x, mod_l, w_bf16, cos, sup, sdn, rope_blocks, quarter):
    n_out = w_bf16.shape[1]
    tbl = pl.BlockSpec((TM, LANES), lambda i: (i % TILES_PER_BATCH, 0))
    return pl.pallas_call(
        functools.partial(_modmm_kernel, rope_blocks=rope_blocks, quarter=quarter),
        grid=(N_ALL // TM,),
        in_specs=[
            pl.BlockSpec((TM, D_MODEL), lambda i: (i, 0)),
            pl.BlockSpec((1, 6, D_MODEL), lambda i: (_mod_row(False)(i), 0, 0)),
            pl.BlockSpec((D_MODEL, n_out), lambda i: (0, 0)),
            tbl, tbl, tbl,
        ],
        out_specs=pl.BlockSpec((TM, n_out), lambda i: (i, 0)),
        out_shape=jax.ShapeDtypeStruct((N_ALL, n_out), F32),
        compiler_params=_cparams(("arbitrary",)),
        name="mod_matmul",
    )(x, mod_l, w_bf16, cos, sup, sdn)


def _rope_tables(head_dim, reps):
    rows = SEQ // GRID_W
    row = np.repeat(np.arange(rows, dtype=np.float32), GRID_W)
    col = np.tile(np.arange(GRID_W, dtype=np.float32), rows)
    quarter = head_dim // 4
    inv = (ROPE_BASE ** (-np.arange(quarter, dtype=np.float32) / quarter)).astype(np.float32)
    ang_r = row[:, None] * inv
    ang_c = col[:, None] * inv
    ang = np.concatenate([ang_r, ang_r, ang_c, ang_c], axis=-1)
    cos = np.cos(ang).astype(np.float32)
    sin = np.sin(ang).astype(np.float32)
    q_idx = (np.arange(head_dim) // quarter) % 2
    sin_up = np.where(q_idx == 1, sin, 0.0).astype(np.float32)
    sin_dn = np.where(q_idx == 0, -sin, 0.0).astype(np.float32)

    def full(t, ctx_val):
        t = np.tile(t, (1, reps))
        return jnp.asarray(np.concatenate([np.full((CTX_LEN, t.shape[1]), ctx_val, np.float32), t], axis=0))

    return full(cos, 1.0), full(sin_up, 0.0), full(sin_dn, 0.0)


def _rope(x, cos, sin_up, sin_dn, quarter):
    width = x.shape[-1]
    return x * cos + pltpu.roll(x, quarter, 1) * sin_up + pltpu.roll(x, width - quarter, 1) * sin_dn


def _dot_tb(a, b):
    return lax.dot_general(a, b, (((1,), (1,)), ((), ())), preferred_element_type=F32)


def _dot_ta(a, b):
    return lax.dot_general(a, b, (((0,), (0,)), ((), ())), preferred_element_type=F32)


def _split_bf16(x, parts):
    out = []
    for _ in range(parts):
        t = x.astype(BF16)
        out.append(t)
        x = x - t.astype(F32)
    return out


def _dot_split(a, b):
    a_hi, a_lo = _split_bf16(a, 2)
    b_hi, b_lo = _split_bf16(b, 2)
    dot = lambda u, v: jnp.dot(u, v, preferred_element_type=F32)
    n = b.shape[1]
    both = dot(a_hi, jnp.concatenate([b_hi, b_lo], axis=1))
    return both[:, :n] + (both[:, n:] + dot(a_lo, b_hi))


def _dot_mask(mask_bf16, x):
    return sum(jnp.dot(mask_bf16, t, preferred_element_type=F32) for t in reversed(_split_bf16(x, 3)))


def _retention_tables():
    c = CHUNK_A
    i = np.arange(c, dtype=np.float64)
    out = np.zeros((H_A, 7, c, LANES), np.float64)
    for h in range(H_A):
        lgf = np.log1p(-np.exp2(-(RET_EXP_FWD + h)))
        lgb = np.log1p(-np.exp2(-(RET_EXP_BWD + h)))
        d = i[:, None] - i[None, :]
        out[h, 0] = np.where(d >= 0, np.exp(lgf * d), np.exp(lgb * (-d - 1)))
        out[h, 1] = np.exp(lgf * (i + 1))[:, None]
        out[h, 2] = np.exp(lgb * (c - 1 - i))[:, None]
        out[h, 3] = np.exp(lgf * (c - 1 - i))[:, None]
        out[h, 4] = np.exp(lgb * i)[:, None]
        out[h, 5] = np.exp(lgf * c)
        out[h, 6] = np.exp(lgb * c)
    return jnp.asarray(out.astype(np.float32))


def _scan_a_kernel(q_ref, k_ref, v_ref, g_ref, dec_ref, gn_ref, o_ref, sb_scr):
    c = CHUNK_A
    n_ctx = CTX_LEN // c
    n_all = T_ALL // c
    scale = DK_A ** -0.5
    dmat = dec_ref[0, 0]
    q_f, q_b, k_f, k_b = dec_ref[0, 1], dec_ref[0, 2], dec_ref[0, 3], dec_ref[0, 4]
    g_fc, g_bc = dec_ref[0, 5], dec_ref[0, 6]
    gn = gn_ref[...]
    zero = jnp.zeros((DK_A, DV_A), F32)

    def chunk(ci):
        return pl.ds(pl.multiple_of(ci * c, c), c)

    def kv_state(ci, k_dec):
        sl = chunk(ci)
        return _dot_ta((k_ref[sl, :] * k_dec).astype(BF16), v_ref[sl, :].astype(BF16))

    def run(lo, hi, sf0, sb0):
        def bwd(j, sb):
            ci = hi - 1 - j
            sb_scr[ci] = sb
            return g_bc * sb + kv_state(ci, k_b)

        sb_fin = lax.fori_loop(0, hi - lo, bwd, sb0, unroll=SCAN_UNROLL)

        def fwd(j, sf):
            ci = lo + j
            sl = chunk(ci)
            q = q_ref[sl, :] * scale
            k = k_ref[sl, :]
            vb = v_ref[sl, :].astype(BF16)
            att = _dot_tb(q.astype(BF16), k.astype(BF16)) * dmat
            o = jnp.dot(att.astype(BF16), vb, preferred_element_type=F32)
            o = o + jnp.dot((q * q_f).astype(BF16), sf.astype(BF16), preferred_element_type=F32)
            o = o + jnp.dot((q * q_b).astype(BF16), sb_scr[ci].astype(BF16), preferred_element_type=F32)
            o = o - jnp.mean(o, axis=-1, keepdims=True)
            o = o * lax.rsqrt(jnp.mean(o * o, axis=-1, keepdims=True) + LN_EPS)
            o_ref[sl, :] = _silu(g_ref[sl, :]) * (o * gn)
            return g_fc * sf + _dot_ta((k * k_f).astype(BF16), vb)

        sf_fin = lax.fori_loop(0, hi - lo, fwd, sf0, unroll=SCAN_UNROLL)
        return sf_fin, sb_fin

    sf_c, sb_c = run(0, n_ctx, zero, zero)
    run(n_ctx, n_all, sf_c, sb_c)


def _scan_a(z, dec, gn_a):
    blk = lambda col0: pl.BlockSpec((T_ALL, LANES), lambda b, h: (b, col0 + h))
    return pl.pallas_call(
        _scan_a_kernel,
        grid=(BATCH, H_A),
        in_specs=[blk(0), blk(4), blk(8), blk(12),
                  pl.BlockSpec((1, 7, CHUNK_A, LANES), lambda b, h: (h, 0, 0, 0)),
                  pl.BlockSpec((1, LANES), lambda b, h: (0, h))],
        out_specs=pl.BlockSpec((T_ALL, LANES), lambda b, h: (b, h)),
        out_shape=jax.ShapeDtypeStruct((N_ALL, H_A * DV_A), F32),
        scratch_shapes=[pltpu.VMEM((T_ALL // CHUNK_A, DK_A, DV_A), F32)],
        compiler_params=_cparams(("arbitrary", "arbitrary")),
        name="scan_retention",
    )(z, z, z, z, dec, gn_a)


def _log_sigmoid(g):
    return jnp.minimum(g, 0.0) - jnp.log1p(jnp.exp(-jnp.abs(g)))


def _scan_b_kernel(q_ref, k_ref, v_ref, g_ref, lr_ref, wlr_ref, blr_ref, gn_ref, o_ref,
                   qf_scr, kf_scr, qb_scr, kb_scr, ktf_scr, ktb_scr, ef_scr, eb_scr, sb_scr):
    c = CHUNK_B
    n_ctx = CTX_LEN // c
    n_all = T_ALL // c
    per_group = GLA_GROUP // c
    scale = DK_B ** -0.5

    gi_r = lax.broadcasted_iota(jnp.int32, (GLA_GROUP, GLA_GROUP), 0)
    gi_c = lax.broadcasted_iota(jnp.int32, (GLA_GROUP, GLA_GROUP), 1)
    same_chunk = (gi_r // c) == (gi_c // c)
    prefix = (same_chunk & (gi_c <= gi_r)).astype(BF16)
    suffix = (same_chunk & (gi_c >= gi_r)).astype(BF16)

    def prepare(gi, carry):
        sl = pl.ds(pl.multiple_of(gi * GLA_GROUP, GLA_GROUP), GLA_GROUP)
        gates = _dot_split(lr_ref[sl, :], wlr_ref[0]) + blr_ref[0]
        laf = _log_sigmoid(gates[:, :LANES]) * (1.0 / GLA_TAU)
        lab = _log_sigmoid(gates[:, LANES:]) * (1.0 / GLA_TAU)
        b = _dot_mask(prefix, laf)
        rb = _dot_mask(suffix, lab)
        q = q_ref[sl, :] * scale
        k = k_ref[sl, :]
        qf_scr[sl, :] = (q * jnp.exp(b)).astype(BF16)
        kf_scr[sl, :] = (k * jnp.exp(-b)).astype(BF16)
        qb_scr[sl, :] = (q * jnp.exp(rb - lab)).astype(BF16)
        kb_scr[sl, :] = (k * jnp.exp(-rb)).astype(BF16)
        b3 = b.reshape(per_group, c, LANES)
        rb3 = rb.reshape(per_group, c, LANES)
        k3 = k.reshape(per_group, c, LANES)
        b_tot = b3[:, c - 1:c, :]
        rb_tot = rb3[:, 0:1, :]
        ktf_scr[sl, :] = (k3 * jnp.exp(b_tot - b3)).reshape(GLA_GROUP, LANES).astype(BF16)
        ktb_scr[sl, :] = (k3 * jnp.exp(rb_tot - rb3)).reshape(GLA_GROUP, LANES).astype(BF16)
        for m in range(per_group):
            ef_scr[gi * per_group + m] = jnp.broadcast_to(jnp.exp(b_tot[m]), (8, LANES))
            eb_scr[gi * per_group + m] = jnp.broadcast_to(jnp.exp(rb_tot[m]), (8, LANES))
        return carry

    lax.fori_loop(0, T_ALL // GLA_GROUP, prepare, 0)

    lane = lax.broadcasted_iota(jnp.int32, (1, LANES), 1)
    masks = [lane < DK_B, lane >= DK_B]
    ri = lax.broadcasted_iota(jnp.int32, (c, c), 0)
    cj = lax.broadcasted_iota(jnp.int32, (c, c), 1)
    lower = cj <= ri
    gn = gn_ref[...]
    zero = jnp.zeros((DV_B, LANES), F32)
    zero_b = jnp.zeros((), BF16)

    def chunk(ci):
        return pl.ds(pl.multiple_of(ci * c, c), c)

    def run(lo, hi, sf0, sb0):
        def bwd(j, sb):
            ci = hi - 1 - j
            sl = chunk(ci)
            e_tot = eb_scr[ci][0:1, :]
            kt = ktb_scr[sl, :]
            v = v_ref[sl, :]
            new = []
            for h in range(2):
                sb_scr[ci, h] = sb[h]
                vh = v[:, h * DV_B:(h + 1) * DV_B].astype(BF16)
                new.append(sb[h] * e_tot + _dot_ta(vh, jnp.where(masks[h], kt, zero_b)))
            return tuple(new)

        sb_fin = lax.fori_loop(0, hi - lo, bwd, sb0, unroll=SCAN_UNROLL)

        def fwd(j, sf):
            ci = lo + j
            sl = chunk(ci)
            e_tot = ef_scr[ci][0:1, :]
            qf, kf, qb, kb, kt = qf_scr[sl, :], kf_scr[sl, :], qb_scr[sl, :], kb_scr[sl, :], ktf_scr[sl, :]
            v = v_ref[sl, :]
            g = g_ref[sl, :]
            new = []
            for h in range(2):
                pick = lambda t: jnp.where(masks[h], t, zero_b)
                vh = v[:, h * DV_B:(h + 1) * DV_B].astype(BF16)
                qfh, qbh = pick(qf), pick(qb)
                att = jnp.where(lower, _dot_tb(qfh, pick(kf)), _dot_tb(qbh, pick(kb)))
                o = jnp.dot(att.astype(BF16), vh, preferred_element_type=F32)
                o = o + _dot_tb(qfh, sf[h].astype(BF16))
                o = o + _dot_tb(qbh, sb_scr[ci, h].astype(BF16))
                o = o * lax.rsqrt(jnp.mean(o * o, axis=-1, keepdims=True) + LN_EPS)
                cols = slice(h * DV_B, (h + 1) * DV_B)
                o_ref[sl, cols] = _silu(g[:, cols]) * (o * gn[:, cols])
                new.append(sf[h] * e_tot + _dot_ta(vh, pick(kt)))
            return tuple(new)

        sf_fin = lax.fori_loop(0, hi - lo, fwd, sf0, unroll=SCAN_UNROLL)
        return sf_fin, sb_fin

    sf_c, sb_c = run(0, n_ctx, (zero, zero), (zero, zero))
    run(n_ctx, n_all, sf_c, sb_c)


def _scan_b(z, wlr, blr, gn_b):
    pairs = H_B // 2
    return pl.pallas_call(
        _scan_b_kernel,
        grid=(BATCH, pairs),
        in_specs=[
            pl.BlockSpec((T_ALL, LANES), lambda b, p: (b, 16 + p)),
            pl.BlockSpec((T_ALL, LANES), lambda b, p: (b, 18 + p)),
            pl.BlockSpec((T_ALL, 2 * DV_B), lambda b, p: (b, 10 + p)),
            pl.BlockSpec((T_ALL, 2 * DV_B), lambda b, p: (b, 12 + p)),
            pl.BlockSpec((T_ALL, LANES), lambda b, p: (b, 28)),
            pl.BlockSpec((1, LANES, 2 * LANES), lambda b, p: (p, 0, 0)),
            pl.BlockSpec((1, 1, 2 * LANES), lambda b, p: (p, 0, 0)),
            pl.BlockSpec((1, 2 * DV_B), lambda b, p: (0, p)),
        ],
        out_specs=pl.BlockSpec((T_ALL, 2 * DV_B), lambda b, p: (b, p)),
        out_shape=jax.ShapeDtypeStruct((N_ALL, H_B * DV_B), F32),
        scratch_shapes=[pltpu.VMEM((T_ALL, LANES), BF16)] * 6
                       + [pltpu.VMEM((T_ALL // CHUNK_B, 8, LANES), F32)] * 2
                       + [pltpu.VMEM((T_ALL // CHUNK_B, 2, DV_B, LANES), F32)],
        compiler_params=_cparams(("arbitrary", "arbitrary")),
        name="scan_gla",
    )(z, z, z, z, z, wlr, blr, gn_b)


def _attn_kernel(lam_ref, q_ref, k_ref, v_ref, gsub_ref, o_ref, k_scr, v_scr, *, post_scale, tile0):
    h = pl.program_id(1)
    t = pl.program_id(2)
    scale = DH_C ** -0.5

    @pl.when(t == 0)
    def _():
        k_scr[...] = k_ref[...].astype(BF16)
        v_scr[...] = v_ref[...].astype(BF16)

    lam = lam_ref[h]
    lane = lax.broadcasted_iota(jnp.int32, (1, LANES), 1)
    m1 = (lane < DH_C).astype(F32)
    m2 = (lane >= DH_C).astype(F32)

    def attend(n_keys):
        q = q_ref[...] * scale
        kb = k_scr[0:n_keys, :]
        s1 = _dot_tb((q * m1).astype(BF16), kb)
        s2 = _dot_tb((q * m2).astype(BF16), kb)
        e1 = jnp.exp(s1 - jnp.max(s1, axis=-1, keepdims=True))
        e2 = jnp.exp(s2 - jnp.max(s2, axis=-1, keepdims=True))
        r1 = 1.0 / jnp.sum(e1, axis=-1, keepdims=True)
        r2 = lam / jnp.sum(e2, axis=-1, keepdims=True)
        a = e1 * r1 - e2 * r2
        o = jnp.dot(a.astype(BF16), v_scr[0:n_keys, :], preferred_element_type=F32)
        o = o * lax.rsqrt(jnp.mean(o * o, axis=-1, keepdims=True) + LN_EPS)
        o_ref[...] = o * (gsub_ref[...] * post_scale)

    if tile0 == 0:
        pl.when(t == 0)(lambda: attend(CTX_LEN))
        pl.when(t > 0)(lambda: attend(T_ALL))
    else:
        attend(T_ALL)


def _attention(z, lam, gsub, post_scale, latent_only):
    tile0 = 1 if latent_only else 0
    n_qt = TILES_PER_BATCH - tile0
    kern = functools.partial(_attn_kernel, post_scale=post_scale, tile0=tile0)
    kv = lambda col0: pl.BlockSpec((T_ALL, LANES), lambda b, h, t, lam_r: (b, col0 + h))
    return pl.pallas_call(
        kern,
        grid_spec=pltpu.PrefetchScalarGridSpec(
            num_scalar_prefetch=1,
            grid=(BATCH, H_C, n_qt),
            in_specs=[pl.BlockSpec((TQ, LANES), lambda b, h, t, lam_r: (b * TILES_PER_BATCH + tile0 + t, h)),
                      kv(H_C), kv(2 * H_C),
                      pl.BlockSpec((1, LANES), lambda b, h, t, lam_r: (0, h))],
            out_specs=pl.BlockSpec((TQ, LANES), lambda b, h, t, lam_r: (b * n_qt + t, h)),
            scratch_shapes=[pltpu.VMEM((T_ALL, LANES), BF16), pltpu.VMEM((T_ALL, LANES), BF16)],
        ),
        out_shape=jax.ShapeDtypeStruct((BATCH * n_qt * TQ, H_C * DV_C), F32),
        compiler_params=_cparams(("arbitrary", "arbitrary", "arbitrary")),
        name="diff_attention",
    )(lam, z, z, z, gsub)


def _layer_norm(r, g, b):
    mu = jnp.mean(r, axis=-1, keepdims=True)
    d = r - mu
    var = jnp.mean(d * d, axis=-1, keepdims=True)
    return d * lax.rsqrt(var + LN_EPS) * g + b


def _route_tile(u, w_ref, b_ref, mi_ref, mf_ref, cnt_ref, carry):
    @pl.when(pl.program_id(0) == 0)
    def _():
        carry[...] = jnp.zeros_like(carry)

    logits = _dot_split(u, w_ref[...]) + b_ref[...]
    lane = lax.broadcasted_iota(jnp.int32, (TM, LANES), 1)
    lane_f = lane.astype(F32)
    neg = -jnp.inf
    big = 1e9

    gmask = lane < N_GROUPS
    gl = jnp.where(gmask, logits, neg)
    gmax = jnp.max(gl, axis=-1, keepdims=True)
    gidx = jnp.min(jnp.where(gl == gmax, lane_f, big), axis=-1, keepdims=True)
    gw = 1.0 / jnp.sum(jnp.where(gmask, jnp.exp(logits - gmax), 0.0), axis=-1, keepdims=True)

    e_lane = lane - N_GROUPS
    in_grp = (e_lane >= 0) & (e_lane < N_EXPERTS) & ((e_lane >> 3) == gidx.astype(jnp.int32))
    el = jnp.where(in_grp, logits, neg)
    v1 = jnp.max(el, axis=-1, keepdims=True)
    i1 = jnp.min(jnp.where(el == v1, lane_f, big), axis=-1, keepdims=True)
    el2 = jnp.where(lane_f == i1, neg, el)
    v2 = jnp.max(el2, axis=-1, keepdims=True)
    i2 = jnp.min(jnp.where(el2 == v2, lane_f, big), axis=-1, keepdims=True)
    t = jnp.exp(v2 - v1)
    c0 = gw / (1.0 + t)
    c1 = gw * t / (1.0 + t)
    e0 = i1 - N_GROUPS
    e1 = i2 - N_GROUPS

    oh0 = lane_f == e0
    oh1 = lane_f == e1
    cnt = oh0.astype(F32) + oh1.astype(F32)
    ri = lax.broadcasted_iota(jnp.int32, (TM, TM), 0)
    cj = lax.broadcasted_iota(jnp.int32, (TM, TM), 1)
    strict = (cj < ri).astype(BF16)
    before = jnp.dot(strict, cnt.astype(BF16), preferred_element_type=F32) + carry[0:1, :]
    r0 = jnp.sum(jnp.where(oh0, before, 0.0), axis=-1, keepdims=True)
    r1 = jnp.sum(jnp.where(oh1, before, 0.0), axis=-1, keepdims=True)
    carry[0:1, :] = carry[0:1, :] + jnp.sum(cnt, axis=0, keepdims=True)

    mi = jnp.where(lane == 0, e0, jnp.where(lane == 1, e1, jnp.where(lane == 2, r0, jnp.where(lane == 3, r1, 0.0))))
    mi_ref[...] = mi.astype(jnp.int32)
    mf_ref[...] = jnp.where(lane == 0, c0, jnp.where(lane == 1, c1, 0.0))
    cnt_ref[...] = carry[...]


def _proj_ln_kernel(y1_ref, y2_ref, w_ref, x_ref, mod_ref, g_ref, b_ref, wr_ref, br_ref,
                    o_ref, mi_ref, mf_ref, cnt_ref, carry):
    heads = jnp.concatenate([y1_ref[...], y2_ref[...]], axis=1).astype(BF16)
    y = jnp.dot(heads, w_ref[...], preferred_element_type=F32)
    r = DEEPNORM_ALPHA * x_ref[...] + mod_ref[0, 2:3, :] * y
    x1 = _layer_norm(r, g_ref[...], b_ref[...])
    o_ref[...] = x1
    u = x1 * (1.0 + mod_ref[0, 4:5, :]) + mod_ref[0, 3:4, :]
    _route_tile(u, wr_ref, br_ref, mi_ref, mf_ref, cnt_ref, carry)


def _proj_ln(y1, y2, col2, w_out_bf16, x, mod_l, ln_g, ln_b, w_route, b_route, latent_only):
    half = D_MODEL // 2
    rt = _row_tile(latent_only)
    mr = _mod_row(latent_only)
    n_tiles = _n_tiles(latent_only)
    n_tok = n_tiles * TM
    row_blk = lambda w: pl.BlockSpec((TM, w), lambda i: (i, 0))
    const = lambda shape: pl.BlockSpec(shape, lambda i: (0, 0))
    return pl.pallas_call(
        _proj_ln_kernel,
        grid=(n_tiles,),
        in_specs=[
            pl.BlockSpec((TM, half), lambda i: (i, 0)),
            pl.BlockSpec((TM, half), lambda i: (i, col2)),
            const((D_MODEL, D_MODEL)),
            pl.BlockSpec((TM, D_MODEL), lambda i: (rt(i), 0)),
            pl.BlockSpec((1, 6, D_MODEL), lambda i: (mr(i), 0, 0)),
            const((1, D_MODEL)), const((1, D_MODEL)), const((D_MODEL, LANES)), const((1, LANES)),
        ],
        out_specs=[row_blk(D_MODEL), row_blk(LANES), row_blk(LANES), const((8, LANES))],
        out_shape=[jax.ShapeDtypeStruct((n_tok, D_MODEL), F32),
                   jax.ShapeDtypeStruct((n_tok, LANES), jnp.int32),
                   jax.ShapeDtypeStruct((n_tok, LANES), F32),
                   jax.ShapeDtypeStruct((8, LANES), F32)],
        scratch_shapes=[pltpu.VMEM((8, LANES), F32)],
        compiler_params=_cparams(("arbitrary",)),
        name="proj_ln_route",
    )(y1, y2, w_out_bf16, x, mod_l, ln_g, ln_b, w_route, b_route)


def _pack_rows(x):
    half = x.shape[-1] // 2
    bits = lambda t: lax.bitcast_convert_type(t.astype(BF16).astype(F32), jnp.uint32)
    return (bits(x[:, :half]) >> 16) | (bits(x[:, half:]) & jnp.uint32(0xFFFF0000))


def _unpack_rows(w):
    lo = lax.bitcast_convert_type(w << 16, F32)
    hi = lax.bitcast_convert_type(w & jnp.uint32(0xFFFF0000), F32)
    return jnp.concatenate([lo, hi], axis=-1)


def _each_row(fn):
    def body(g, carry):
        for j in range(ROW_UNROLL):
            for k in range(2):
                fn(g, j, k)
        return carry
    lax.fori_loop(0, TM // ROW_UNROLL, body, 0)


def _dispatch_kernel(pad_end_ref, padded_ref, x_ref, mod_ref, dest_hbm, xs_hbm,
                     idx_smem, u_scr, zero_scr, sem_idx, sem_row, sem_zero):
    i = pl.program_id(0)
    slot = i % 2
    per_tile = 2 * TM

    @pl.when(i == 0)
    def _():
        zero_scr[...] = jnp.zeros_like(zero_scr)

        def zero_block(first_row):
            rows = pl.ds(pl.multiple_of(first_row, MOE_ROWS), MOE_ROWS)
            return pltpu.make_async_copy(zero_scr, xs_hbm.at[rows], sem_zero)

        n_rows = xs_hbm.shape[0]
        total = pad_end_ref[N_EXPERTS - 1]
        for e in range(N_EXPERTS):
            pl.when(padded_ref[e] > 0)(lambda e=e: zero_block(pad_end_ref[e] - MOE_ROWS).start())
            pl.when(total + e * MOE_ROWS < n_rows)(lambda e=e: zero_block(total + e * MOE_ROWS).start())
        for e in range(N_EXPERTS):
            pl.when(padded_ref[e] > 0)(lambda e=e: zero_block(0).wait())
            pl.when(total + e * MOE_ROWS < n_rows)(lambda e=e: zero_block(0).wait())

    def idx_copy(tile, sl):
        return pltpu.make_async_copy(dest_hbm.at[pl.ds(tile * per_tile, per_tile)],
                                     idx_smem.at[pl.ds(sl * per_tile, per_tile)], sem_idx)

    pl.when(i == 0)(lambda: idx_copy(0, 0).start())
    idx_copy(i, slot).wait()
    pl.when(i + 1 < pl.num_programs(0))(lambda: idx_copy(i + 1, 1 - slot).start())
    u = x_ref[...] * (1.0 + mod_ref[0, 4:5, :]) + mod_ref[0, 3:4, :]
    u_scr[slot] = _pack_rows(u).reshape(TM // ROW_UNROLL, ROW_UNROLL, PACKED)

    def row_copy(sl, g, j, dst_row):
        return pltpu.make_async_copy(u_scr.at[sl, g, pl.ds(j, 1)], xs_hbm.at[pl.ds(dst_row, 1)], sem_row.at[sl])

    def dest_of(sl, g, j, k):
        return idx_smem[sl * per_tile + g * (2 * ROW_UNROLL) + (2 * j + k)]

    _each_row(lambda g, j, k: row_copy(slot, g, j, dest_of(slot, g, j, k)).start())

    @pl.when(i > 0)
    def _():
        _each_row(lambda g, j, k: row_copy(1 - slot, g, j, 0).wait())

    @pl.when(i == pl.num_programs(0) - 1)
    def _():
        _each_row(lambda g, j, k: row_copy(slot, g, j, 0).wait())


def _dispatch(pad_end, padded, x1, mod_l, dest, n_blocks, latent_only):
    mr = _mod_row(latent_only)
    return pl.pallas_call(
        _dispatch_kernel,
        grid_spec=pltpu.PrefetchScalarGridSpec(
            num_scalar_prefetch=2,
            grid=(_n_tiles(latent_only),),
            in_specs=[
                pl.BlockSpec((TM, D_MODEL), lambda i, pe, pd: (i, 0)),
                pl.BlockSpec((1, 6, D_MODEL), lambda i, pe, pd: (mr(i), 0, 0)),
                pl.BlockSpec(memory_space=pl.ANY),
            ],
            out_specs=pl.BlockSpec(memory_space=pl.ANY),
            scratch_shapes=[
                pltpu.SMEM((2 * 2 * TM,), jnp.int32),
                pltpu.VMEM((2, TM // ROW_UNROLL, ROW_UNROLL, PACKED), jnp.uint32),
                pltpu.VMEM((MOE_ROWS, PACKED), jnp.uint32),
                pltpu.SemaphoreType.DMA(()),
                pltpu.SemaphoreType.DMA((2,)),
                pltpu.SemaphoreType.DMA(()),
            ],
        ),
        out_shape=jax.ShapeDtypeStruct((n_blocks * MOE_ROWS, PACKED), jnp.uint32),
        compiler_params=_cparams(("arbitrary",)),
        name="moe_dispatch",
    )(pad_end, padded, x1, mod_l, dest)


def _expert_kernel(blk_exp_ref, n_used_ref, x_ref, wg_ref, wu_ref, wd_ref, o_ref):
    used = pl.program_id(0) < n_used_ref[0]

    @pl.when(used)
    def _():
        x = _unpack_rows(x_ref[...]).astype(BF16)
        gate = jnp.dot(x, wg_ref[0, 0].astype(BF16), preferred_element_type=F32)
        up = jnp.dot(x, wu_ref[0, 0].astype(BF16), preferred_element_type=F32)
        hid = (_silu(gate) * up).astype(BF16)
        o_ref[...] = _pack_rows(jnp.dot(hid, wd_ref[0, 0].astype(BF16), preferred_element_type=F32))

    @pl.when(jnp.logical_not(used))
    def _():
        o_ref[...] = jnp.zeros_like(o_ref)


def _experts(blk_exp, n_used, xs, layer, w_gate, w_up, w_down):
    n_blocks = xs.shape[0] // MOE_ROWS
    row_in = pl.BlockSpec((MOE_ROWS, PACKED), lambda i, be, nu: (jnp.minimum(i, nu[0] - 1), 0))
    w_in = pl.BlockSpec((1, 1, D_MODEL, D_EXPERT), lambda i, be, nu: (layer, be[i], 0, 0))
    w_out = pl.BlockSpec((1, 1, D_EXPERT, D_MODEL), lambda i, be, nu: (layer, be[i], 0, 0))
    return pl.pallas_call(
        _expert_kernel,
        grid_spec=pltpu.PrefetchScalarGridSpec(
            num_scalar_prefetch=2,
            grid=(n_blocks,),
            in_specs=[row_in, w_in, w_in, w_out],
            out_specs=pl.BlockSpec((MOE_ROWS, PACKED), lambda i, be, nu: (i, 0)),
        ),
        out_shape=jax.ShapeDtypeStruct(xs.shape, jnp.uint32),
        compiler_params=_cparams(("arbitrary",)),
        name="moe_experts",
    )(blk_exp, n_used, xs, w_gate, w_up, w_down)


def _combine_ln_kernel(ys_hbm, dest_hbm, mf_ref, x_ref, mod_ref, g_ref, b_ref, o_ref,
                       idx_smem, y_buf, sem_idx, sem_row):
    i = pl.program_id(0)
    slot = i % 2
    per_tile = 2 * TM

    def row_copy(sl, g, j, k, src_row):
        return pltpu.make_async_copy(ys_hbm.at[pl.ds(src_row, 1)], y_buf.at[sl, k, g, pl.ds(j, 1)], sem_row.at[sl])

    def idx_copy(tile, sl):
        return pltpu.make_async_copy(dest_hbm.at[pl.ds(tile * per_tile, per_tile)],
                                     idx_smem.at[pl.ds(sl * per_tile, per_tile)], sem_idx)

    def request(sl):
        _each_row(lambda g, j, k: row_copy(
            sl, g, j, k, idx_smem[sl * per_tile + g * (2 * ROW_UNROLL) + (2 * j + k)]).start())

    n = pl.num_programs(0)

    @pl.when(i == 0)
    def _():
        idx_copy(0, 0).start()
        idx_copy(0, 0).wait()
        request(0)
        pl.when(n > 1)(lambda: idx_copy(1, 1).start())

    @pl.when(i + 1 < n)
    def _():
        idx_copy(i + 1, 1 - slot).wait()
        request(1 - slot)
        pl.when(i + 2 < n)(lambda: idx_copy(i + 2, slot).start())

    _each_row(lambda g, j, k: row_copy(slot, g, j, k, 0).wait())

    mf = mf_ref[...]
    y0 = _unpack_rows(y_buf[slot, 0].reshape(TM, PACKED))
    y1 = _unpack_rows(y_buf[slot, 1].reshape(TM, PACKED))
    y = mf[:, 0:1] * y0 + mf[:, 1:2] * y1
    r = DEEPNORM_ALPHA * x_ref[...] + mod_ref[0, 5:6, :] * y
    o_ref[...] = _layer_norm(r, g_ref[...], b_ref[...])


def _combine_ln(ys, dest, mf, x1, mod_l, ln_g, ln_b, latent_only):
    mr = _mod_row(latent_only)
    n_tiles = _n_tiles(latent_only)
    return pl.pallas_call(
        _combine_ln_kernel,
        grid=(n_tiles,),
        in_specs=[
            pl.BlockSpec(memory_space=pl.ANY),
            pl.BlockSpec(memory_space=pl.ANY),
            pl.BlockSpec((TM, LANES), lambda i: (i, 0)),
            pl.BlockSpec((TM, D_MODEL), lambda i: (i, 0)),
            pl.BlockSpec((1, 6, D_MODEL), lambda i: (mr(i), 0, 0)),
            pl.BlockSpec((1, D_MODEL), lambda i: (0, 0)),
            pl.BlockSpec((1, D_MODEL), lambda i: (0, 0)),
        ],
        out_specs=pl.BlockSpec((TM, D_MODEL), lambda i: (i, 0)),
        out_shape=jax.ShapeDtypeStruct((n_tiles * TM, D_MODEL), F32),
        scratch_shapes=[
            pltpu.SMEM((2 * 2 * TM,), jnp.int32),
            pltpu.VMEM((2, 2, TM // ROW_UNROLL, ROW_UNROLL, PACKED), jnp.uint32),
            pltpu.SemaphoreType.DMA(()),
            pltpu.SemaphoreType.DMA((2,)),
        ],
        compiler_params=_cparams(("arbitrary",)),
        name="combine_ln",
    )(ys, dest, mf, x1, mod_l, ln_g, ln_b)


def _route_params(w_grp, b_grp, w_rexp, b_rexp):
    pad = LANES - N_GROUPS - N_EXPERTS
    w_route = jnp.concatenate([w_grp, w_rexp, jnp.zeros((D_MODEL, pad), F32)], axis=1)
    b_route = jnp.concatenate([b_grp, b_rexp, jnp.zeros((pad,), F32)])[None, :]
    return w_route, b_route


def _moe(x1, mi, mf, cnt, mod_l, layer, w_gate, w_up, w_down, ln_g, ln_b, latent_only):
    counts = cnt[0, :N_EXPERTS].astype(jnp.int32)
    padded = (counts + MOE_ROWS - 1) // MOE_ROWS * MOE_ROWS
    pad_end = jnp.cumsum(padded)
    pad_start = pad_end - padded
    n_tok = x1.shape[0]
    n_blocks = (2 * n_tok) // MOE_ROWS + N_EXPERTS
    experts = jnp.arange(N_EXPERTS, dtype=jnp.int32)
    start_of = jnp.sum(jnp.where(mi[:, 0:2, None] == experts, pad_start, 0), axis=-1)
    dest = (start_of + mi[:, 2:4]).reshape(-1)
    blk_start = jnp.arange(n_blocks, dtype=jnp.int32) * MOE_ROWS
    blk_exp = jnp.minimum(jnp.sum((pad_end[None, :] <= blk_start[:, None]).astype(jnp.int32), axis=1),
                          N_EXPERTS - 1)
    n_used = pad_end[-1:] // MOE_ROWS
    xs = _dispatch(pad_end, padded, x1, mod_l, dest, n_blocks, latent_only)
    ys = _experts(blk_exp, n_used, xs, layer, w_gate, w_up, w_down)
    return _combine_ln(ys, dest, mf, x1, mod_l, ln_g, ln_b, latent_only)


def kernel(x, c, ctx, c_ctx, ada_w, ada_b, ln1_g, ln1_b, ln2_g, ln2_b, ab_w_in, ab_w_lr_f, ab_b_lr_f, ab_w_lr_b, ab_b_lr_b, ab_gn_a, ab_gn_b, ab_w_out, c_w_qkv, c_lq1, c_lk1, c_lq2, c_lk2, c_subln_g, c_w_out, moe_w_grp, moe_b_grp, moe_w_rexp, moe_b_rexp, moe_w_gate, moe_w_up, moe_w_down):
    assert x.shape == (BATCH, SEQ, D_MODEL) and ctx.shape == (BATCH, CTX_LEN, D_MODEL)
    xs = jnp.concatenate([ctx, x], axis=1).reshape(N_ALL, D_MODEL)
    c_all = jnp.concatenate([c, c_ctx[None, :], jnp.zeros((16 - BATCH - 1, D_MODEL), F32)], axis=0)
    mod = _ada_tables(c_all, ada_w, ada_b).reshape(DEPTH, 16, 6, D_MODEL)

    rope_a = _rope_tables(DK_A, 1)
    rope_c = _rope_tables(DH_C, 2)
    dec_a = _retention_tables()

    for l in range(DEPTH):
        last = l == DEPTH - 1
        i = l // 2
        mod_l = mod[l]
        row = lambda v: v[None, :]
        route = _route_params(moe_w_grp[l], moe_b_grp[l], moe_w_rexp[l], moe_b_rexp[l])
        if l % 2 == 0:
            assert not last
            w_in = jnp.pad(ab_w_in[i], ((0, 0), (0, AB_COLS - ab_w_in.shape[2]))).astype(BF16)
            z = _mod_matmul(xs, mod_l, w_in, *rope_a, 2 * H_A, DK_A // 4)
            ya = _scan_a(z, dec_a, row(ab_gn_a[i]))
            wf = ab_w_lr_f[i].reshape(GLA_RANK, H_B // 2, LANES)
            wb = ab_w_lr_b[i].reshape(GLA_RANK, H_B // 2, LANES)
            wlr = jnp.zeros((H_B // 2, LANES, 2 * LANES), F32)
            wlr = wlr.at[:, 0:GLA_RANK, 0:LANES].set(jnp.swapaxes(wf, 0, 1))
            wlr = wlr.at[:, GLA_RANK:2 * GLA_RANK, LANES:].set(jnp.swapaxes(wb, 0, 1))
            blr = jnp.concatenate([ab_b_lr_f[i].reshape(H_B // 2, 1, LANES),
                                   ab_b_lr_b[i].reshape(H_B // 2, 1, LANES)], axis=-1)
            yb = _scan_b(z, wlr, blr, row(ab_gn_b[i]))
            x1, mi, mf, cnt = _proj_ln(ya, yb, 0, ab_w_out[i].astype(BF16), xs, mod_l, row(ln1_g[l]),
                                       row(ln1_b[l]), *route, False)
        else:
            lam_init = 0.8 - 0.6 * math.exp(-0.3 * l)
            lam = (jnp.exp(jnp.sum(c_lq1[i] * c_lk1[i], axis=-1))
                   - jnp.exp(jnp.sum(c_lq2[i] * c_lk2[i], axis=-1))).astype(F32) + lam_init
            z = _mod_matmul(xs, mod_l, c_w_qkv[i].astype(BF16), *rope_c, 2 * H_C, DH_C // 4)
            gsub = row(c_subln_g[i])
            y = _attention(z, lam, gsub, 1.0 - lam_init, last)
            x1, mi, mf, cnt = _proj_ln(y, y, 1, c_w_out[i].astype(BF16), xs, mod_l, row(ln1_g[l]),
                                       row(ln1_b[l]), *route, last)
        xs = _moe(x1, mi, mf, cnt, mod_l, l, moe_w_gate, moe_w_up, moe_w_down, row(ln2_g[l]), row(ln2_b[l]), last)
    return xs.reshape(BATCH, SEQ, D_MODEL)
```

```python
import functools
import math

import numpy as np
import jax
import jax.numpy as jnp
from jax import lax
from jax.experimental import pallas as pl
from jax.experimental.pallas import tpu as pltpu

F32 = jnp.float32
BF16 = jnp.bfloat16

D_MODEL = 1024
BATCH = 8
SEQ = 2048
DEPTH = 4
GRID_W = 64
CTX_LEN = 256
ROPE_BASE = 10000.0
LN_EPS = 1e-5
DEEPNORM_ALPHA = (2 * DEPTH) ** 0.25
H_A = 4
DK_A = 128
DV_A = 128
CHUNK_A = 128
RET_EXP_FWD = 5.0
RET_EXP_BWD = 5.5
H_B = 4
DK_B = 64
DV_B = 128
GLA_RANK = 16
GLA_TAU = 16.0
CHUNK_B = 64
H_C = 8
DH_C = 64
DV_C = 128
N_GROUPS = 4
EXPERTS_PER_GROUP = 8
N_EXPERTS = 32
D_EXPERT = 512

LANES = 128
T_ALL = CTX_LEN + SEQ
N_ALL = BATCH * T_ALL
TM = 256
TILES_PER_BATCH = T_ALL // TM
LATENT_TILES_PER_BATCH = SEQ // TM
AB_COLS = 29 * LANES
MOE_ROWS = 512
PACKED = D_MODEL // 2
ROW_UNROLL = 8
SCAN_UNROLL = 8
GLA_GROUP = 256
TQ = 256
ATTN_KEYS = 256
VMEM_LIMIT = 56 * 1024 * 1024


def _cparams(sem):
    return pltpu.CompilerParams(dimension_semantics=sem, vmem_limit_bytes=VMEM_LIMIT)


def _silu(v):
    return v * (1.0 / (1.0 + jnp.exp(-v)))


def _n_tiles(latent_only):
    return BATCH * (LATENT_TILES_PER_BATCH if latent_only else TILES_PER_BATCH)


def _row_tile(latent_only):
    if latent_only:
        return lambda i: (i // LATENT_TILES_PER_BATCH) * TILES_PER_BATCH + 1 + i % LATENT_TILES_PER_BATCH
    return lambda i: i


def _mod_row(latent_only):
    if latent_only:
        return lambda i: i // LATENT_TILES_PER_BATCH
    return lambda i: jnp.where(i % TILES_PER_BATCH == 0, BATCH, i // TILES_PER_BATCH)


def _ada_kernel(c_ref, w_ref, b_ref, o_ref):
    sc = _silu(c_ref[...])
    o_ref[0] = jnp.dot(sc.astype(BF16), w_ref[0].astype(BF16), preferred_element_type=F32) + b_ref[0]


def _ada_tables(c_all, ada_w, ada_b):
    tn = 1536
    n_out = 6 * D_MODEL
    return pl.pallas_call(
        _ada_kernel,
        grid=(DEPTH, n_out // tn),
        in_specs=[
            pl.BlockSpec((16, D_MODEL), lambda l, j: (0, 0)),
            pl.BlockSpec((1, D_MODEL, tn), lambda l, j: (l, 0, j)),
            pl.BlockSpec((1, 1, tn), lambda l, j: (l, 0, j)),
        ],
        out_specs=pl.BlockSpec((1, 16, tn), lambda l, j: (l, 0, j)),
        out_shape=jax.ShapeDtypeStruct((DEPTH, 16, n_out), F32),
        compiler_params=_cparams(("arbitrary", "arbitrary")),
        name="ada_tables",
    )(c_all, ada_w, ada_b.reshape(DEPTH, 1, n_out))


def _modmm_kernel(x_ref, mod_ref, w_ref, cos_ref, sup_ref, sdn_ref, o_ref, *, rope_blocks, quarter):
    u = x_ref[...] * (1.0 + mod_ref[0, 1:2, :]) + mod_ref[0, 0:1, :]
    z = jnp.dot(u.astype(BF16), w_ref[...], preferred_element_type=F32)
    cos, sup, sdn = cos_ref[...], sup_ref[...], sdn_ref[...]
    for c in range(rope_blocks):
        cols = slice(c * LANES, (c + 1) * LANES)
        o_ref[:, cols] = _rope(z[:, cols], cos, sup, sdn, quarter)
    o_ref[:, rope_blocks * LANES:] = z[:, rope_blocks * LANES:]


def _mod_matmul(x, mod_l, w_bf16, cos, sup, sdn, rope_blocks, quarter):
    n_out = w_bf16.shape[1]
    tbl = pl.BlockSpec((TM, LANES), lambda i: (i % TILES_PER_BATCH, 0))
    return pl.pallas_call(
        functools.partial(_modmm_kernel, rope_blocks=rope_blocks, quarter=quarter),
        grid=(N_ALL // TM,),
        in_specs=[
            pl.BlockSpec((TM, D_MODEL), lambda i: (i, 0)),
            pl.BlockSpec((1, 6, D_MODEL), lambda i: (_mod_row(False)(i), 0, 0)),
            pl.BlockSpec((D_MODEL, n_out), lambda i: (0, 0)),
            tbl, tbl, tbl,
        ],
        out_specs=pl.BlockSpec((TM, n_out), lambda i: (i, 0)),
        out_shape=jax.ShapeDtypeStruct((N_ALL, n_out), F32),
        compiler_params=_cparams(("arbitrary",)),
        name="mod_matmul",
    )(x, mod_l, w_bf16, cos, sup, sdn)


def _rope_tables(head_dim, reps):
    rows = SEQ // GRID_W
    row = np.repeat(np.arange(rows, dtype=np.float32), GRID_W)
    col = np.tile(np.arange(GRID_W, dtype=np.float32), rows)
    quarter = head_dim // 4
    inv = (ROPE_BASE ** (-np.arange(quarter, dtype=np.float32) / quarter)).astype(np.float32)
    ang_r = row[:, None] * inv
    ang_c = col[:, None] * inv
    ang = np.concatenate([ang_r, ang_r, ang_c, ang_c], axis=-1)
    cos = np.cos(ang).astype(np.float32)
    sin = np.sin(ang).astype(np.float32)
    q_idx = (np.arange(head_dim) // quarter) % 2
    sin_up = np.where(q_idx == 1, sin, 0.0).astype(np.float32)
    sin_dn = np.where(q_idx == 0, -sin, 0.0).astype(np.float32)

    def full(t, ctx_val):
        t = np.tile(t, (1, reps))
        return jnp.asarray(np.concatenate([np.full((CTX_LEN, t.shape[1]), ctx_val, np.float32), t], axis=0))

    return full(cos, 1.0), full(sin_up, 0.0), full(sin_dn, 0.0)


def _rope(x, cos, sin_up, sin_dn, quarter):
    width = x.shape[-1]
    return x * cos + pltpu.roll(x, quarter, 1) * sin_up + pltpu.roll(x, width - quarter, 1) * sin_dn


def _dot_tb(a, b):
    return lax.dot_general(a, b, (((1,), (1,)), ((), ())), preferred_element_type=F32)


def _dot_ta(a, b):
    return lax.dot_general(a, b, (((0,), (0,)), ((), ())), preferred_element_type=F32)


def _split_bf16(x, parts):
    out = []
    for _ in range(parts):
        t = x.astype(BF16)
        out.append(t)
        x = x - t.astype(F32)
    return out


def _dot_split(a, b):
    a_hi, a_lo = _split_bf16(a, 2)
    b_hi, b_lo = _split_bf16(b, 2)
    dot = lambda u, v: jnp.dot(u, v, preferred_element_type=F32)
    n = b.shape[1]
    both = dot(a_hi, jnp.concatenate([b_hi, b_lo], axis=1))
    return both[:, :n] + (both[:, n:] + dot(a_lo, b_hi))


def _dot_mask(mask_bf16, x):
    return sum(jnp.dot(mask_bf16, t, preferred_element_type=F32) for t in reversed(_split_bf16(x, 3)))


def _retention_tables():
    c = CHUNK_A
    i = np.arange(c, dtype=np.float64)
    out = np.zeros((H_A, 7, c, LANES), np.float64)
    for h in range(H_A):
        lgf = np.log1p(-np.exp2(-(RET_EXP_FWD + h)))
        lgb = np.log1p(-np.exp2(-(RET_EXP_BWD + h)))
        d = i[:, None] - i[None, :]
        out[h, 0] = np.where(d >= 0, np.exp(lgf * d), np.exp(lgb * (-d - 1)))
        out[h, 1] = np.exp(lgf * (i + 1))[:, None]
        out[h, 2] = np.exp(lgb * (c - 1 - i))[:, None]
        out[h, 3] = np.exp(lgf * (c - 1 - i))[:, None]
        out[h, 4] = np.exp(lgb * i)[:, None]
        out[h, 5] = np.exp(lgf * c)
        out[h, 6] = np.exp(lgb * c)
    return jnp.asarray(out.astype(np.float32))


def _scan_a_kernel(q_ref, k_ref, v_ref, g_ref, dec_ref, gn_ref, o_ref, sb_scr):
    c = CHUNK_A
    n_ctx = CTX_LEN // c
    n_all = T_ALL // c
    scale = DK_A ** -0.5
    dmat = dec_ref[0, 0]
    q_f, q_b, k_f, k_b = dec_ref[0, 1], dec_ref[0, 2], dec_ref[0, 3], dec_ref[0, 4]
    g_fc, g_bc = dec_ref[0, 5], dec_ref[0, 6]
    gn = gn_ref[...]
    zero = jnp.zeros((DK_A, DV_A), F32)

    def chunk(ci):
        return pl.ds(pl.multiple_of(ci * c, c), c)

    def kv_state(ci, k_dec):
        sl = chunk(ci)
        return _dot_ta((k_ref[sl, :] * k_dec).astype(BF16), v_ref[sl, :].astype(BF16))

    def run(lo, hi, sf0, sb0):
        def bwd(j, sb):
            ci = hi - 1 - j
            sb_scr[ci] = sb
            return g_bc * sb + kv_state(ci, k_b)

        sb_fin = lax.fori_loop(0, hi - lo, bwd, sb0, unroll=SCAN_UNROLL)

        def fwd(j, sf):
            ci = lo + j
            sl = chunk(ci)
            q = q_ref[sl, :] * scale
            k = k_ref[sl, :]
            vb = v_ref[sl, :].astype(BF16)
            att = _dot_tb(q.astype(BF16), k.astype(BF16)) * dmat
            o = jnp.dot(att.astype(BF16), vb, preferred_element_type=F32)
            o = o + jnp.dot((q * q_f).astype(BF16), sf.astype(BF16), preferred_element_type=F32)
            o = o + jnp.dot((q * q_b).astype(BF16), sb_scr[ci].astype(BF16), preferred_element_type=F32)
            o = o - jnp.mean(o, axis=-1, keepdims=True)
            o = o * lax.rsqrt(jnp.mean(o * o, axis=-1, keepdims=True) + LN_EPS)
            o_ref[sl, :] = _silu(g_ref[sl, :]) * (o * gn)
            return g_fc * sf + _dot_ta((k * k_f).astype(BF16), vb)

        sf_fin = lax.fori_loop(0, hi - lo, fwd, sf0, unroll=SCAN_UNROLL)
        return sf_fin, sb_fin

    sf_c, sb_c = run(0, n_ctx, zero, zero)
    run(n_ctx, n_all, sf_c, sb_c)


def _scan_a(z, dec, gn_a):
    blk = lambda col0: pl.BlockSpec((T_ALL, LANES), lambda b, h: (b, col0 + h))
    return pl.pallas_call(
        _scan_a_kernel,
        grid=(BATCH, H_A),
        in_specs=[blk(0), blk(4), blk(8), blk(12),
                  pl.BlockSpec((1, 7, CHUNK_A, LANES), lambda b, h: (h, 0, 0, 0)),
                  pl.BlockSpec((1, LANES), lambda b, h: (0, h))],
        out_specs=pl.BlockSpec((T_ALL, LANES), lambda b, h: (b, h)),
        out_shape=jax.ShapeDtypeStruct((N_ALL, H_A * DV_A), F32),
        scratch_shapes=[pltpu.VMEM((T_ALL // CHUNK_A, DK_A, DV_A), F32)],
        compiler_params=_cparams(("arbitrary", "arbitrary")),
        name="scan_retention",
    )(z, z, z, z, dec, gn_a)


def _log_sigmoid(g):
    return jnp.minimum(g, 0.0) - jnp.log1p(jnp.exp(-jnp.abs(g)))


def _scan_b_kernel(q_ref, k_ref, v_ref, g_ref, lr_ref, wlr_ref, blr_ref, gn_ref, o_ref,
                   qf_scr, kf_scr, qb_scr, kb_scr, ktf_scr, ktb_scr, ef_scr, eb_scr, sb_scr):
    c = CHUNK_B
    n_ctx = CTX_LEN // c
    n_all = T_ALL // c
    per_group = GLA_GROUP // c
    scale = DK_B ** -0.5

    gi_r = lax.broadcasted_iota(jnp.int32, (GLA_GROUP, GLA_GROUP), 0)
    gi_c = lax.broadcasted_iota(jnp.int32, (GLA_GROUP, GLA_GROUP), 1)
    same_chunk = (gi_r // c) == (gi_c // c)
    prefix = (same_chunk & (gi_c <= gi_r)).astype(BF16)
    suffix = (same_chunk & (gi_c >= gi_r)).astype(BF16)

    def prepare(gi, carry):
        sl = pl.ds(pl.multiple_of(gi * GLA_GROUP, GLA_GROUP), GLA_GROUP)
        gates = _dot_split(lr_ref[sl, :], wlr_ref[0]) + blr_ref[0]
        laf = _log_sigmoid(gates[:, :LANES]) * (1.0 / GLA_TAU)
        lab = _log_sigmoid(gates[:, LANES:]) * (1.0 / GLA_TAU)
        b = _dot_mask(prefix, laf)
        rb = _dot_mask(suffix, lab)
        q = q_ref[sl, :] * scale
        k = k_ref[sl, :]
        qf_scr[sl, :] = (q * jnp.exp(b)).astype(BF16)
        kf_scr[sl, :] = (k * jnp.exp(-b)).astype(BF16)
        qb_scr[sl, :] = (q * jnp.exp(rb - lab)).astype(BF16)
        kb_scr[sl, :] = (k * jnp.exp(-rb)).astype(BF16)
        b3 = b.reshape(per_group, c, LANES)
        rb3 = rb.reshape(per_group, c, LANES)
        k3 = k.reshape(per_group, c, LANES)
        b_tot = b3[:, c - 1:c, :]
        rb_tot = rb3[:, 0:1, :]
        ktf_scr[sl, :] = (k3 * jnp.exp(b_tot - b3)).reshape(GLA_GROUP, LANES).astype(BF16)
        ktb_scr[sl, :] = (k3 * jnp.exp(rb_tot - rb3)).reshape(GLA_GROUP, LANES).astype(BF16)
        for m in range(per_group):
            ef_scr[gi * per_group + m] = jnp.broadcast_to(jnp.exp(b_tot[m]), (8, LANES))
            eb_scr[gi * per_group + m] = jnp.broadcast_to(jnp.exp(rb_tot[m]), (8, LANES))
        return carry

    lax.fori_loop(0, T_ALL // GLA_GROUP, prepare, 0)

    lane = lax.broadcasted_iota(jnp.int32, (1, LANES), 1)
    masks = [lane < DK_B, lane >= DK_B]
    ri = lax.broadcasted_iota(jnp.int32, (c, c), 0)
    cj = lax.broadcasted_iota(jnp.int32, (c, c), 1)
    lower = cj <= ri
    gn = gn_ref[...]
    zero = jnp.zeros((DV_B, LANES), F32)
    zero_b = jnp.zeros((), BF16)

    def chunk(ci):
        return pl.ds(pl.multiple_of(ci * c, c), c)

    def run(lo, hi, sf0, sb0):
        def bwd(j, sb):
            ci = hi - 1 - j
            sl = chunk(ci)
            e_tot = eb_scr[ci][0:1, :]
            kt = ktb_scr[sl, :]
            v = v_ref[sl, :]
            new = []
            for h in range(2):
                sb_scr[ci, h] = sb[h]
                vh = v[:, h * DV_B:(h + 1) * DV_B].astype(BF16)
                new.append(sb[h] * e_tot + _dot_ta(vh, jnp.where(masks[h], kt, zero_b)))
            return tuple(new)

        sb_fin = lax.fori_loop(0, hi - lo, bwd, sb0, unroll=SCAN_UNROLL)

        def fwd(j, sf):
            ci = lo + j
            sl = chunk(ci)
            e_tot = ef_scr[ci][0:1, :]
            qf, kf, qb, kb, kt = qf_scr[sl, :], kf_scr[sl, :], qb_scr[sl, :], kb_scr[sl, :], ktf_scr[sl, :]
            v = v_ref[sl, :]
            g = g_ref[sl, :]
            new = []
            for h in range(2):
                pick = lambda t: jnp.where(masks[h], t, zero_b)
                vh = v[:, h * DV_B:(h + 1) * DV_B].astype(BF16)
                qfh, qbh = pick(qf), pick(qb)
                att = jnp.where(lower, _dot_tb(qfh, pick(kf)), _dot_tb(qbh, pick(kb)))
                o = jnp.dot(att.astype(BF16), vh, preferred_element_type=F32)
                o = o + _dot_tb(qfh, sf[h].astype(BF16))
                o = o + _dot_tb(qbh, sb_scr[ci, h].astype(BF16))
                o = o * lax.rsqrt(jnp.mean(o * o, axis=-1, keepdims=True) + LN_EPS)
                cols = slice(h * DV_B, (h + 1) * DV_B)
                o_ref[sl, cols] = _silu(g[:, cols]) * (o * gn[:, cols])
                new.append(sf[h] * e_tot + _dot_ta(vh, pick(kt)))
            return tuple(new)

        sf_fin = lax.fori_loop(0, hi - lo, fwd, sf0, unroll=SCAN_UNROLL)
        return sf_fin, sb_fin

    sf_c, sb_c = run(0, n_ctx, (zero, zero), (zero, zero))
    run(n_ctx, n_all, sf_c, sb_c)


def _scan_b(z, wlr, blr, gn_b):
    pairs = H_B // 2
    return pl.pallas_call(
        _scan_b_kernel,
        grid=(BATCH, pairs),
        in_specs=[
            pl.BlockSpec((T_ALL, LANES), lambda b, p: (b, 16 + p)),
            pl.BlockSpec((T_ALL, LANES), lambda b, p: (b, 18 + p)),
            pl.BlockSpec((T_ALL, 2 * DV_B), lambda b, p: (b, 10 + p)),
            pl.BlockSpec((T_ALL, 2 * DV_B), lambda b, p: (b, 12 + p)),
            pl.BlockSpec((T_ALL, LANES), lambda b, p: (b, 28)),
            pl.BlockSpec((1, LANES, 2 * LANES), lambda b, p: (p, 0, 0)),
            pl.BlockSpec((1, 1, 2 * LANES), lambda b, p: (p, 0, 0)),
            pl.BlockSpec((1, 2 * DV_B), lambda b, p: (0, p)),
        ],
        out_specs=pl.BlockSpec((T_ALL, 2 * DV_B), lambda b, p: (b, p)),
        out_shape=jax.ShapeDtypeStruct((N_ALL, H_B * DV_B), F32),
        scratch_shapes=[pltpu.VMEM((T_ALL, LANES), BF16)] * 6
                       + [pltpu.VMEM((T_ALL // CHUNK_B, 8, LANES), F32)] * 2
                       + [pltpu.VMEM((T_ALL // CHUNK_B, 2, DV_B, LANES), F32)],
        compiler_params=_cparams(("arbitrary", "arbitrary")),
        name="scan_gla",
    )(z, z, z, z, z, wlr, blr, gn_b)


def _attn_kernel(lam_ref, q_ref, k_ref, v_ref, gsub_ref, o_ref, k_scr, v_scr, *, post_scale, tile0):
    h = pl.program_id(1)
    t = pl.program_id(2)
    scale = DH_C ** -0.5 * math.log2(math.e)

    @pl.when(t == 0)
    def _():
        k_scr[...] = k_ref[...].astype(BF16)
        v_scr[:, 0:LANES] = v_ref[...].astype(BF16)
        v_scr[:, LANES:] = jnp.ones((T_ALL, LANES), BF16)

    lam = lam_ref[h]
    lane = lax.broadcasted_iota(jnp.int32, (1, LANES), 1)
    m1 = (lane < DH_C).astype(F32)
    m2 = (lane >= DH_C).astype(F32)

    def attend(n_keys):
        q = q_ref[...] * scale
        qs = [(q * m1).astype(BF16), (q * m2).astype(BF16)]
        run_max = [jnp.full((TQ, 1), -jnp.inf, F32) for _ in range(2)]
        acc = [jnp.zeros((TQ, 2 * LANES), F32) for _ in range(2)]
        chunk = min(ATTN_KEYS, n_keys)
        for c in range(n_keys // chunk):
            keys = slice(c * chunk, (c + 1) * chunk)
            kb = k_scr[keys, :]
            vb = v_scr[keys, :]
            for i in range(2):
                s = _dot_tb(qs[i], kb)
                new_max = jnp.maximum(run_max[i], jnp.max(s, axis=-1, keepdims=True))
                p = jnp.exp2(s - new_max).astype(BF16)
                acc[i] = acc[i] * jnp.exp2(run_max[i] - new_max) + jnp.dot(p, vb, preferred_element_type=F32)
                run_max[i] = new_max
        o = acc[0][:, :LANES] / acc[0][:, LANES:] - lam * (acc[1][:, :LANES] / acc[1][:, LANES:])
        o = o * lax.rsqrt(jnp.mean(o * o, axis=-1, keepdims=True) + LN_EPS)
        o_ref[...] = o * (gsub_ref[...] * post_scale)

    if tile0 == 0:
        pl.when(t == 0)(lambda: attend(CTX_LEN))
        pl.when(t > 0)(lambda: attend(T_ALL))
    else:
        attend(T_ALL)


def _attention(z, lam, gsub, post_scale, latent_only):
    tile0 = 1 if latent_only else 0
    n_qt = TILES_PER_BATCH - tile0
    kern = functools.partial(_attn_kernel, post_scale=post_scale, tile0=tile0)
    kv = lambda col0: pl.BlockSpec((T_ALL, LANES), lambda b, h, t, lam_r: (b, col0 + h))
    return pl.pallas_call(
        kern,
        grid_spec=pltpu.PrefetchScalarGridSpec(
            num_scalar_prefetch=1,
            grid=(BATCH, H_C, n_qt),
            in_specs=[pl.BlockSpec((TQ, LANES), lambda b, h, t, lam_r: (b * TILES_PER_BATCH + tile0 + t, h)),
                      kv(H_C), kv(2 * H_C),
                      pl.BlockSpec((1, LANES), lambda b, h, t, lam_r: (0, h))],
            out_specs=pl.BlockSpec((TQ, LANES), lambda b, h, t, lam_r: (b * n_qt + t, h)),
            scratch_shapes=[pltpu.VMEM((T_ALL, LANES), BF16), pltpu.VMEM((T_ALL, 2 * LANES), BF16)],
        ),
        out_shape=jax.ShapeDtypeStruct((BATCH * n_qt * TQ, H_C * DV_C), F32),
        compiler_params=_cparams(("arbitrary", "arbitrary", "arbitrary")),
        name="diff_attention",
    )(lam, z, z, z, gsub)


def _layer_norm(r, g, b):
    mu = jnp.mean(r, axis=-1, keepdims=True)
    d = r - mu
    var = jnp.mean(d * d, axis=-1, keepdims=True)
    return d * lax.rsqrt(var + LN_EPS) * g + b


def _route_tile(u, w_ref, b_ref, mi_ref, mf_ref, cnt_ref, carry):
    @pl.when(pl.program_id(0) == 0)
    def _():
        carry[...] = jnp.zeros_like(carry)

    logits = _dot_split(u, w_ref[...]) + b_ref[...]
    lane = lax.broadcasted_iota(jnp.int32, (TM, LANES), 1)
    lane_f = lane.astype(F32)
    neg = -jnp.inf
    big = 1e9

    gmask = lane < N_GROUPS
    gl = jnp.where(gmask, logits, neg)
    gmax = jnp.max(gl, axis=-1, keepdims=True)
    gidx = jnp.min(jnp.where(gl == gmax, lane_f, big), axis=-1, keepdims=True)
    gw = 1.0 / jnp.sum(jnp.where(gmask, jnp.exp(logits - gmax), 0.0), axis=-1, keepdims=True)

    e_lane = lane - N_GROUPS
    in_grp = (e_lane >= 0) & (e_lane < N_EXPERTS) & ((e_lane >> 3) == gidx.astype(jnp.int32))
    el = jnp.where(in_grp, logits, neg)
    v1 = jnp.max(el, axis=-1, keepdims=True)
    i1 = jnp.min(jnp.where(el == v1, lane_f, big), axis=-1, keepdims=True)
    el2 = jnp.where(lane_f == i1, neg, el)
    v2 = jnp.max(el2, axis=-1, keepdims=True)
    i2 = jnp.min(jnp.where(el2 == v2, lane_f, big), axis=-1, keepdims=True)
    t = jnp.exp(v2 - v1)
    c0 = gw / (1.0 + t)
    c1 = gw * t / (1.0 + t)
    e0 = i1 - N_GROUPS
    e1 = i2 - N_GROUPS

    oh0 = lane_f == e0
    oh1 = lane_f == e1
    cnt = oh0.astype(F32) + oh1.astype(F32)
    ri = lax.broadcasted_iota(jnp.int32, (TM, TM), 0)
    cj = lax.broadcasted_iota(jnp.int32, (TM, TM), 1)
    strict = (cj < ri).astype(BF16)
    before = jnp.dot(strict, cnt.astype(BF16), preferred_element_type=F32) + carry[0:1, :]
    r0 = jnp.sum(jnp.where(oh0, before, 0.0), axis=-1, keepdims=True)
    r1 = jnp.sum(jnp.where(oh1, before, 0.0), axis=-1, keepdims=True)
    carry[0:1, :] = carry[0:1, :] + jnp.sum(cnt, axis=0, keepdims=True)

    mi = jnp.where(lane == 0, e0, jnp.where(lane == 1, e1, jnp.where(lane == 2, r0, jnp.where(lane == 3, r1, 0.0))))
    mi_ref[...] = mi.astype(jnp.int32)
    mf_ref[...] = jnp.where(lane == 0, c0, jnp.where(lane == 1, c1, 0.0))
    cnt_ref[...] = carry[...]


def _proj_ln_kernel(y1_ref, y2_ref, w_ref, x_ref, mod_ref, g_ref, b_ref, wr_ref, br_ref,
                    o_ref, mi_ref, mf_ref, cnt_ref, carry):
    heads = jnp.concatenate([y1_ref[...], y2_ref[...]], axis=1).astype(BF16)
    y = jnp.dot(heads, w_ref[...], preferred_element_type=F32)
    r = DEEPNORM_ALPHA * x_ref[...] + mod_ref[0, 2:3, :] * y
    x1 = _layer_norm(r, g_ref[...], b_ref[...])
    o_ref[...] = x1
    u = x1 * (1.0 + mod_ref[0, 4:5, :]) + mod_ref[0, 3:4, :]
    _route_tile(u, wr_ref, br_ref, mi_ref, mf_ref, cnt_ref, carry)


def _proj_ln(y1, y2, col2, w_out_bf16, x, mod_l, ln_g, ln_b, w_route, b_route, latent_only):
    half = D_MODEL // 2
    rt = _row_tile(latent_only)
    mr = _mod_row(latent_only)
    n_tiles = _n_tiles(latent_only)
    n_tok = n_tiles * TM
    row_blk = lambda w: pl.BlockSpec((TM, w), lambda i: (i, 0))
    const = lambda shape: pl.BlockSpec(shape, lambda i: (0, 0))
    return pl.pallas_call(
        _proj_ln_kernel,
        grid=(n_tiles,),
        in_specs=[
            pl.BlockSpec((TM, half), lambda i: (i, 0)),
            pl.BlockSpec((TM, half), lambda i: (i, col2)),
            const((D_MODEL, D_MODEL)),
            pl.BlockSpec((TM, D_MODEL), lambda i: (rt(i), 0)),
            pl.BlockSpec((1, 6, D_MODEL), lambda i: (mr(i), 0, 0)),
            const((1, D_MODEL)), const((1, D_MODEL)), const((D_MODEL, LANES)), const((1, LANES)),
        ],
        out_specs=[row_blk(D_MODEL), row_blk(LANES), row_blk(LANES), const((8, LANES))],
        out_shape=[jax.ShapeDtypeStruct((n_tok, D_MODEL), F32),
                   jax.ShapeDtypeStruct((n_tok, LANES), jnp.int32),
                   jax.ShapeDtypeStruct((n_tok, LANES), F32),
                   jax.ShapeDtypeStruct((8, LANES), F32)],
        scratch_shapes=[pltpu.VMEM((8, LANES), F32)],
        compiler_params=_cparams(("arbitrary",)),
        name="proj_ln_route",
    )(y1, y2, w_out_bf16, x, mod_l, ln_g, ln_b, w_route, b_route)


def _pack_rows(x):
    half = x.shape[-1] // 2
    bits = lambda t: lax.bitcast_convert_type(t.astype(BF16).astype(F32), jnp.uint32)
    return (bits(x[:, :half]) >> 16) | (bits(x[:, half:]) & jnp.uint32(0xFFFF0000))


def _unpack_rows(w):
    lo = lax.bitcast_convert_type(w << 16, F32)
    hi = lax.bitcast_convert_type(w & jnp.uint32(0xFFFF0000), F32)
    return jnp.concatenate([lo, hi], axis=-1)


def _each_row(fn):
    def body(g, carry):
        for j in range(ROW_UNROLL):
            for k in range(2):
                fn(g, j, k)
        return carry
    lax.fori_loop(0, TM // ROW_UNROLL, body, 0)


def _dispatch_kernel(pad_end_ref, padded_ref, x_ref, mod_ref, dest_hbm, xs_hbm,
                     idx_smem, u_scr, zero_scr, sem_idx, sem_row, sem_zero):
    i = pl.program_id(0)
    slot = i % 2
    per_tile = 2 * TM

    @pl.when(i == 0)
    def _():
        zero_scr[...] = jnp.zeros_like(zero_scr)

        def zero_block(first_row):
            rows = pl.ds(pl.multiple_of(first_row, MOE_ROWS), MOE_ROWS)
            return pltpu.make_async_copy(zero_scr, xs_hbm.at[rows], sem_zero)

        n_rows = xs_hbm.shape[0]
        total = pad_end_ref[N_EXPERTS - 1]
        for e in range(N_EXPERTS):
            pl.when(padded_ref[e] > 0)(lambda e=e: zero_block(pad_end_ref[e] - MOE_ROWS).start())
            pl.when(total + e * MOE_ROWS < n_rows)(lambda e=e: zero_block(total + e * MOE_ROWS).start())
        for e in range(N_EXPERTS):
            pl.when(padded_ref[e] > 0)(lambda e=e: zero_block(0).wait())
            pl.when(total + e * MOE_ROWS < n_rows)(lambda e=e: zero_block(0).wait())

    def idx_copy(tile, sl):
        return pltpu.make_async_copy(dest_hbm.at[pl.ds(tile * per_tile, per_tile)],
                                     idx_smem.at[pl.ds(sl * per_tile, per_tile)], sem_idx)

    pl.when(i == 0)(lambda: idx_copy(0, 0).start())
    idx_copy(i, slot).wait()
    pl.when(i + 1 < pl.num_programs(0))(lambda: idx_copy(i + 1, 1 - slot).start())
    u = x_ref[...] * (1.0 + mod_ref[0, 4:5, :]) + mod_ref[0, 3:4, :]
    u_scr[slot] = _pack_rows(u).reshape(TM // ROW_UNROLL, ROW_UNROLL, PACKED)

    def row_copy(sl, g, j, dst_row):
        return pltpu.make_async_copy(u_scr.at[sl, g, pl.ds(j, 1)], xs_hbm.at[pl.ds(dst_row, 1)], sem_row.at[sl])

    def dest_of(sl, g, j, k):
        return idx_smem[sl * per_tile + g * (2 * ROW_UNROLL) + (2 * j + k)]

    _each_row(lambda g, j, k: row_copy(slot, g, j, dest_of(slot, g, j, k)).start())

    @pl.when(i > 0)
    def _():
        _each_row(lambda g, j, k: row_copy(1 - slot, g, j, 0).wait())

    @pl.when(i == pl.num_programs(0) - 1)
    def _():
        _each_row(lambda g, j, k: row_copy(slot, g, j, 0).wait())


def _dispatch(pad_end, padded, x1, mod_l, dest, n_blocks, latent_only):
    mr = _mod_row(latent_only)
    return pl.pallas_call(
        _dispatch_kernel,
        grid_spec=pltpu.PrefetchScalarGridSpec(
            num_scalar_prefetch=2,
            grid=(_n_tiles(latent_only),),
            in_specs=[
                pl.BlockSpec((TM, D_MODEL), lambda i, pe, pd: (i, 0)),
                pl.BlockSpec((1, 6, D_MODEL), lambda i, pe, pd: (mr(i), 0, 0)),
                pl.BlockSpec(memory_space=pl.ANY),
            ],
            out_specs=pl.BlockSpec(memory_space=pl.ANY),
            scratch_shapes=[
                pltpu.SMEM((2 * 2 * TM,), jnp.int32),
                pltpu.VMEM((2, TM // ROW_UNROLL, ROW_UNROLL, PACKED), jnp.uint32),
                pltpu.VMEM((MOE_ROWS, PACKED), jnp.uint32),
                pltpu.SemaphoreType.DMA(()),
                pltpu.SemaphoreType.DMA((2,)),
                pltpu.SemaphoreType.DMA(()),
            ],
        ),
        out_shape=jax.ShapeDtypeStruct((n_blocks * MOE_ROWS, PACKED), jnp.uint32),
        compiler_params=_cparams(("arbitrary",)),
        name="moe_dispatch",
    )(pad_end, padded, x1, mod_l, dest)


def _expert_kernel(blk_exp_ref, n_used_ref, x_ref, wg_ref, wu_ref, wd_ref, o_ref):
    used = pl.program_id(0) < n_used_ref[0]

    @pl.when(used)
    def _():
        x = _unpack_rows(x_ref[...]).astype(BF16)
        gate = jnp.dot(x, wg_ref[0, 0].astype(BF16), preferred_element_type=F32)
        up = jnp.dot(x, wu_ref[0, 0].astype(BF16), preferred_element_type=F32)
        hid = (_silu(gate) * up).astype(BF16)
        o_ref[...] = _pack_rows(jnp.dot(hid, wd_ref[0, 0].astype(BF16), preferred_element_type=F32))

    @pl.when(jnp.logical_not(used))
    def _():
        o_ref[...] = jnp.zeros_like(o_ref)


def _experts(blk_exp, n_used, xs, layer, w_gate, w_up, w_down):
    n_blocks = xs.shape[0] // MOE_ROWS
    row_in = pl.BlockSpec((MOE_ROWS, PACKED), lambda i, be, nu: (jnp.minimum(i, nu[0] - 1), 0))
    w_in = pl.BlockSpec((1, 1, D_MODEL, D_EXPERT), lambda i, be, nu: (layer, be[i], 0, 0))
    w_out = pl.BlockSpec((1, 1, D_EXPERT, D_MODEL), lambda i, be, nu: (layer, be[i], 0, 0))
    return pl.pallas_call(
        _expert_kernel,
        grid_spec=pltpu.PrefetchScalarGridSpec(
            num_scalar_prefetch=2,
            grid=(n_blocks,),
            in_specs=[row_in, w_in, w_in, w_out],
            out_specs=pl.BlockSpec((MOE_ROWS, PACKED), lambda i, be, nu: (i, 0)),
        ),
        out_shape=jax.ShapeDtypeStruct(xs.shape, jnp.uint32),
        compiler_params=_cparams(("arbitrary",)),
        name="moe_experts",
    )(blk_exp, n_used, xs, w_gate, w_up, w_down)


def _combine_ln_kernel(ys_hbm, dest_hbm, mf_ref, x_ref, mod_ref, g_ref, b_ref, o_ref,
                       idx_smem, y_buf, sem_idx, sem_row):
    i = pl.program_id(0)
    slot = i % 2
    per_tile = 2 * TM

    def row_copy(sl, g, j, k, src_row):
        return pltpu.make_async_copy(ys_hbm.at[pl.ds(src_row, 1)], y_buf.at[sl, k, g, pl.ds(j, 1)], sem_row.at[sl])

    def idx_copy(tile, sl):
        return pltpu.make_async_copy(dest_hbm.at[pl.ds(tile * per_tile, per_tile)],
                                     idx_smem.at[pl.ds(sl * per_tile, per_tile)], sem_idx)

    def request(sl):
        _each_row(lambda g, j, k: row_copy(
            sl, g, j, k, idx_smem[sl * per_tile + g * (2 * ROW_UNROLL) + (2 * j + k)]).start())

    n = pl.num_programs(0)

    @pl.when(i == 0)
    def _():
        idx_copy(0, 0).start()
        idx_copy(0, 0).wait()
        request(0)
        pl.when(n > 1)(lambda: idx_copy(1, 1).start())

    @pl.when(i + 1 < n)
    def _():
        idx_copy(i + 1, 1 - slot).wait()
        request(1 - slot)
        pl.when(i + 2 < n)(lambda: idx_copy(i + 2, slot).start())

    _each_row(lambda g, j, k: row_copy(slot, g, j, k, 0).wait())

    mf = mf_ref[...]
    y0 = _unpack_rows(y_buf[slot, 0].reshape(TM, PACKED))
    y1 = _unpack_rows(y_buf[slot, 1].reshape(TM, PACKED))
    y = mf[:, 0:1] * y0 + mf[:, 1:2] * y1
    r = DEEPNORM_ALPHA * x_ref[...] + mod_ref[0, 5:6, :] * y
    o_ref[...] = _layer_norm(r, g_ref[...], b_ref[...])


def _combine_ln(ys, dest, mf, x1, mod_l, ln_g, ln_b, latent_only):
    mr = _mod_row(latent_only)
    n_tiles = _n_tiles(latent_only)
    return pl.pallas_call(
        _combine_ln_kernel,
        grid=(n_tiles,),
        in_specs=[
            pl.BlockSpec(memory_space=pl.ANY),
            pl.BlockSpec(memory_space=pl.ANY),
            pl.BlockSpec((TM, LANES), lambda i: (i, 0)),
            pl.BlockSpec((TM, D_MODEL), lambda i: (i, 0)),
            pl.BlockSpec((1, 6, D_MODEL), lambda i: (mr(i), 0, 0)),
            pl.BlockSpec((1, D_MODEL), lambda i: (0, 0)),
            pl.BlockSpec((1, D_MODEL), lambda i: (0, 0)),
        ],
        out_specs=pl.BlockSpec((TM, D_MODEL), lambda i: (i, 0)),
        out_shape=jax.ShapeDtypeStruct((n_tiles * TM, D_MODEL), F32),
        scratch_shapes=[
            pltpu.SMEM((2 * 2 * TM,), jnp.int32),
            pltpu.VMEM((2, 2, TM // ROW_UNROLL, ROW_UNROLL, PACKED), jnp.uint32),
            pltpu.SemaphoreType.DMA(()),
            pltpu.SemaphoreType.DMA((2,)),
        ],
        compiler_params=_cparams(("arbitrary",)),
        name="combine_ln",
    )(ys, dest, mf, x1, mod_l, ln_g, ln_b)


def _route_params(w_grp, b_grp, w_rexp, b_rexp):
    pad = LANES - N_GROUPS - N_EXPERTS
    w_route = jnp.concatenate([w_grp, w_rexp, jnp.zeros((D_MODEL, pad), F32)], axis=1)
    b_route = jnp.concatenate([b_grp, b_rexp, jnp.zeros((pad,), F32)])[None, :]
    return w_route, b_route


def _moe(x1, mi, mf, cnt, mod_l, layer, w_gate, w_up, w_down, ln_g, ln_b, latent_only):
    counts = cnt[0, :N_EXPERTS].astype(jnp.int32)
    padded = (counts + MOE_ROWS - 1) // MOE_ROWS * MOE_ROWS
    pad_end = jnp.cumsum(padded)
    pad_start = pad_end - padded
    n_tok = x1.shape[0]
    n_blocks = (2 * n_tok) // MOE_ROWS + N_EXPERTS
    experts = jnp.arange(N_EXPERTS, dtype=jnp.int32)
    start_of = jnp.sum(jnp.where(mi[:, 0:2, None] == experts, pad_start, 0), axis=-1)
    dest = (start_of + mi[:, 2:4]).reshape(-1)
    blk_start = jnp.arange(n_blocks, dtype=jnp.int32) * MOE_ROWS
    blk_exp = jnp.minimum(jnp.sum((pad_end[None, :] <= blk_start[:, None]).astype(jnp.int32), axis=1),
                          N_EXPERTS - 1)
    n_used = pad_end[-1:] // MOE_ROWS
    xs = _dispatch(pad_end, padded, x1, mod_l, dest, n_blocks, latent_only)
    ys = _experts(blk_exp, n_used, xs, layer, w_gate, w_up, w_down)
    return _combine_ln(ys, dest, mf, x1, mod_l, ln_g, ln_b, latent_only)


def kernel(x, c, ctx, c_ctx, ada_w, ada_b, ln1_g, ln1_b, ln2_g, ln2_b, ab_w_in, ab_w_lr_f, ab_b_lr_f, ab_w_lr_b, ab_b_lr_b, ab_gn_a, ab_gn_b, ab_w_out, c_w_qkv, c_lq1, c_lk1, c_lq2, c_lk2, c_subln_g, c_w_out, moe_w_grp, moe_b_grp, moe_w_rexp, moe_b_rexp, moe_w_gate, moe_w_up, moe_w_down):
    assert x.shape == (BATCH, SEQ, D_MODEL) and ctx.shape == (BATCH, CTX_LEN, D_MODEL)
    xs = jnp.concatenate([ctx, x], axis=1).reshape(N_ALL, D_MODEL)
    c_all = jnp.concatenate([c, c_ctx[None, :], jnp.zeros((16 - BATCH - 1, D_MODEL), F32)], axis=0)
    mod = _ada_tables(c_all, ada_w, ada_b).reshape(DEPTH, 16, 6, D_MODEL)

    rope_a = _rope_tables(DK_A, 1)
    rope_c = _rope_tables(DH_C, 2)
    dec_a = _retention_tables()

    for l in range(DEPTH):
        last = l == DEPTH - 1
        i = l // 2
        mod_l = mod[l]
        row = lambda v: v[None, :]
        route = _route_params(moe_w_grp[l], moe_b_grp[l], moe_w_rexp[l], moe_b_rexp[l])
        if l % 2 == 0:
            assert not last
            w_in = jnp.pad(ab_w_in[i], ((0, 0), (0, AB_COLS - ab_w_in.shape[2]))).astype(BF16)
            z = _mod_matmul(xs, mod_l, w_in, *rope_a, 2 * H_A, DK_A // 4)
            ya = _scan_a(z, dec_a, row(ab_gn_a[i]))
            wf = ab_w_lr_f[i].reshape(GLA_RANK, H_B // 2, LANES)
            wb = ab_w_lr_b[i].reshape(GLA_RANK, H_B // 2, LANES)
            wlr = jnp.zeros((H_B // 2, LANES, 2 * LANES), F32)
            wlr = wlr.at[:, 0:GLA_RANK, 0:LANES].set(jnp.swapaxes(wf, 0, 1))
            wlr = wlr.at[:, GLA_RANK:2 * GLA_RANK, LANES:].set(jnp.swapaxes(wb, 0, 1))
            blr = jnp.concatenate([ab_b_lr_f[i].reshape(H_B // 2, 1, LANES),
                                   ab_b_lr_b[i].reshape(H_B // 2, 1, LANES)], axis=-1)
            yb = _scan_b(z, wlr, blr, row(ab_gn_b[i]))
            x1, mi, mf, cnt = _proj_ln(ya, yb, 0, ab_w_out[i].astype(BF16), xs, mod_l, row(ln1_g[l]),
                                       row(ln1_b[l]), *route, False)
        else:
            lam_init = 0.8 - 0.6 * math.exp(-0.3 * l)
            lam = (jnp.exp(jnp.sum(c_lq1[i] * c_lk1[i], axis=-1))
                   - jnp.exp(jnp.sum(c_lq2[i] * c_lk2[i], axis=-1))).astype(F32) + lam_init
            z = _mod_matmul(xs, mod_l, c_w_qkv[i].astype(BF16), *rope_c, 2 * H_C, DH_C // 4)
            gsub = row(c_subln_g[i])
            y = _attention(z, lam, gsub, 1.0 - lam_init, last)
            x1, mi, mf, cnt = _proj_ln(y, y, 1, c_w_out[i].astype(BF16), xs, mod_l, row(ln1_g[l]),
                                       row(ln1_b[l]), *route, last)
        xs = _moe(x1, mi, mf, cnt, mod_l, l, moe_w_gate, moe_w_up, moe_w_down, row(ln2_g[l]), row(ln2_b[l]), last)
    return xs.reshape(BATCH, SEQ, D_MODEL)
```

```python
import functools
import math

import numpy as np
import jax
import jax.numpy as jnp
from jax import lax
from jax.experimental import pallas as pl
from jax.experimental.pallas import tpu as pltpu

F32 = jnp.float32
BF16 = jnp.bfloat16

D_MODEL = 1024
BATCH = 8
SEQ = 2048
DEPTH = 4
GRID_W = 64
CTX_LEN = 256
ROPE_BASE = 10000.0
LN_EPS = 1e-5
DEEPNORM_ALPHA = (2 * DEPTH) ** 0.25
H_A = 4
DK_A = 128
DV_A = 128
CHUNK_A = 128
RET_EXP_FWD = 5.0
RET_EXP_BWD = 5.5
H_B = 4
DK_B = 64
DV_B = 128
GLA_RANK = 16
GLA_TAU = 16.0
CHUNK_B = 64
H_C = 8
DH_C = 64
DV_C = 128
N_GROUPS = 4
EXPERTS_PER_GROUP = 8
N_EXPERTS = 32
D_EXPERT = 512

LANES = 128
T_ALL = CTX_LEN + SEQ
N_ALL = BATCH * T_ALL
TM = 256
TILES_PER_BATCH = T_ALL // TM
LATENT_TILES_PER_BATCH = SEQ // TM
AB_COLS = 29 * LANES
MOE_ROWS = 512
PACKED = D_MODEL // 2
ROW_UNROLL = 8
SCAN_UNROLL = 8
GLA_GROUP = 256
TQ = 256
ATTN_KEYS = 256
VMEM_LIMIT = 56 * 1024 * 1024


def _cparams(sem):
    return pltpu.CompilerParams(dimension_semantics=sem, vmem_limit_bytes=VMEM_LIMIT)


def _silu(v):
    return v * (1.0 / (1.0 + jnp.exp(-v)))


def _n_tiles(latent_only):
    return BATCH * (LATENT_TILES_PER_BATCH if latent_only else TILES_PER_BATCH)


def _row_tile(latent_only):
    if latent_only:
        return lambda i: (i // LATENT_TILES_PER_BATCH) * TILES_PER_BATCH + 1 + i % LATENT_TILES_PER_BATCH
    return lambda i: i


def _mod_row(latent_only):
    if latent_only:
        return lambda i: i // LATENT_TILES_PER_BATCH
    return lambda i: jnp.where(i % TILES_PER_BATCH == 0, BATCH, i // TILES_PER_BATCH)


def _ada_kernel(c_ref, w_ref, b_ref, o_ref):
    sc = _silu(c_ref[...])
    o_ref[0] = jnp.dot(sc.astype(BF16), w_ref[0].astype(BF16), preferred_element_type=F32) + b_ref[0]


def _ada_tables(c_all, ada_w, ada_b):
    tn = 1536
    n_out = 6 * D_MODEL
    return pl.pallas_call(
        _ada_kernel,
        grid=(DEPTH, n_out // tn),
        in_specs=[
            pl.BlockSpec((16, D_MODEL), lambda l, j: (0, 0)),
            pl.BlockSpec((1, D_MODEL, tn), lambda l, j: (l, 0, j)),
            pl.BlockSpec((1, 1, tn), lambda l, j: (l, 0, j)),
        ],
        out_specs=pl.BlockSpec((1, 16, tn), lambda l, j: (l, 0, j)),
        out_shape=jax.ShapeDtypeStruct((DEPTH, 16, n_out), F32),
        compiler_params=_cparams(("arbitrary", "arbitrary")),
        name="ada_tables",
    )(c_all, ada_w, ada_b.reshape(DEPTH, 1, n_out))


def _modmm_kernel(x_ref, mod_ref, w_ref, cos_ref, sup_ref, sdn_ref, o_ref, *, rope_blocks, quarter):
    u = x_ref[...] * (1.0 + mod_ref[0, 1:2, :]) + mod_ref[0, 0:1, :]
    z = jnp.dot(u.astype(BF16), w_ref[...], preferred_element_type=F32)
    cos, sup, sdn = cos_ref[...], sup_ref[...], sdn_ref[...]
    for c in range(rope_blocks):
        cols = slice(c * LANES, (c + 1) * LANES)
        o_ref[:, cols] = _rope(z[:, cols], cos, sup, sdn, quarter)
    o_ref[:, rope_blocks * LANES:] = z[:, rope_blocks * LANES:]


def _mod_matmul(x, mod_l, w_bf16, cos, sup, sdn, rope_blocks, quarter):
    n_out = w_bf16.shape[1]
    tbl = pl.BlockSpec((TM, LANES), lambda i: (i % TILES_PER_BATCH, 0))
    return pl.pallas_call(
        functools.partial(_modmm_kernel, rope_blocks=rope_blocks, quarter=quarter),
        grid=(N_ALL // TM,),
        in_specs=[
            pl.BlockSpec((TM, D_MODEL), lambda i: (i, 0)),
            pl.BlockSpec((1, 6, D_MODEL), lambda i: (_mod_row(False)(i), 0, 0)),
            pl.BlockSpec((D_MODEL, n_out), lambda i: (0, 0)),
            tbl, tbl, tbl,
        ],
        out_specs=pl.BlockSpec((TM, n_out), lambda i: (i, 0)),
        out_shape=jax.ShapeDtypeStruct((N_ALL, n_out), F32),
        compiler_params=_cparams(("arbitrary",)),
        name="mod_matmul",
    )(x, mod_l, w_bf16, cos, sup, sdn)


def _rope_tables(head_dim, reps):
    rows = SEQ // GRID_W
    row = np.repeat(np.arange(rows, dtype=np.float32), GRID_W)
    col = np.tile(np.arange(GRID_W, dtype=np.float32), rows)
    quarter = head_dim // 4
    inv = (ROPE_BASE ** (-np.arange(quarter, dtype=np.float32) / quarter)).astype(np.float32)
    ang_r = row[:, None] * inv
    ang_c = col[:, None] * inv
    ang = np.concatenate([ang_r, ang_r, ang_c, ang_c], axis=-1)
    cos = np.cos(ang).astype(np.float32)
    sin = np.sin(ang).astype(np.float32)
    q_idx = (np.arange(head_dim) // quarter) % 2
    sin_up = np.where(q_idx == 1, sin, 0.0).astype(np.float32)
    sin_dn = np.where(q_idx == 0, -sin, 0.0).astype(np.float32)

    def full(t, ctx_val):
        t = np.tile(t, (1, reps))
        return jnp.asarray(np.concatenate([np.full((CTX_LEN, t.shape[1]), ctx_val, np.float32), t], axis=0))

    return full(cos, 1.0), full(sin_up, 0.0), full(sin_dn, 0.0)


def _rope(x, cos, sin_up, sin_dn, quarter):
    width = x.shape[-1]
    return x * cos + pltpu.roll(x, quarter, 1) * sin_up + pltpu.roll(x, width - quarter, 1) * sin_dn


def _dot_tb(a, b):
    return lax.dot_general(a, b, (((1,), (1,)), ((), ())), preferred_element_type=F32)


def _dot_ta(a, b):
    return lax.dot_general(a, b, (((0,), (0,)), ((), ())), preferred_element_type=F32)


def _split_bf16(x, parts):
    out = []
    for _ in range(parts):
        t = x.astype(BF16)
        out.append(t)
        x = x - t.astype(F32)
    return out


def _dot_split(a, b):
    a_hi, a_lo = _split_bf16(a, 2)
    b_hi, b_lo = _split_bf16(b, 2)
    dot = lambda u, v: jnp.dot(u, v, preferred_element_type=F32)
    n = b.shape[1]
    both = dot(a_hi, jnp.concatenate([b_hi, b_lo], axis=1))
    return both[:, :n] + (both[:, n:] + dot(a_lo, b_hi))


def _dot_mask(mask_bf16, x):
    return sum(jnp.dot(mask_bf16, t, preferred_element_type=F32) for t in reversed(_split_bf16(x, 3)))


def _retention_tables():
    c = CHUNK_A
    i = np.arange(c, dtype=np.float64)
    out = np.zeros((H_A, 7, c, LANES), np.float64)
    for h in range(H_A):
        lgf = np.log1p(-np.exp2(-(RET_EXP_FWD + h)))
        lgb = np.log1p(-np.exp2(-(RET_EXP_BWD + h)))
        d = i[:, None] - i[None, :]
        out[h, 0] = np.where(d >= 0, np.exp(lgf * d), np.exp(lgb * (-d - 1)))
        out[h, 1] = np.exp(lgf * (i + 1))[:, None]
        out[h, 2] = np.exp(lgb * (c - 1 - i))[:, None]
        out[h, 3] = np.exp(lgf * (c - 1 - i))[:, None]
        out[h, 4] = np.exp(lgb * i)[:, None]
        out[h, 5] = np.exp(lgf * c)
        out[h, 6] = np.exp(lgb * c)
    return jnp.asarray(out.astype(np.float32))


def _scan_a_kernel(q_ref, k_ref, v_ref, g_ref, dec_ref, gn_ref, o_ref, sb_scr):
    c = CHUNK_A
    n_ctx = CTX_LEN // c
    n_all = T_ALL // c
    scale = DK_A ** -0.5
    dmat = dec_ref[0, 0]
    q_f, q_b, k_f, k_b = dec_ref[0, 1], dec_ref[0, 2], dec_ref[0, 3], dec_ref[0, 4]
    g_fc, g_bc = dec_ref[0, 5], dec_ref[0, 6]
    gn = gn_ref[...]
    zero = jnp.zeros((DK_A, DV_A), F32)

    def chunk(ci):
        return pl.ds(pl.multiple_of(ci * c, c), c)

    def kv_state(ci, k_dec):
        sl = chunk(ci)
        return _dot_ta((k_ref[sl, :] * k_dec).astype(BF16), v_ref[sl, :].astype(BF16))

    def run(lo, hi, sf0, sb0):
        def bwd(j, sb):
            ci = hi - 1 - j
            sb_scr[ci] = sb
            return g_bc * sb + kv_state(ci, k_b)

        sb_fin = lax.fori_loop(0, hi - lo, bwd, sb0, unroll=SCAN_UNROLL)

        def fwd(j, sf):
            ci = lo + j
            sl = chunk(ci)
            q = q_ref[sl, :] * scale
            k = k_ref[sl, :]
            vb = v_ref[sl, :].astype(BF16)
            att = _dot_tb(q.astype(BF16), k.astype(BF16)) * dmat
            o = jnp.dot(att.astype(BF16), vb, preferred_element_type=F32)
            o = o + jnp.dot((q * q_f).astype(BF16), sf.astype(BF16), preferred_element_type=F32)
            o = o + jnp.dot((q * q_b).astype(BF16), sb_scr[ci].astype(BF16), preferred_element_type=F32)
            o = o - jnp.mean(o, axis=-1, keepdims=True)
            o = o * lax.rsqrt(jnp.mean(o * o, axis=-1, keepdims=True) + LN_EPS)
            o_ref[sl, :] = _silu(g_ref[sl, :]) * (o * gn)
            return g_fc * sf + _dot_ta((k * k_f).astype(BF16), vb)

        sf_fin = lax.fori_loop(0, hi - lo, fwd, sf0, unroll=SCAN_UNROLL)
        return sf_fin, sb_fin

    sf_c, sb_c = run(0, n_ctx, zero, zero)
    run(n_ctx, n_all, sf_c, sb_c)


def _scan_a(z, dec, gn_a):
    blk = lambda col0: pl.BlockSpec((T_ALL, LANES), lambda b, h: (b, col0 + h))
    return pl.pallas_call(
        _scan_a_kernel,
        grid=(BATCH, H_A),
        in_specs=[blk(0), blk(4), blk(8), blk(12),
                  pl.BlockSpec((1, 7, CHUNK_A, LANES), lambda b, h: (h, 0, 0, 0)),
                  pl.BlockSpec((1, LANES), lambda b, h: (0, h))],
        out_specs=pl.BlockSpec((T_ALL, LANES), lambda b, h: (b, h)),
        out_shape=jax.ShapeDtypeStruct((N_ALL, H_A * DV_A), F32),
        scratch_shapes=[pltpu.VMEM((T_ALL // CHUNK_A, DK_A, DV_A), F32)],
        compiler_params=_cparams(("arbitrary", "arbitrary")),
        name="scan_retention",
    )(z, z, z, z, dec, gn_a)


def _log_sigmoid(g):
    return jnp.minimum(g, 0.0) - jnp.log1p(jnp.exp(-jnp.abs(g)))


def _scan_b_kernel(q_ref, k_ref, v_ref, g_ref, lr_ref, wlr_ref, blr_ref, gn_ref, o_ref,
                   qf_scr, kf_scr, qb_scr, kb_scr, ktf_scr, ktb_scr, ef_scr, eb_scr, sb_scr):
    c = CHUNK_B
    n_ctx = CTX_LEN // c
    n_all = T_ALL // c
    per_group = GLA_GROUP // c
    scale = DK_B ** -0.5

    gi_r = lax.broadcasted_iota(jnp.int32, (GLA_GROUP, GLA_GROUP), 0)
    gi_c = lax.broadcasted_iota(jnp.int32, (GLA_GROUP, GLA_GROUP), 1)
    same_chunk = (gi_r // c) == (gi_c // c)
    prefix = (same_chunk & (gi_c <= gi_r)).astype(BF16)
    suffix = (same_chunk & (gi_c >= gi_r)).astype(BF16)

    def prepare(gi, carry):
        sl = pl.ds(pl.multiple_of(gi * GLA_GROUP, GLA_GROUP), GLA_GROUP)
        gates = _dot_split(lr_ref[sl, :], wlr_ref[0]) + blr_ref[0]
        laf = _log_sigmoid(gates[:, :LANES]) * (1.0 / GLA_TAU)
        lab = _log_sigmoid(gates[:, LANES:]) * (1.0 / GLA_TAU)
        b = _dot_mask(prefix, laf)
        rb = _dot_mask(suffix, lab)
        q = q_ref[sl, :] * scale
        k = k_ref[sl, :]
        qf_scr[sl, :] = (q * jnp.exp(b)).astype(BF16)
        kf_scr[sl, :] = (k * jnp.exp(-b)).astype(BF16)
        qb_scr[sl, :] = (q * jnp.exp(rb - lab)).astype(BF16)
        kb_scr[sl, :] = (k * jnp.exp(-rb)).astype(BF16)
        b3 = b.reshape(per_group, c, LANES)
        rb3 = rb.reshape(per_group, c, LANES)
        k3 = k.reshape(per_group, c, LANES)
        b_tot = b3[:, c - 1:c, :]
        rb_tot = rb3[:, 0:1, :]
        ktf_scr[sl, :] = (k3 * jnp.exp(b_tot - b3)).reshape(GLA_GROUP, LANES).astype(BF16)
        ktb_scr[sl, :] = (k3 * jnp.exp(rb_tot - rb3)).reshape(GLA_GROUP, LANES).astype(BF16)
        for m in range(per_group):
            ef_scr[gi * per_group + m] = jnp.broadcast_to(jnp.exp(b_tot[m]), (8, LANES))
            eb_scr[gi * per_group + m] = jnp.broadcast_to(jnp.exp(rb_tot[m]), (8, LANES))
        return carry

    lax.fori_loop(0, T_ALL // GLA_GROUP, prepare, 0)

    lane = lax.broadcasted_iota(jnp.int32, (1, LANES), 1)
    masks = [lane < DK_B, lane >= DK_B]
    ri = lax.broadcasted_iota(jnp.int32, (c, c), 0)
    cj = lax.broadcasted_iota(jnp.int32, (c, c), 1)
    lower = cj <= ri
    gn = gn_ref[...]
    zero = jnp.zeros((DV_B, LANES), F32)
    zero_b = jnp.zeros((), BF16)

    def chunk(ci):
        return pl.ds(pl.multiple_of(ci * c, c), c)

    def run(lo, hi, sf0, sb0):
        def bwd(j, sb):
            ci = hi - 1 - j
            sl = chunk(ci)
            e_tot = eb_scr[ci][0:1, :]
            kt = ktb_scr[sl, :]
            v = v_ref[sl, :]
            new = []
            for h in range(2):
                sb_scr[ci, h] = sb[h]
                vh = v[:, h * DV_B:(h + 1) * DV_B].astype(BF16)
                new.append(sb[h] * e_tot + _dot_ta(vh, jnp.where(masks[h], kt, zero_b)))
            return tuple(new)

        sb_fin = lax.fori_loop(0, hi - lo, bwd, sb0, unroll=SCAN_UNROLL)

        def fwd(j, sf):
            ci = lo + j
            sl = chunk(ci)
            e_tot = ef_scr[ci][0:1, :]
            qf, kf, qb, kb, kt = qf_scr[sl, :], kf_scr[sl, :], qb_scr[sl, :], kb_scr[sl, :], ktf_scr[sl, :]
            v = v_ref[sl, :]
            g = g_ref[sl, :]
            new = []
            for h in range(2):
                pick = lambda t: jnp.where(masks[h], t, zero_b)
                vh = v[:, h * DV_B:(h + 1) * DV_B].astype(BF16)
                qfh, qbh = pick(qf), pick(qb)
                att = jnp.where(lower, _dot_tb(qfh, pick(kf)), _dot_tb(qbh, pick(kb)))
                o = jnp.dot(att.astype(BF16), vh, preferred_element_type=F32)
                o = o + _dot_tb(qfh, sf[h].astype(BF16))
                o = o + _dot_tb(qbh, sb_scr[ci, h].astype(BF16))
                o = o * lax.rsqrt(jnp.mean(o * o, axis=-1, keepdims=True) + LN_EPS)
                cols = slice(h * DV_B, (h + 1) * DV_B)
                o_ref[sl, cols] = _silu(g[:, cols]) * (o * gn[:, cols])
                new.append(sf[h] * e_tot + _dot_ta(vh, pick(kt)))
            return tuple(new)

        sf_fin = lax.fori_loop(0, hi - lo, fwd, sf0, unroll=SCAN_UNROLL)
        return sf_fin, sb_fin

    sf_c, sb_c = run(0, n_ctx, (zero, zero), (zero, zero))
    run(n_ctx, n_all, sf_c, sb_c)


def _scan_b(z, wlr, blr, gn_b):
    pairs = H_B // 2
    return pl.pallas_call(
        _scan_b_kernel,
        grid=(BATCH, pairs),
        in_specs=[
            pl.BlockSpec((T_ALL, LANES), lambda b, p: (b, 16 + p)),
            pl.BlockSpec((T_ALL, LANES), lambda b, p: (b, 18 + p)),
            pl.BlockSpec((T_ALL, 2 * DV_B), lambda b, p: (b, 10 + p)),
            pl.BlockSpec((T_ALL, 2 * DV_B), lambda b, p: (b, 12 + p)),
            pl.BlockSpec((T_ALL, LANES), lambda b, p: (b, 28)),
            pl.BlockSpec((1, LANES, 2 * LANES), lambda b, p: (p, 0, 0)),
            pl.BlockSpec((1, 1, 2 * LANES), lambda b, p: (p, 0, 0)),
            pl.BlockSpec((1, 2 * DV_B), lambda b, p: (0, p)),
        ],
        out_specs=pl.BlockSpec((T_ALL, 2 * DV_B), lambda b, p: (b, p)),
        out_shape=jax.ShapeDtypeStruct((N_ALL, H_B * DV_B), F32),
        scratch_shapes=[pltpu.VMEM((T_ALL, LANES), BF16)] * 6
                       + [pltpu.VMEM((T_ALL // CHUNK_B, 8, LANES), F32)] * 2
                       + [pltpu.VMEM((T_ALL // CHUNK_B, 2, DV_B, LANES), F32)],
        compiler_params=_cparams(("arbitrary", "arbitrary")),
        name="scan_gla",
    )(z, z, z, z, z, wlr, blr, gn_b)


def _attn_kernel(lam_ref, q_ref, k_ref, v_ref, gsub_ref, o_ref, k_scr, v_scr, *, post_scale, tile0):
    h = pl.program_id(1)
    t = pl.program_id(2)
    scale = DH_C ** -0.5 * math.log2(math.e)

    @pl.when(t == 0)
    def _():
        k_scr[...] = k_ref[...].astype(BF16)
        v_scr[:, 0:LANES] = v_ref[...].astype(BF16)
        v_scr[:, LANES:] = jnp.ones((T_ALL, LANES), BF16)

    lam = lam_ref[h]
    lane = lax.broadcasted_iota(jnp.int32, (1, LANES), 1)
    m1 = (lane < DH_C).astype(F32)
    m2 = (lane >= DH_C).astype(F32)

    def attend(n_keys):
        q = q_ref[...] * scale
        qs = [(q * m1).astype(BF16), (q * m2).astype(BF16)]
        run_max = [jnp.full((TQ, 1), -jnp.inf, F32) for _ in range(2)]
        acc = [jnp.zeros((TQ, 2 * LANES), F32) for _ in range(2)]
        chunk = min(ATTN_KEYS, n_keys)
        for c in range(n_keys // chunk):
            keys = slice(c * chunk, (c + 1) * chunk)
            kb = k_scr[keys, :]
            vb = v_scr[keys, :]
            for i in range(2):
                s = _dot_tb(qs[i], kb)
                new_max = jnp.maximum(run_max[i], jnp.max(s, axis=-1, keepdims=True))
                p = jnp.exp2(s - new_max).astype(BF16)
                acc[i] = acc[i] * jnp.exp2(run_max[i] - new_max) + jnp.dot(p, vb, preferred_element_type=F32)
                run_max[i] = new_max
        o = acc[0][:, :LANES] / acc[0][:, LANES:] - lam * (acc[1][:, :LANES] / acc[1][:, LANES:])
        o = o * lax.rsqrt(jnp.mean(o * o, axis=-1, keepdims=True) + LN_EPS)
        o_ref[...] = o * (gsub_ref[...] * post_scale)

    if tile0 == 0:
        pl.when(t == 0)(lambda: attend(CTX_LEN))
        pl.when(t > 0)(lambda: attend(T_ALL))
    else:
        attend(T_ALL)


def _attention(z, lam, gsub, post_scale, latent_only):
    tile0 = 1 if latent_only else 0
    n_qt = TILES_PER_BATCH - tile0
    kern = functools.partial(_attn_kernel, post_scale=post_scale, tile0=tile0)
    kv = lambda col0: pl.BlockSpec((T_ALL, LANES), lambda b, h, t, lam_r: (b, col0 + h))
    return pl.pallas_call(
        kern,
        grid_spec=pltpu.PrefetchScalarGridSpec(
            num_scalar_prefetch=1,
            grid=(BATCH, H_C, n_qt),
            in_specs=[pl.BlockSpec((TQ, LANES), lambda b, h, t, lam_r: (b * TILES_PER_BATCH + tile0 + t, h)),
                      kv(H_C), kv(2 * H_C),
                      pl.BlockSpec((1, LANES), lambda b, h, t, lam_r: (0, h))],
            out_specs=pl.BlockSpec((TQ, LANES), lambda b, h, t, lam_r: (b * n_qt + t, h)),
            scratch_shapes=[pltpu.VMEM((T_ALL, LANES), BF16), pltpu.VMEM((T_ALL, 2 * LANES), BF16)],
        ),
        out_shape=jax.ShapeDtypeStruct((BATCH * n_qt * TQ, H_C * DV_C), F32),
        compiler_params=_cparams(("arbitrary", "arbitrary", "arbitrary")),
        name="diff_attention",
    )(lam, z, z, z, gsub)


def _layer_norm(r, g, b):
    mu = jnp.mean(r, axis=-1, keepdims=True)
    d = r - mu
    var = jnp.mean(d * d, axis=-1, keepdims=True)
    return d * lax.rsqrt(var + LN_EPS) * g + b


def _route_tile(u, w_ref, b_ref, mi_ref, mf_ref, cnt_ref, carry):
    @pl.when(pl.program_id(0) == 0)
    def _():
        carry[...] = jnp.zeros_like(carry)

    logits = _dot_split(u, w_ref[...]) + b_ref[...]
    lane = lax.broadcasted_iota(jnp.int32, (TM, LANES), 1)
    lane_f = lane.astype(F32)
    neg = -jnp.inf
    big = 1e9

    gmask = lane < N_GROUPS
    gl = jnp.where(gmask, logits, neg)
    gmax = jnp.max(gl, axis=-1, keepdims=True)
    gidx = jnp.min(jnp.where(gl == gmax, lane_f, big), axis=-1, keepdims=True)
    gw = 1.0 / jnp.sum(jnp.where(gmask, jnp.exp(logits - gmax), 0.0), axis=-1, keepdims=True)

    e_lane = lane - N_GROUPS
    in_grp = (e_lane >= 0) & (e_lane < N_EXPERTS) & ((e_lane >> 3) == gidx.astype(jnp.int32))
    el = jnp.where(in_grp, logits, neg)
    v1 = jnp.max(el, axis=-1, keepdims=True)
    i1 = jnp.min(jnp.where(el == v1, lane_f, big), axis=-1, keepdims=True)
    el2 = jnp.where(lane_f == i1, neg, el)
    v2 = jnp.max(el2, axis=-1, keepdims=True)
    i2 = jnp.min(jnp.where(el2 == v2, lane_f, big), axis=-1, keepdims=True)
    t = jnp.exp(v2 - v1)
    c0 = gw / (1.0 + t)
    c1 = gw * t / (1.0 + t)
    e0 = i1 - N_GROUPS
    e1 = i2 - N_GROUPS

    oh0 = lane_f == e0
    oh1 = lane_f == e1
    cnt = oh0.astype(F32) + oh1.astype(F32)
    ri = lax.broadcasted_iota(jnp.int32, (TM, TM), 0)
    cj = lax.broadcasted_iota(jnp.int32, (TM, TM), 1)
    strict = (cj < ri).astype(BF16)
    before = jnp.dot(strict, cnt.astype(BF16), preferred_element_type=F32) + carry[0:1, :]
    r0 = jnp.sum(jnp.where(oh0, before, 0.0), axis=-1, keepdims=True)
    r1 = jnp.sum(jnp.where(oh1, before, 0.0), axis=-1, keepdims=True)
    carry[0:1, :] = carry[0:1, :] + jnp.sum(cnt, axis=0, keepdims=True)

    mi = jnp.where(lane == 0, e0, jnp.where(lane == 1, e1, jnp.where(lane == 2, r0, jnp.where(lane == 3, r1, 0.0))))
    mi_ref[...] = mi.astype(jnp.int32)
    mf_ref[...] = jnp.where(lane == 0, c0, jnp.where(lane == 1, c1, 0.0))
    cnt_ref[...] = carry[...]


def _proj_ln_kernel(y1_ref, y2_ref, w_ref, x_ref, mod_ref, g_ref, b_ref, wr_ref, br_ref,
                    o_ref, mi_ref, mf_ref, cnt_ref, carry):
    heads = jnp.concatenate([y1_ref[...], y2_ref[...]], axis=1).astype(BF16)
    y = jnp.dot(heads, w_ref[...], preferred_element_type=F32)
    r = DEEPNORM_ALPHA * x_ref[...] + mod_ref[0, 2:3, :] * y
    x1 = _layer_norm(r, g_ref[...], b_ref[...])
    o_ref[...] = x1
    u = x1 * (1.0 + mod_ref[0, 4:5, :]) + mod_ref[0, 3:4, :]
    _route_tile(u, wr_ref, br_ref, mi_ref, mf_ref, cnt_ref, carry)


def _proj_ln(y1, y2, col2, w_out_bf16, x, mod_l, ln_g, ln_b, w_route, b_route, latent_only):
    half = D_MODEL // 2
    rt = _row_tile(latent_only)
    mr = _mod_row(latent_only)
    n_tiles = _n_tiles(latent_only)
    n_tok = n_tiles * TM
    row_blk = lambda w: pl.BlockSpec((TM, w), lambda i: (i, 0))
    const = lambda shape: pl.BlockSpec(shape, lambda i: (0, 0))
    return pl.pallas_call(
        _proj_ln_kernel,
        grid=(n_tiles,),
        in_specs=[
            pl.BlockSpec((TM, half), lambda i: (i, 0)),
            pl.BlockSpec((TM, half), lambda i: (i, col2)),
            const((D_MODEL, D_MODEL)),
            pl.BlockSpec((TM, D_MODEL), lambda i: (rt(i), 0)),
            pl.BlockSpec((1, 6, D_MODEL), lambda i: (mr(i), 0, 0)),
            const((1, D_MODEL)), const((1, D_MODEL)), const((D_MODEL, LANES)), const((1, LANES)),
        ],
        out_specs=[row_blk(D_MODEL), row_blk(LANES), row_blk(LANES), const((8, LANES))],
        out_shape=[jax.ShapeDtypeStruct((n_tok, D_MODEL), F32),
                   jax.ShapeDtypeStruct((n_tok, LANES), jnp.int32),
                   jax.ShapeDtypeStruct((n_tok, LANES), F32),
                   jax.ShapeDtypeStruct((8, LANES), F32)],
        scratch_shapes=[pltpu.VMEM((8, LANES), F32)],
        compiler_params=_cparams(("arbitrary",)),
        name="proj_ln_route",
    )(y1, y2, w_out_bf16, x, mod_l, ln_g, ln_b, w_route, b_route)


def _pack_rows(x):
    half = x.shape[-1] // 2
    bits = lambda t: lax.bitcast_convert_type(t.astype(BF16).astype(F32), jnp.uint32)
    return (bits(x[:, :half]) >> 16) | (bits(x[:, half:]) & jnp.uint32(0xFFFF0000))


def _unpack_rows(w):
    lo = lax.bitcast_convert_type(w << 16, F32)
    hi = lax.bitcast_convert_type(w & jnp.uint32(0xFFFF0000), F32)
    return jnp.concatenate([lo, hi], axis=-1)


def _each_row(fn, per_group=None):
    def body(g, carry):
        if per_group is not None:
            per_group(g)
        for j in range(ROW_UNROLL):
            for k in range(2):
                fn(g, j, k)
        return carry
    lax.fori_loop(0, TM // ROW_UNROLL, body, 0)


def _dispatch_kernel(pad_end_ref, padded_ref, x_ref, xn_ref, mod_ref, modn_ref, dest_hbm, xs_hbm,
                     idx_smem, u_scr, zero_scr, sem_idx, sem_row, sem_zero):
    i = pl.program_id(0)
    per_tile = 2 * TM

    @pl.when(i == 0)
    def _():
        zero_scr[...] = jnp.zeros_like(zero_scr)

        def zero_block(first_row):
            rows = pl.ds(pl.multiple_of(first_row, MOE_ROWS), MOE_ROWS)
            return pltpu.make_async_copy(zero_scr, xs_hbm.at[rows], sem_zero)

        n_rows = xs_hbm.shape[0]
        total = pad_end_ref[N_EXPERTS - 1]
        for e in range(N_EXPERTS):
            pl.when(padded_ref[e] > 0)(lambda e=e: zero_block(pad_end_ref[e] - MOE_ROWS).start())
            pl.when(total + e * MOE_ROWS < n_rows)(lambda e=e: zero_block(total + e * MOE_ROWS).start())
        for e in range(N_EXPERTS):
            pl.when(padded_ref[e] > 0)(lambda e=e: zero_block(0).wait())
            pl.when(total + e * MOE_ROWS < n_rows)(lambda e=e: zero_block(0).wait())

    def idx_copy(tile, sl):
        return pltpu.make_async_copy(dest_hbm.at[pl.ds(tile * per_tile, per_tile)],
                                     idx_smem.at[pl.ds(sl * per_tile, per_tile)], sem_idx)

    n = pl.num_programs(0)
    half = i % 2
    pl.when(i == 0)(lambda: idx_copy(0, 0).start())
    idx_copy(i, half).wait()
    pl.when(i + 1 < n)(lambda: idx_copy(i + 1, 1 - half).start())

    def packed(x, m_ref):
        return _pack_rows(x * (1.0 + m_ref[0, 4:5, :]) + m_ref[0, 3:4, :])

    @pl.when(i == 0)
    def _():
        u_scr[0] = packed(x_ref[...], mod_ref).reshape(TM // ROW_UNROLL, ROW_UNROLL, PACKED)

    slot = i % 3
    nxt = (i + 1) % 3
    prv = (i + 2) % 3

    def row_copy(sl, g, j, dst_row):
        return pltpu.make_async_copy(u_scr.at[sl, g, pl.ds(j, 1)], xs_hbm.at[pl.ds(dst_row, 1)], sem_row.at[sl])

    def send(g, j, k):
        row_copy(slot, g, j, idx_smem[half * per_tile + g * (2 * ROW_UNROLL) + (2 * j + k)]).start()

    def pack_next(g):
        rows = pl.ds(pl.multiple_of(g * ROW_UNROLL, ROW_UNROLL), ROW_UNROLL)
        u_scr[nxt, g] = packed(xn_ref[rows, :], modn_ref)

    pl.when(i + 1 < n)(lambda: _each_row(send, pack_next))
    pl.when(i + 1 >= n)(lambda: _each_row(send))

    @pl.when(i > 0)
    def _():
        _each_row(lambda g, j, k: row_copy(prv, g, j, 0).wait())

    @pl.when(i == n - 1)
    def _():
        _each_row(lambda g, j, k: row_copy(slot, g, j, 0).wait())


def _dispatch(pad_end, padded, x1, mod_l, dest, n_blocks, latent_only):
    mr = _mod_row(latent_only)
    n_tiles = _n_tiles(latent_only)
    nxt = lambda i: jnp.minimum(i + 1, n_tiles - 1)
    return pl.pallas_call(
        _dispatch_kernel,
        grid_spec=pltpu.PrefetchScalarGridSpec(
            num_scalar_prefetch=2,
            grid=(n_tiles,),
            in_specs=[
                pl.BlockSpec((TM, D_MODEL), lambda i, pe, pd: (i, 0)),
                pl.BlockSpec((TM, D_MODEL), lambda i, pe, pd: (nxt(i), 0)),
                pl.BlockSpec((1, 6, D_MODEL), lambda i, pe, pd: (mr(i), 0, 0)),
                pl.BlockSpec((1, 6, D_MODEL), lambda i, pe, pd: (mr(nxt(i)), 0, 0)),
                pl.BlockSpec(memory_space=pl.ANY),
            ],
            out_specs=pl.BlockSpec(memory_space=pl.ANY),
            scratch_shapes=[
                pltpu.SMEM((2 * 2 * TM,), jnp.int32),
                pltpu.VMEM((3, TM // ROW_UNROLL, ROW_UNROLL, PACKED), jnp.uint32),
                pltpu.VMEM((MOE_ROWS, PACKED), jnp.uint32),
                pltpu.SemaphoreType.DMA(()),
                pltpu.SemaphoreType.DMA((3,)),
                pltpu.SemaphoreType.DMA(()),
            ],
        ),
        out_shape=jax.ShapeDtypeStruct((n_blocks * MOE_ROWS, PACKED), jnp.uint32),
        compiler_params=_cparams(("arbitrary",)),
        name="moe_dispatch",
    )(pad_end, padded, x1, x1, mod_l, mod_l, dest)


def _expert_kernel(blk_exp_ref, n_used_ref, x_ref, wg_ref, wu_ref, wd_ref, o_ref):
    used = pl.program_id(0) < n_used_ref[0]

    @pl.when(used)
    def _():
        x = _unpack_rows(x_ref[...]).astype(BF16)
        gate = jnp.dot(x, wg_ref[0, 0].astype(BF16), preferred_element_type=F32)
        up = jnp.dot(x, wu_ref[0, 0].astype(BF16), preferred_element_type=F32)
        hid = (_silu(gate) * up).astype(BF16)
        o_ref[...] = _pack_rows(jnp.dot(hid, wd_ref[0, 0].astype(BF16), preferred_element_type=F32))

    @pl.when(jnp.logical_not(used))
    def _():
        o_ref[...] = jnp.zeros_like(o_ref)


def _experts(blk_exp, n_used, xs, layer, w_gate, w_up, w_down):
    n_blocks = xs.shape[0] // MOE_ROWS
    row_in = pl.BlockSpec((MOE_ROWS, PACKED), lambda i, be, nu: (jnp.minimum(i, nu[0] - 1), 0))
    w_in = pl.BlockSpec((1, 1, D_MODEL, D_EXPERT), lambda i, be, nu: (layer, be[i], 0, 0))
    w_out = pl.BlockSpec((1, 1, D_EXPERT, D_MODEL), lambda i, be, nu: (layer, be[i], 0, 0))
    return pl.pallas_call(
        _expert_kernel,
        grid_spec=pltpu.PrefetchScalarGridSpec(
            num_scalar_prefetch=2,
            grid=(n_blocks,),
            in_specs=[row_in, w_in, w_in, w_out],
            out_specs=pl.BlockSpec((MOE_ROWS, PACKED), lambda i, be, nu: (i, 0)),
        ),
        out_shape=jax.ShapeDtypeStruct(xs.shape, jnp.uint32),
        compiler_params=_cparams(("arbitrary",)),
        name="moe_experts",
    )(blk_exp, n_used, xs, w_gate, w_up, w_down)


def _combine_ln_kernel(ys_hbm, dest_hbm, mf_ref, x_ref, mod_ref, g_ref, b_ref, o_ref,
                       idx_smem, y_buf, sem_idx, sem_row):
    i = pl.program_id(0)
    slot = i % 2
    per_tile = 2 * TM

    def row_copy(sl, g, j, k, src_row):
        return pltpu.make_async_copy(ys_hbm.at[pl.ds(src_row, 1)], y_buf.at[sl, k, g, pl.ds(j, 1)], sem_row.at[sl])

    def idx_copy(tile, sl):
        return pltpu.make_async_copy(dest_hbm.at[pl.ds(tile * per_tile, per_tile)],
                                     idx_smem.at[pl.ds(sl * per_tile, per_tile)], sem_idx)

    def request(sl):
        _each_row(lambda g, j, k: row_copy(
            sl, g, j, k, idx_smem[sl * per_tile + g * (2 * ROW_UNROLL) + (2 * j + k)]).start())

    n = pl.num_programs(0)

    @pl.when(i == 0)
    def _():
        idx_copy(0, 0).start()
        idx_copy(0, 0).wait()
        request(0)
        pl.when(n > 1)(lambda: idx_copy(1, 1).start())

    @pl.when(i + 1 < n)
    def _():
        idx_copy(i + 1, 1 - slot).wait()
        request(1 - slot)
        pl.when(i + 2 < n)(lambda: idx_copy(i + 2, slot).start())

    _each_row(lambda g, j, k: row_copy(slot, g, j, k, 0).wait())

    mf = mf_ref[...]
    y0 = _unpack_rows(y_buf[slot, 0].reshape(TM, PACKED))
    y1 = _unpack_rows(y_buf[slot, 1].reshape(TM, PACKED))
    y = mf[:, 0:1] * y0 + mf[:, 1:2] * y1
    r = DEEPNORM_ALPHA * x_ref[...] + mod_ref[0, 5:6, :] * y
    o_ref[...] = _layer_norm(r, g_ref[...], b_ref[...])


def _combine_ln(ys, dest, mf, x1, mod_l, ln_g, ln_b, latent_only):
    mr = _mod_row(latent_only)
    n_tiles = _n_tiles(latent_only)
    return pl.pallas_call(
        _combine_ln_kernel,
        grid=(n_tiles,),
        in_specs=[
            pl.BlockSpec(memory_space=pl.ANY),
            pl.BlockSpec(memory_space=pl.ANY),
            pl.BlockSpec((TM, LANES), lambda i: (i, 0)),
            pl.BlockSpec((TM, D_MODEL), lambda i: (i, 0)),
            pl.BlockSpec((1, 6, D_MODEL), lambda i: (mr(i), 0, 0)),
            pl.BlockSpec((1, D_MODEL), lambda i: (0, 0)),
            pl.BlockSpec((1, D_MODEL), lambda i: (0, 0)),
        ],
        out_specs=pl.BlockSpec((TM, D_MODEL), lambda i: (i, 0)),
        out_shape=jax.ShapeDtypeStruct((n_tiles * TM, D_MODEL), F32),
        scratch_shapes=[
            pltpu.SMEM((2 * 2 * TM,), jnp.int32),
            pltpu.VMEM((2, 2, TM // ROW_UNROLL, ROW_UNROLL, PACKED), jnp.uint32),
            pltpu.SemaphoreType.DMA(()),
            pltpu.SemaphoreType.DMA((2,)),
        ],
        compiler_params=_cparams(("arbitrary",)),
        name="combine_ln",
    )(ys, dest, mf, x1, mod_l, ln_g, ln_b)


def _route_params(w_grp, b_grp, w_rexp, b_rexp):
    pad = LANES - N_GROUPS - N_EXPERTS
    w_route = jnp.concatenate([w_grp, w_rexp, jnp.zeros((D_MODEL, pad), F32)], axis=1)
    b_route = jnp.concatenate([b_grp, b_rexp, jnp.zeros((pad,), F32)])[None, :]
    return w_route, b_route


def _moe(x1, mi, mf, cnt, mod_l, layer, w_gate, w_up, w_down, ln_g, ln_b, latent_only):
    counts = cnt[0, :N_EXPERTS].astype(jnp.int32)
    padded = (counts + MOE_ROWS - 1) // MOE_ROWS * MOE_ROWS
    pad_end = jnp.cumsum(padded)
    pad_start = pad_end - padded
    n_tok = x1.shape[0]
    n_blocks = (2 * n_tok) // MOE_ROWS + N_EXPERTS
    experts = jnp.arange(N_EXPERTS, dtype=jnp.int32)
    start_of = jnp.sum(jnp.where(mi[:, 0:2, None] == experts, pad_start, 0), axis=-1)
    dest = (start_of + mi[:, 2:4]).reshape(-1)
    blk_start = jnp.arange(n_blocks, dtype=jnp.int32) * MOE_ROWS
    blk_exp = jnp.minimum(jnp.sum((pad_end[None, :] <= blk_start[:, None]).astype(jnp.int32), axis=1),
                          N_EXPERTS - 1)
    n_used = pad_end[-1:] // MOE_ROWS
    xs = _dispatch(pad_end, padded, x1, mod_l, dest, n_blocks, latent_only)
    ys = _experts(blk_exp, n_used, xs, layer, w_gate, w_up, w_down)
    return _combine_ln(ys, dest, mf, x1, mod_l, ln_g, ln_b, latent_only)


def kernel(x, c, ctx, c_ctx, ada_w, ada_b, ln1_g, ln1_b, ln2_g, ln2_b, ab_w_in, ab_w_lr_f, ab_b_lr_f, ab_w_lr_b, ab_b_lr_b, ab_gn_a, ab_gn_b, ab_w_out, c_w_qkv, c_lq1, c_lk1, c_lq2, c_lk2, c_subln_g, c_w_out, moe_w_grp, moe_b_grp, moe_w_rexp, moe_b_rexp, moe_w_gate, moe_w_up, moe_w_down):
    assert x.shape == (BATCH, SEQ, D_MODEL) and ctx.shape == (BATCH, CTX_LEN, D_MODEL)
    xs = jnp.concatenate([ctx, x], axis=1).reshape(N_ALL, D_MODEL)
    c_all = jnp.concatenate([c, c_ctx[None, :], jnp.zeros((16 - BATCH - 1, D_MODEL), F32)], axis=0)
    mod = _ada_tables(c_all, ada_w, ada_b).reshape(DEPTH, 16, 6, D_MODEL)

    rope_a = _rope_tables(DK_A, 1)
    rope_c = _rope_tables(DH_C, 2)
    dec_a = _retention_tables()

    for l in range(DEPTH):
        last = l == DEPTH - 1
        i = l // 2
        mod_l = mod[l]
        row = lambda v: v[None, :]
        route = _route_params(moe_w_grp[l], moe_b_grp[l], moe_w_rexp[l], moe_b_rexp[l])
        if l % 2 == 0:
            assert not last
            w_in = jnp.pad(ab_w_in[i], ((0, 0), (0, AB_COLS - ab_w_in.shape[2]))).astype(BF16)
            z = _mod_matmul(xs, mod_l, w_in, *rope_a, 2 * H_A, DK_A // 4)
            ya = _scan_a(z, dec_a, row(ab_gn_a[i]))
            wf = ab_w_lr_f[i].reshape(GLA_RANK, H_B // 2, LANES)
            wb = ab_w_lr_b[i].reshape(GLA_RANK, H_B // 2, LANES)
            wlr = jnp.zeros((H_B // 2, LANES, 2 * LANES), F32)
            wlr = wlr.at[:, 0:GLA_RANK, 0:LANES].set(jnp.swapaxes(wf, 0, 1))
            wlr = wlr.at[:, GLA_RANK:2 * GLA_RANK, LANES:].set(jnp.swapaxes(wb, 0, 1))
            blr = jnp.concatenate([ab_b_lr_f[i].reshape(H_B // 2, 1, LANES),
                                   ab_b_lr_b[i].reshape(H_B // 2, 1, LANES)], axis=-1)
            yb = _scan_b(z, wlr, blr, row(ab_gn_b[i]))
            x1, mi, mf, cnt = _proj_ln(ya, yb, 0, ab_w_out[i].astype(BF16), xs, mod_l, row(ln1_g[l]),
                                       row(ln1_b[l]), *route, False)
        else:
            lam_init = 0.8 - 0.6 * math.exp(-0.3 * l)
            lam = (jnp.exp(jnp.sum(c_lq1[i] * c_lk1[i], axis=-1))
                   - jnp.exp(jnp.sum(c_lq2[i] * c_lk2[i], axis=-1))).astype(F32) + lam_init
            z = _mod_matmul(xs, mod_l, c_w_qkv[i].astype(BF16), *rope_c, 2 * H_C, DH_C // 4)
            gsub = row(c_subln_g[i])
            y = _attention(z, lam, gsub, 1.0 - lam_init, last)
            x1, mi, mf, cnt = _proj_ln(y, y, 1, c_w_out[i].astype(BF16), xs, mod_l, row(ln1_g[l]),
                                       row(ln1_b[l]), *route, last)
        xs = _moe(x1, mi, mf, cnt, mod_l, l, moe_w_gate, moe_w_up, moe_w_down, row(ln2_g[l]), row(ln2_b[l]), last)
    return xs.reshape(BATCH, SEQ, D_MODEL)
```

```python
import functools
import math

import numpy as np
import jax
import jax.numpy as jnp
from jax import lax
from jax.experimental import pallas as pl
from jax.experimental.pallas import tpu as pltpu

F32 = jnp.float32
BF16 = jnp.bfloat16

D_MODEL = 1024
BATCH = 8
SEQ = 2048
DEPTH = 4
GRID_W = 64
CTX_LEN = 256
ROPE_BASE = 10000.0
LN_EPS = 1e-5
DEEPNORM_ALPHA = (2 * DEPTH) ** 0.25
H_A = 4
DK_A = 128
DV_A = 128
CHUNK_A = 128
RET_EXP_FWD = 5.0
RET_EXP_BWD = 5.5
H_B = 4
DK_B = 64
DV_B = 128
GLA_RANK = 16
GLA_TAU = 16.0
CHUNK_B = 64
H_C = 8
DH_C = 64
DV_C = 128
N_GROUPS = 4
EXPERTS_PER_GROUP = 8
N_EXPERTS = 32
D_EXPERT = 512

LANES = 128
T_ALL = CTX_LEN + SEQ
N_ALL = BATCH * T_ALL
TM = 256
TILES_PER_BATCH = T_ALL // TM
LATENT_TILES_PER_BATCH = SEQ // TM
AB_COLS = 29 * LANES
MOE_ROWS = 512
PACKED = D_MODEL // 2
ROW_UNROLL = 8
SCAN_UNROLL = 8
GLA_GROUP = 256
TQ = 256
ATTN_KEYS = 256
VMEM_LIMIT = 56 * 1024 * 1024


def _cparams(sem):
    return pltpu.CompilerParams(dimension_semantics=sem, vmem_limit_bytes=VMEM_LIMIT)


def _silu(v):
    return v * (1.0 / (1.0 + jnp.exp(-v)))


def _n_tiles(latent_only):
    return BATCH * (LATENT_TILES_PER_BATCH if latent_only else TILES_PER_BATCH)


def _row_tile(latent_only):
    if latent_only:
        return lambda i: (i // LATENT_TILES_PER_BATCH) * TILES_PER_BATCH + 1 + i % LATENT_TILES_PER_BATCH
    return lambda i: i


def _mod_row(latent_only):
    if latent_only:
        return lambda i: i // LATENT_TILES_PER_BATCH
    return lambda i: jnp.where(i % TILES_PER_BATCH == 0, BATCH, i // TILES_PER_BATCH)


def _ada_kernel(c_ref, w_ref, b_ref, o_ref):
    sc = _silu(c_ref[...])
    o_ref[0] = jnp.dot(sc.astype(BF16), w_ref[0].astype(BF16), preferred_element_type=F32) + b_ref[0]


def _ada_tables(c_all, ada_w, ada_b):
    tn = 1536
    n_out = 6 * D_MODEL
    return pl.pallas_call(
        _ada_kernel,
        grid=(DEPTH, n_out // tn),
        in_specs=[
            pl.BlockSpec((16, D_MODEL), lambda l, j: (0, 0)),
            pl.BlockSpec((1, D_MODEL, tn), lambda l, j: (l, 0, j)),
            pl.BlockSpec((1, 1, tn), lambda l, j: (l, 0, j)),
        ],
        out_specs=pl.BlockSpec((1, 16, tn), lambda l, j: (l, 0, j)),
        out_shape=jax.ShapeDtypeStruct((DEPTH, 16, n_out), F32),
        compiler_params=_cparams(("arbitrary", "arbitrary")),
        name="ada_tables",
    )(c_all, ada_w, ada_b.reshape(DEPTH, 1, n_out))


def _modmm_kernel(x_ref, mod_ref, w_ref, cos_ref, sup_ref, sdn_ref, o_ref, *, rope_blocks, quarter):
    u = x_ref[...] * (1.0 + mod_ref[0, 1:2, :]) + mod_ref[0, 0:1, :]
    z = jnp.dot(u.astype(BF16), w_ref[...], preferred_element_type=F32)
    cos, sup, sdn = cos_ref[...], sup_ref[...], sdn_ref[...]
    for c in range(rope_blocks):
        cols = slice(c * LANES, (c + 1) * LANES)
        o_ref[:, cols] = _rope(z[:, cols], cos, sup, sdn, quarter)
    o_ref[:, rope_blocks * LANES:] = z[:, rope_blocks * LANES:]


def _mod_matmul(x, mod_l, w_bf16, cos, sup, sdn, rope_blocks, quarter):
    n_out = w_bf16.shape[1]
    tbl = pl.BlockSpec((TM, LANES), lambda i: (i % TILES_PER_BATCH, 0))
    return pl.pallas_call(
        functools.partial(_modmm_kernel, rope_blocks=rope_blocks, quarter=quarter),
        grid=(N_ALL // TM,),
        in_specs=[
            pl.BlockSpec((TM, D_MODEL), lambda i: (i, 0)),
            pl.BlockSpec((1, 6, D_MODEL), lambda i: (_mod_row(False)(i), 0, 0)),
            pl.BlockSpec((D_MODEL, n_out), lambda i: (0, 0)),
            tbl, tbl, tbl,
        ],
        out_specs=pl.BlockSpec((TM, n_out), lambda i: (i, 0)),
        out_shape=jax.ShapeDtypeStruct((N_ALL, n_out), F32),
        compiler_params=_cparams(("arbitrary",)),
        name="mod_matmul",
    )(x, mod_l, w_bf16, cos, sup, sdn)


def _rope_tables(head_dim, reps):
    rows = SEQ // GRID_W
    row = np.repeat(np.arange(rows, dtype=np.float32), GRID_W)
    col = np.tile(np.arange(GRID_W, dtype=np.float32), rows)
    quarter = head_dim // 4
    inv = (ROPE_BASE ** (-np.arange(quarter, dtype=np.float32) / quarter)).astype(np.float32)
    ang_r = row[:, None] * inv
    ang_c = col[:, None] * inv
    ang = np.concatenate([ang_r, ang_r, ang_c, ang_c], axis=-1)
    cos = np.cos(ang).astype(np.float32)
    sin = np.sin(ang).astype(np.float32)
    q_idx = (np.arange(head_dim) // quarter) % 2
    sin_up = np.where(q_idx == 1, sin, 0.0).astype(np.float32)
    sin_dn = np.where(q_idx == 0, -sin, 0.0).astype(np.float32)

    def full(t, ctx_val):
        t = np.tile(t, (1, reps))
        return jnp.asarray(np.concatenate([np.full((CTX_LEN, t.shape[1]), ctx_val, np.float32), t], axis=0))

    return full(cos, 1.0), full(sin_up, 0.0), full(sin_dn, 0.0)


def _rope(x, cos, sin_up, sin_dn, quarter):
    width = x.shape[-1]
    return x * cos + pltpu.roll(x, quarter, 1) * sin_up + pltpu.roll(x, width - quarter, 1) * sin_dn


def _dot_tb(a, b):
    return lax.dot_general(a, b, (((1,), (1,)), ((), ())), preferred_element_type=F32)


def _dot_ta(a, b):
    return lax.dot_general(a, b, (((0,), (0,)), ((), ())), preferred_element_type=F32)


def _split_bf16(x, parts):
    out = []
    for _ in range(parts):
        t = x.astype(BF16)
        out.append(t)
        x = x - t.astype(F32)
    return out


def _dot_split(a, b):
    a_hi, a_lo = _split_bf16(a, 2)
    b_hi, b_lo = _split_bf16(b, 2)
    dot = lambda u, v: jnp.dot(u, v, preferred_element_type=F32)
    n = b.shape[1]
    both = dot(a_hi, jnp.concatenate([b_hi, b_lo], axis=1))
    return both[:, :n] + (both[:, n:] + dot(a_lo, b_hi))


def _dot_mask(mask_bf16, x):
    return sum(jnp.dot(mask_bf16, t, preferred_element_type=F32) for t in reversed(_split_bf16(x, 3)))


def _retention_tables():
    c = CHUNK_A
    i = np.arange(c, dtype=np.float64)
    out = np.zeros((H_A, 7, c, LANES), np.float64)
    for h in range(H_A):
        lgf = np.log1p(-np.exp2(-(RET_EXP_FWD + h)))
        lgb = np.log1p(-np.exp2(-(RET_EXP_BWD + h)))
        d = i[:, None] - i[None, :]
        out[h, 0] = np.where(d >= 0, np.exp(lgf * d), np.exp(lgb * (-d - 1)))
        out[h, 1] = np.exp(lgf * (i + 1))[:, None]
        out[h, 2] = np.exp(lgb * (c - 1 - i))[:, None]
        out[h, 3] = np.exp(lgf * (c - 1 - i))[:, None]
        out[h, 4] = np.exp(lgb * i)[:, None]
        out[h, 5] = np.exp(lgf * c)
        out[h, 6] = np.exp(lgb * c)
    return jnp.asarray(out.astype(np.float32))


def _scan_a_kernel(q_ref, k_ref, v_ref, g_ref, dec_ref, gn_ref, o_ref, sb_scr):
    c = CHUNK_A
    n_ctx = CTX_LEN // c
    n_all = T_ALL // c
    scale = DK_A ** -0.5
    dmat = dec_ref[0, 0]
    q_f, q_b, k_f, k_b = dec_ref[0, 1], dec_ref[0, 2], dec_ref[0, 3], dec_ref[0, 4]
    g_fc, g_bc = dec_ref[0, 5], dec_ref[0, 6]
    gn = gn_ref[...]
    zero = jnp.zeros((DK_A, DV_A), F32)

    def chunk(ci):
        return pl.ds(pl.multiple_of(ci * c, c), c)

    def kv_state(ci, k_dec):
        sl = chunk(ci)
        return _dot_ta((k_ref[sl, :] * k_dec).astype(BF16), v_ref[sl, :].astype(BF16))

    def run(lo, hi, sf0, sb0):
        def bwd(j, sb):
            ci = hi - 1 - j
            sb_scr[ci] = sb
            return g_bc * sb + kv_state(ci, k_b)

        sb_fin = lax.fori_loop(0, hi - lo, bwd, sb0, unroll=SCAN_UNROLL)

        def fwd(j, sf):
            ci = lo + j
            sl = chunk(ci)
            q = q_ref[sl, :] * scale
            k = k_ref[sl, :]
            vb = v_ref[sl, :].astype(BF16)
            att = _dot_tb(q.astype(BF16), k.astype(BF16)) * dmat
            o = jnp.dot(att.astype(BF16), vb, preferred_element_type=F32)
            o = o + jnp.dot((q * q_f).astype(BF16), sf.astype(BF16), preferred_element_type=F32)
            o = o + jnp.dot((q * q_b).astype(BF16), sb_scr[ci].astype(BF16), preferred_element_type=F32)
            o = o - jnp.mean(o, axis=-1, keepdims=True)
            o = o * lax.rsqrt(jnp.mean(o * o, axis=-1, keepdims=True) + LN_EPS)
            o_ref[sl, :] = _silu(g_ref[sl, :]) * (o * gn)
            return g_fc * sf + _dot_ta((k * k_f).astype(BF16), vb)

        sf_fin = lax.fori_loop(0, hi - lo, fwd, sf0, unroll=SCAN_UNROLL)
        return sf_fin, sb_fin

    sf_c, sb_c = run(0, n_ctx, zero, zero)
    run(n_ctx, n_all, sf_c, sb_c)


def _scan_a(z, dec, gn_a):
    blk = lambda col0: pl.BlockSpec((T_ALL, LANES), lambda b, h: (b, col0 + h))
    return pl.pallas_call(
        _scan_a_kernel,
        grid=(BATCH, H_A),
        in_specs=[blk(0), blk(4), blk(8), blk(12),
                  pl.BlockSpec((1, 7, CHUNK_A, LANES), lambda b, h: (h, 0, 0, 0)),
                  pl.BlockSpec((1, LANES), lambda b, h: (0, h))],
        out_specs=pl.BlockSpec((T_ALL, LANES), lambda b, h: (b, h)),
        out_shape=jax.ShapeDtypeStruct((N_ALL, H_A * DV_A), F32),
        scratch_shapes=[pltpu.VMEM((T_ALL // CHUNK_A, DK_A, DV_A), F32)],
        compiler_params=_cparams(("arbitrary", "arbitrary")),
        name="scan_retention",
    )(z, z, z, z, dec, gn_a)


def _log_sigmoid(g):
    return jnp.minimum(g, 0.0) - jnp.log1p(jnp.exp(-jnp.abs(g)))


def _scan_b_kernel(q_ref, k_ref, v_ref, g_ref, lr_ref, wlr_ref, blr_ref, gn_ref, o_ref,
                   qf_scr, kf_scr, qb_scr, kb_scr, ktf_scr, ktb_scr, ef_scr, eb_scr, sb_scr):
    c = CHUNK_B
    n_ctx = CTX_LEN // c
    n_all = T_ALL // c
    per_group = GLA_GROUP // c
    scale = DK_B ** -0.5

    gi_r = lax.broadcasted_iota(jnp.int32, (GLA_GROUP, GLA_GROUP), 0)
    gi_c = lax.broadcasted_iota(jnp.int32, (GLA_GROUP, GLA_GROUP), 1)
    same_chunk = (gi_r // c) == (gi_c // c)
    prefix = (same_chunk & (gi_c <= gi_r)).astype(BF16)
    suffix = (same_chunk & (gi_c >= gi_r)).astype(BF16)

    def prepare(gi, carry):
        sl = pl.ds(pl.multiple_of(gi * GLA_GROUP, GLA_GROUP), GLA_GROUP)
        gates = _dot_split(lr_ref[sl, :], wlr_ref[0]) + blr_ref[0]
        laf = _log_sigmoid(gates[:, :LANES]) * (1.0 / GLA_TAU)
        lab = _log_sigmoid(gates[:, LANES:]) * (1.0 / GLA_TAU)
        b = _dot_mask(prefix, laf)
        rb = _dot_mask(suffix, lab)
        q = q_ref[sl, :] * scale
        k = k_ref[sl, :]
        qf_scr[sl, :] = (q * jnp.exp(b)).astype(BF16)
        kf_scr[sl, :] = (k * jnp.exp(-b)).astype(BF16)
        qb_scr[sl, :] = (q * jnp.exp(rb - lab)).astype(BF16)
        kb_scr[sl, :] = (k * jnp.exp(-rb)).astype(BF16)
        b3 = b.reshape(per_group, c, LANES)
        rb3 = rb.reshape(per_group, c, LANES)
        k3 = k.reshape(per_group, c, LANES)
        b_tot = b3[:, c - 1:c, :]
        rb_tot = rb3[:, 0:1, :]
        ktf_scr[sl, :] = (k3 * jnp.exp(b_tot - b3)).reshape(GLA_GROUP, LANES).astype(BF16)
        ktb_scr[sl, :] = (k3 * jnp.exp(rb_tot - rb3)).reshape(GLA_GROUP, LANES).astype(BF16)
        for m in range(per_group):
            ef_scr[gi * per_group + m] = jnp.broadcast_to(jnp.exp(b_tot[m]), (8, LANES))
            eb_scr[gi * per_group + m] = jnp.broadcast_to(jnp.exp(rb_tot[m]), (8, LANES))
        return carry

    lax.fori_loop(0, T_ALL // GLA_GROUP, prepare, 0)

    lane = lax.broadcasted_iota(jnp.int32, (1, LANES), 1)
    masks = [lane < DK_B, lane >= DK_B]
    ri = lax.broadcasted_iota(jnp.int32, (c, c), 0)
    cj = lax.broadcasted_iota(jnp.int32, (c, c), 1)
    lower = cj <= ri
    gn = gn_ref[...]
    zero = jnp.zeros((DV_B, LANES), F32)
    zero_b = jnp.zeros((), BF16)

    def chunk(ci):
        return pl.ds(pl.multiple_of(ci * c, c), c)

    def run(lo, hi, sf0, sb0):
        def bwd(j, sb):
            ci = hi - 1 - j
            sl = chunk(ci)
            e_tot = eb_scr[ci][0:1, :]
            kt = ktb_scr[sl, :]
            v = v_ref[sl, :]
            new = []
            for h in range(2):
                sb_scr[ci, h] = sb[h]
                vh = v[:, h * DV_B:(h + 1) * DV_B].astype(BF16)
                new.append(sb[h] * e_tot + _dot_ta(vh, jnp.where(masks[h], kt, zero_b)))
            return tuple(new)

        sb_fin = lax.fori_loop(0, hi - lo, bwd, sb0, unroll=SCAN_UNROLL)

        def fwd(j, sf):
            ci = lo + j
            sl = chunk(ci)
            e_tot = ef_scr[ci][0:1, :]
            qf, kf, qb, kb, kt = qf_scr[sl, :], kf_scr[sl, :], qb_scr[sl, :], kb_scr[sl, :], ktf_scr[sl, :]
            v = v_ref[sl, :]
            g = g_ref[sl, :]
            new = []
            for h in range(2):
                pick = lambda t: jnp.where(masks[h], t, zero_b)
                vh = v[:, h * DV_B:(h + 1) * DV_B].astype(BF16)
                qfh, qbh = pick(qf), pick(qb)
                att = jnp.where(lower, _dot_tb(qfh, pick(kf)), _dot_tb(qbh, pick(kb)))
                o = jnp.dot(att.astype(BF16), vh, preferred_element_type=F32)
                o = o + _dot_tb(qfh, sf[h].astype(BF16))
                o = o + _dot_tb(qbh, sb_scr[ci, h].astype(BF16))
                o = o * lax.rsqrt(jnp.mean(o * o, axis=-1, keepdims=True) + LN_EPS)
                cols = slice(h * DV_B, (h + 1) * DV_B)
                o_ref[sl, cols] = _silu(g[:, cols]) * (o * gn[:, cols])
                new.append(sf[h] * e_tot + _dot_ta(vh, pick(kt)))
            return tuple(new)

        sf_fin = lax.fori_loop(0, hi - lo, fwd, sf0, unroll=SCAN_UNROLL)
        return sf_fin, sb_fin

    sf_c, sb_c = run(0, n_ctx, (zero, zero), (zero, zero))
    run(n_ctx, n_all, sf_c, sb_c)


def _scan_b(z, wlr, blr, gn_b):
    pairs = H_B // 2
    return pl.pallas_call(
        _scan_b_kernel,
        grid=(BATCH, pairs),
        in_specs=[
            pl.BlockSpec((T_ALL, LANES), lambda b, p: (b, 16 + p)),
            pl.BlockSpec((T_ALL, LANES), lambda b, p: (b, 18 + p)),
            pl.BlockSpec((T_ALL, 2 * DV_B), lambda b, p: (b, 10 + p)),
            pl.BlockSpec((T_ALL, 2 * DV_B), lambda b, p: (b, 12 + p)),
            pl.BlockSpec((T_ALL, LANES), lambda b, p: (b, 28)),
            pl.BlockSpec((1, LANES, 2 * LANES), lambda b, p: (p, 0, 0)),
            pl.BlockSpec((1, 1, 2 * LANES), lambda b, p: (p, 0, 0)),
            pl.BlockSpec((1, 2 * DV_B), lambda b, p: (0, p)),
        ],
        out_specs=pl.BlockSpec((T_ALL, 2 * DV_B), lambda b, p: (b, p)),
        out_shape=jax.ShapeDtypeStruct((N_ALL, H_B * DV_B), F32),
        scratch_shapes=[pltpu.VMEM((T_ALL, LANES), BF16)] * 6
                       + [pltpu.VMEM((T_ALL // CHUNK_B, 8, LANES), F32)] * 2
                       + [pltpu.VMEM((T_ALL // CHUNK_B, 2, DV_B, LANES), F32)],
        compiler_params=_cparams(("arbitrary", "arbitrary")),
        name="scan_gla",
    )(z, z, z, z, z, wlr, blr, gn_b)


def _attn_kernel(lam_ref, q_ref, k_ref, v_ref, gsub_ref, o_ref, k_scr, v_scr, *, post_scale, tile0):
    h = pl.program_id(1)
    t = pl.program_id(2)
    scale = DH_C ** -0.5 * math.log2(math.e)

    @pl.when(t == 0)
    def _():
        k_scr[...] = k_ref[...].astype(BF16)
        v_scr[:, 0:LANES] = v_ref[...].astype(BF16)
        v_scr[:, LANES:] = jnp.ones((T_ALL, LANES), BF16)

    lam = lam_ref[h]
    lane = lax.broadcasted_iota(jnp.int32, (1, LANES), 1)
    m1 = (lane < DH_C).astype(F32)
    m2 = (lane >= DH_C).astype(F32)

    def attend(n_keys):
        q = q_ref[...] * scale
        qs = [(q * m1).astype(BF16), (q * m2).astype(BF16)]
        run_max = [jnp.full((TQ, 1), -jnp.inf, F32) for _ in range(2)]
        acc = [jnp.zeros((TQ, 2 * LANES), F32) for _ in range(2)]
        chunk = min(ATTN_KEYS, n_keys)
        for c in range(n_keys // chunk):
            keys = slice(c * chunk, (c + 1) * chunk)
            kb = k_scr[keys, :]
            vb = v_scr[keys, :]
            for i in range(2):
                s = _dot_tb(qs[i], kb)
                new_max = jnp.maximum(run_max[i], jnp.max(s, axis=-1, keepdims=True))
                p = jnp.exp2(s - new_max).astype(BF16)
                acc[i] = acc[i] * jnp.exp2(run_max[i] - new_max) + jnp.dot(p, vb, preferred_element_type=F32)
                run_max[i] = new_max
        o = acc[0][:, :LANES] / acc[0][:, LANES:] - lam * (acc[1][:, :LANES] / acc[1][:, LANES:])
        o = o * lax.rsqrt(jnp.mean(o * o, axis=-1, keepdims=True) + LN_EPS)
        o_ref[...] = o * (gsub_ref[...] * post_scale)

    if tile0 == 0:
        pl.when(t == 0)(lambda: attend(CTX_LEN))
        pl.when(t > 0)(lambda: attend(T_ALL))
    else:
        attend(T_ALL)


def _attention(z, lam, gsub, post_scale, latent_only):
    tile0 = 1 if latent_only else 0
    n_qt = TILES_PER_BATCH - tile0
    kern = functools.partial(_attn_kernel, post_scale=post_scale, tile0=tile0)
    kv = lambda col0: pl.BlockSpec((T_ALL, LANES), lambda b, h, t, lam_r: (b, col0 + h))
    return pl.pallas_call(
        kern,
        grid_spec=pltpu.PrefetchScalarGridSpec(
            num_scalar_prefetch=1,
            grid=(BATCH, H_C, n_qt),
            in_specs=[pl.BlockSpec((TQ, LANES), lambda b, h, t, lam_r: (b * TILES_PER_BATCH + tile0 + t, h)),
                      kv(H_C), kv(2 * H_C),
                      pl.BlockSpec((1, LANES), lambda b, h, t, lam_r: (0, h))],
            out_specs=pl.BlockSpec((TQ, LANES), lambda b, h, t, lam_r: (b * n_qt + t, h)),
            scratch_shapes=[pltpu.VMEM((T_ALL, LANES), BF16), pltpu.VMEM((T_ALL, 2 * LANES), BF16)],
        ),
        out_shape=jax.ShapeDtypeStruct((BATCH * n_qt * TQ, H_C * DV_C), F32),
        compiler_params=_cparams(("arbitrary", "arbitrary", "arbitrary")),
        name="diff_attention",
    )(lam, z, z, z, gsub)


def _layer_norm(r, g, b):
    mu = jnp.mean(r, axis=-1, keepdims=True)
    d = r - mu
    var = jnp.mean(d * d, axis=-1, keepdims=True)
    return d * lax.rsqrt(var + LN_EPS) * g + b


def _route_tile(u, w_ref, b_ref, mi_ref, mf_ref, cnt_ref, carry):
    @pl.when(pl.program_id(0) == 0)
    def _():
        carry[...] = jnp.zeros_like(carry)

    logits = _dot_split(u, w_ref[...]) + b_ref[...]
    lane = lax.broadcasted_iota(jnp.int32, (TM, LANES), 1)
    lane_f = lane.astype(F32)
    neg = -jnp.inf
    big = 1e9

    gmask = lane < N_GROUPS
    gl = jnp.where(gmask, logits, neg)
    gmax = jnp.max(gl, axis=-1, keepdims=True)
    gidx = jnp.min(jnp.where(gl == gmax, lane_f, big), axis=-1, keepdims=True)
    gw = 1.0 / jnp.sum(jnp.where(gmask, jnp.exp(logits - gmax), 0.0), axis=-1, keepdims=True)

    e_lane = lane - N_GROUPS
    in_grp = (e_lane >= 0) & (e_lane < N_EXPERTS) & ((e_lane >> 3) == gidx.astype(jnp.int32))
    el = jnp.where(in_grp, logits, neg)
    v1 = jnp.max(el, axis=-1, keepdims=True)
    i1 = jnp.min(jnp.where(el == v1, lane_f, big), axis=-1, keepdims=True)
    el2 = jnp.where(lane_f == i1, neg, el)
    v2 = jnp.max(el2, axis=-1, keepdims=True)
    i2 = jnp.min(jnp.where(el2 == v2, lane_f, big), axis=-1, keepdims=True)
    t = jnp.exp(v2 - v1)
    c0 = gw / (1.0 + t)
    c1 = gw * t / (1.0 + t)
    e0 = i1 - N_GROUPS
    e1 = i2 - N_GROUPS

    oh0 = lane_f == e0
    oh1 = lane_f == e1
    cnt = oh0.astype(F32) + oh1.astype(F32)
    ri = lax.broadcasted_iota(jnp.int32, (TM, TM), 0)
    cj = lax.broadcasted_iota(jnp.int32, (TM, TM), 1)
    strict = (cj < ri).astype(BF16)
    before = jnp.dot(strict, cnt.astype(BF16), preferred_element_type=F32) + carry[0:1, :]
    r0 = jnp.sum(jnp.where(oh0, before, 0.0), axis=-1, keepdims=True)
    r1 = jnp.sum(jnp.where(oh1, before, 0.0), axis=-1, keepdims=True)
    carry[0:1, :] = carry[0:1, :] + jnp.sum(cnt, axis=0, keepdims=True)

    mi = jnp.where(lane == 0, e0, jnp.where(lane == 1, e1, jnp.where(lane == 2, r0, jnp.where(lane == 3, r1, 0.0))))
    mi_ref[...] = mi.astype(jnp.int32)
    mf_ref[...] = jnp.where(lane == 0, c0, jnp.where(lane == 1, c1, 0.0))
    cnt_ref[...] = carry[...]


def _proj_ln_kernel(y1_ref, y2_ref, w_ref, x_ref, mod_ref, g_ref, b_ref, wr_ref, br_ref,
                    o_ref, mi_ref, mf_ref, cnt_ref, carry):
    heads = jnp.concatenate([y1_ref[...], y2_ref[...]], axis=1).astype(BF16)
    y = jnp.dot(heads, w_ref[...], preferred_element_type=F32)
    r = DEEPNORM_ALPHA * x_ref[...] + mod_ref[0, 2:3, :] * y
    x1 = _layer_norm(r, g_ref[...], b_ref[...])
    o_ref[...] = x1
    u = x1 * (1.0 + mod_ref[0, 4:5, :]) + mod_ref[0, 3:4, :]
    _route_tile(u, wr_ref, br_ref, mi_ref, mf_ref, cnt_ref, carry)


def _proj_ln(y1, y2, col2, w_out_bf16, x, mod_l, ln_g, ln_b, w_route, b_route, latent_only):
    half = D_MODEL // 2
    rt = _row_tile(latent_only)
    mr = _mod_row(latent_only)
    n_tiles = _n_tiles(latent_only)
    n_tok = n_tiles * TM
    row_blk = lambda w: pl.BlockSpec((TM, w), lambda i: (i, 0))
    const = lambda shape: pl.BlockSpec(shape, lambda i: (0, 0))
    return pl.pallas_call(
        _proj_ln_kernel,
        grid=(n_tiles,),
        in_specs=[
            pl.BlockSpec((TM, half), lambda i: (i, 0)),
            pl.BlockSpec((TM, half), lambda i: (i, col2)),
            const((D_MODEL, D_MODEL)),
            pl.BlockSpec((TM, D_MODEL), lambda i: (rt(i), 0)),
            pl.BlockSpec((1, 6, D_MODEL), lambda i: (mr(i), 0, 0)),
            const((1, D_MODEL)), const((1, D_MODEL)), const((D_MODEL, LANES)), const((1, LANES)),
        ],
        out_specs=[row_blk(D_MODEL), row_blk(LANES), row_blk(LANES), const((8, LANES))],
        out_shape=[jax.ShapeDtypeStruct((n_tok, D_MODEL), F32),
                   jax.ShapeDtypeStruct((n_tok, LANES), jnp.int32),
                   jax.ShapeDtypeStruct((n_tok, LANES), F32),
                   jax.ShapeDtypeStruct((8, LANES), F32)],
        scratch_shapes=[pltpu.VMEM((8, LANES), F32)],
        compiler_params=_cparams(("arbitrary",)),
        name="proj_ln_route",
    )(y1, y2, w_out_bf16, x, mod_l, ln_g, ln_b, w_route, b_route)


def _pack_rows(x):
    half = x.shape[-1] // 2
    bits = lambda t: lax.bitcast_convert_type(t.astype(BF16).astype(F32), jnp.uint32)
    return (bits(x[:, :half]) >> 16) | (bits(x[:, half:]) & jnp.uint32(0xFFFF0000))


def _unpack_rows(w):
    lo = lax.bitcast_convert_type(w << 16, F32)
    hi = lax.bitcast_convert_type(w & jnp.uint32(0xFFFF0000), F32)
    return jnp.concatenate([lo, hi], axis=-1)


def _each_row(fn, per_group=None):
    def body(g, carry):
        if per_group is not None:
            per_group(g)
        for j in range(ROW_UNROLL):
            for k in range(2):
                fn(g, j, k)
        return carry
    lax.fori_loop(0, TM // ROW_UNROLL, body, 0)


def _dispatch_kernel(pad_end_ref, padded_ref, x_ref, xn_ref, mod_ref, modn_ref, dest_hbm, xs_hbm,
                     idx_smem, u_scr, zero_scr, sem_idx, sem_row, sem_zero):
    i = pl.program_id(0)
    per_tile = 2 * TM

    @pl.when(i == 0)
    def _():
        zero_scr[...] = jnp.zeros_like(zero_scr)

        def zero_block(first_row):
            rows = pl.ds(pl.multiple_of(first_row, MOE_ROWS), MOE_ROWS)
            return pltpu.make_async_copy(zero_scr, xs_hbm.at[rows], sem_zero)

        n_rows = xs_hbm.shape[0]
        total = pad_end_ref[N_EXPERTS - 1]
        for e in range(N_EXPERTS):
            pl.when(padded_ref[e] > 0)(lambda e=e: zero_block(pad_end_ref[e] - MOE_ROWS).start())
            pl.when(total + e * MOE_ROWS < n_rows)(lambda e=e: zero_block(total + e * MOE_ROWS).start())
        for e in range(N_EXPERTS):
            pl.when(padded_ref[e] > 0)(lambda e=e: zero_block(0).wait())
            pl.when(total + e * MOE_ROWS < n_rows)(lambda e=e: zero_block(0).wait())

    def idx_copy(tile, sl):
        return pltpu.make_async_copy(dest_hbm.at[pl.ds(tile * per_tile, per_tile)],
                                     idx_smem.at[pl.ds(sl * per_tile, per_tile)], sem_idx)

    n = pl.num_programs(0)
    half = i % 2
    pl.when(i == 0)(lambda: idx_copy(0, 0).start())
    idx_copy(i, half).wait()
    pl.when(i + 1 < n)(lambda: idx_copy(i + 1, 1 - half).start())

    def packed(x, m_ref):
        return _pack_rows(x * (1.0 + m_ref[0, 4:5, :]) + m_ref[0, 3:4, :])

    @pl.when(i == 0)
    def _():
        u_scr[0] = packed(x_ref[...], mod_ref).reshape(TM // ROW_UNROLL, ROW_UNROLL, PACKED)

    slot = i % 3
    nxt = (i + 1) % 3
    prv = (i + 2) % 3

    def row_copy(sl, g, j, dst_row):
        return pltpu.make_async_copy(u_scr.at[sl, g, pl.ds(j, 1)], xs_hbm.at[pl.ds(dst_row, 1)], sem_row.at[sl])

    def send(g, j, k):
        row_copy(slot, g, j, idx_smem[half * per_tile + g * (2 * ROW_UNROLL) + (2 * j + k)]).start(priority=k)

    def pack_next(g):
        rows = pl.ds(pl.multiple_of(g * ROW_UNROLL, ROW_UNROLL), ROW_UNROLL)
        u_scr[nxt, g] = packed(xn_ref[rows, :], modn_ref)

    pl.when(i + 1 < n)(lambda: _each_row(send, pack_next))
    pl.when(i + 1 >= n)(lambda: _each_row(send))

    @pl.when(i > 0)
    def _():
        _each_row(lambda g, j, k: row_copy(prv, g, j, 0).wait())

    @pl.when(i == n - 1)
    def _():
        _each_row(lambda g, j, k: row_copy(slot, g, j, 0).wait())


def _dispatch(pad_end, padded, x1, mod_l, dest, n_blocks, latent_only):
    mr = _mod_row(latent_only)
    n_tiles = _n_tiles(latent_only)
    nxt = lambda i: jnp.minimum(i + 1, n_tiles - 1)
    return pl.pallas_call(
        _dispatch_kernel,
        grid_spec=pltpu.PrefetchScalarGridSpec(
            num_scalar_prefetch=2,
            grid=(n_tiles,),
            in_specs=[
                pl.BlockSpec((TM, D_MODEL), lambda i, pe, pd: (i, 0)),
                pl.BlockSpec((TM, D_MODEL), lambda i, pe, pd: (nxt(i), 0)),
                pl.BlockSpec((1, 6, D_MODEL), lambda i, pe, pd: (mr(i), 0, 0)),
                pl.BlockSpec((1, 6, D_MODEL), lambda i, pe, pd: (mr(nxt(i)), 0, 0)),
                pl.BlockSpec(memory_space=pl.ANY),
            ],
            out_specs=pl.BlockSpec(memory_space=pl.ANY),
            scratch_shapes=[
                pltpu.SMEM((2 * 2 * TM,), jnp.int32),
                pltpu.VMEM((3, TM // ROW_UNROLL, ROW_UNROLL, PACKED), jnp.uint32),
                pltpu.VMEM((MOE_ROWS, PACKED), jnp.uint32),
                pltpu.SemaphoreType.DMA(()),
                pltpu.SemaphoreType.DMA((3,)),
                pltpu.SemaphoreType.DMA(()),
            ],
        ),
        out_shape=jax.ShapeDtypeStruct((n_blocks * MOE_ROWS, PACKED), jnp.uint32),
        compiler_params=_cparams(("arbitrary",)),
        name="moe_dispatch",
    )(pad_end, padded, x1, x1, mod_l, mod_l, dest)


def _expert_kernel(blk_exp_ref, n_used_ref, x_ref, wg_ref, wu_ref, wd_ref, o_ref):
    used = pl.program_id(0) < n_used_ref[0]

    @pl.when(used)
    def _():
        x = _unpack_rows(x_ref[...]).astype(BF16)
        gate = jnp.dot(x, wg_ref[0, 0].astype(BF16), preferred_element_type=F32)
        up = jnp.dot(x, wu_ref[0, 0].astype(BF16), preferred_element_type=F32)
        hid = (_silu(gate) * up).astype(BF16)
        o_ref[...] = _pack_rows(jnp.dot(hid, wd_ref[0, 0].astype(BF16), preferred_element_type=F32))

    @pl.when(jnp.logical_not(used))
    def _():
        o_ref[...] = jnp.zeros_like(o_ref)


def _experts(blk_exp, n_used, xs, layer, w_gate, w_up, w_down):
    n_blocks = xs.shape[0] // MOE_ROWS
    row_in = pl.BlockSpec((MOE_ROWS, PACKED), lambda i, be, nu: (jnp.minimum(i, nu[0] - 1), 0))
    w_in = pl.BlockSpec((1, 1, D_MODEL, D_EXPERT), lambda i, be, nu: (layer, be[i], 0, 0))
    w_out = pl.BlockSpec((1, 1, D_EXPERT, D_MODEL), lambda i, be, nu: (layer, be[i], 0, 0))
    return pl.pallas_call(
        _expert_kernel,
        grid_spec=pltpu.PrefetchScalarGridSpec(
            num_scalar_prefetch=2,
            grid=(n_blocks,),
            in_specs=[row_in, w_in, w_in, w_out],
            out_specs=pl.BlockSpec((MOE_ROWS, PACKED), lambda i, be, nu: (i, 0)),
        ),
        out_shape=jax.ShapeDtypeStruct(xs.shape, jnp.uint32),
        compiler_params=_cparams(("arbitrary",)),
        name="moe_experts",
    )(blk_exp, n_used, xs, w_gate, w_up, w_down)


def _combine_ln_kernel(ys_hbm, dest_hbm, mf_ref, x_ref, mod_ref, g_ref, b_ref, o_ref,
                       idx_smem, y_buf, sem_idx, sem_row):
    i = pl.program_id(0)
    slot = i % 2
    per_tile = 2 * TM

    def row_copy(sl, g, j, k, src_row):
        return pltpu.make_async_copy(ys_hbm.at[pl.ds(src_row, 1)], y_buf.at[sl, k, g, pl.ds(j, 1)], sem_row.at[sl])

    def idx_copy(tile, sl):
        return pltpu.make_async_copy(dest_hbm.at[pl.ds(tile * per_tile, per_tile)],
                                     idx_smem.at[pl.ds(sl * per_tile, per_tile)], sem_idx)

    def request(sl):
        _each_row(lambda g, j, k: row_copy(
            sl, g, j, k, idx_smem[sl * per_tile + g * (2 * ROW_UNROLL) + (2 * j + k)]).start(priority=k))

    n = pl.num_programs(0)

    @pl.when(i == 0)
    def _():
        idx_copy(0, 0).start()
        idx_copy(0, 0).wait()
        request(0)
        pl.when(n > 1)(lambda: idx_copy(1, 1).start())

    @pl.when(i + 1 < n)
    def _():
        idx_copy(i + 1, 1 - slot).wait()
        request(1 - slot)
        pl.when(i + 2 < n)(lambda: idx_copy(i + 2, slot).start())

    _each_row(lambda g, j, k: row_copy(slot, g, j, k, 0).wait())

    mf = mf_ref[...]
    y0 = _unpack_rows(y_buf[slot, 0].reshape(TM, PACKED))
    y1 = _unpack_rows(y_buf[slot, 1].reshape(TM, PACKED))
    y = mf[:, 0:1] * y0 + mf[:, 1:2] * y1
    r = DEEPNORM_ALPHA * x_ref[...] + mod_ref[0, 5:6, :] * y
    o_ref[...] = _layer_norm(r, g_ref[...], b_ref[...])


def _combine_ln(ys, dest, mf, x1, mod_l, ln_g, ln_b, latent_only):
    mr = _mod_row(latent_only)
    n_tiles = _n_tiles(latent_only)
    return pl.pallas_call(
        _combine_ln_kernel,
        grid=(n_tiles,),
        in_specs=[
            pl.BlockSpec(memory_space=pl.ANY),
            pl.BlockSpec(memory_space=pl.ANY),
            pl.BlockSpec((TM, LANES), lambda i: (i, 0)),
            pl.BlockSpec((TM, D_MODEL), lambda i: (i, 0)),
            pl.BlockSpec((1, 6, D_MODEL), lambda i: (mr(i), 0, 0)),
            pl.BlockSpec((1, D_MODEL), lambda i: (0, 0)),
            pl.BlockSpec((1, D_MODEL), lambda i: (0, 0)),
        ],
        out_specs=pl.BlockSpec((TM, D_MODEL), lambda i: (i, 0)),
        out_shape=jax.ShapeDtypeStruct((n_tiles * TM, D_MODEL), F32),
        scratch_shapes=[
            pltpu.SMEM((2 * 2 * TM,), jnp.int32),
            pltpu.VMEM((2, 2, TM // ROW_UNROLL, ROW_UNROLL, PACKED), jnp.uint32),
            pltpu.SemaphoreType.DMA(()),
            pltpu.SemaphoreType.DMA((2,)),
        ],
        compiler_params=_cparams(("arbitrary",)),
        name="combine_ln",
    )(ys, dest, mf, x1, mod_l, ln_g, ln_b)


def _route_params(w_grp, b_grp, w_rexp, b_rexp):
    pad = LANES - N_GROUPS - N_EXPERTS
    w_route = jnp.concatenate([w_grp, w_rexp, jnp.zeros((D_MODEL, pad), F32)], axis=1)
    b_route = jnp.concatenate([b_grp, b_rexp, jnp.zeros((pad,), F32)])[None, :]
    return w_route, b_route


def _moe(x1, mi, mf, cnt, mod_l, layer, w_gate, w_up, w_down, ln_g, ln_b, latent_only):
    counts = cnt[0, :N_EXPERTS].astype(jnp.int32)
    padded = (counts + MOE_ROWS - 1) // MOE_ROWS * MOE_ROWS
    pad_end = jnp.cumsum(padded)
    pad_start = pad_end - padded
    n_tok = x1.shape[0]
    n_blocks = (2 * n_tok) // MOE_ROWS + N_EXPERTS
    experts = jnp.arange(N_EXPERTS, dtype=jnp.int32)
    start_of = jnp.sum(jnp.where(mi[:, 0:2, None] == experts, pad_start, 0), axis=-1)
    dest = (start_of + mi[:, 2:4]).reshape(-1)
    blk_start = jnp.arange(n_blocks, dtype=jnp.int32) * MOE_ROWS
    blk_exp = jnp.minimum(jnp.sum((pad_end[None, :] <= blk_start[:, None]).astype(jnp.int32), axis=1),
                          N_EXPERTS - 1)
    n_used = pad_end[-1:] // MOE_ROWS
    xs = _dispatch(pad_end, padded, x1, mod_l, dest, n_blocks, latent_only)
    ys = _experts(blk_exp, n_used, xs, layer, w_gate, w_up, w_down)
    return _combine_ln(ys, dest, mf, x1, mod_l, ln_g, ln_b, latent_only)


def kernel(x, c, ctx, c_ctx, ada_w, ada_b, ln1_g, ln1_b, ln2_g, ln2_b, ab_w_in, ab_w_lr_f, ab_b_lr_f, ab_w_lr_b, ab_b_lr_b, ab_gn_a, ab_gn_b, ab_w_out, c_w_qkv, c_lq1, c_lk1, c_lq2, c_lk2, c_subln_g, c_w_out, moe_w_grp, moe_b_grp, moe_w_rexp, moe_b_rexp, moe_w_gate, moe_w_up, moe_w_down):
    assert x.shape == (BATCH, SEQ, D_MODEL) and ctx.shape == (BATCH, CTX_LEN, D_MODEL)
    xs = jnp.concatenate([ctx, x], axis=1).reshape(N_ALL, D_MODEL)
    c_all = jnp.concatenate([c, c_ctx[None, :], jnp.zeros((16 - BATCH - 1, D_MODEL), F32)], axis=0)
    mod = _ada_tables(c_all, ada_w, ada_b).reshape(DEPTH, 16, 6, D_MODEL)

    rope_a = _rope_tables(DK_A, 1)
    rope_c = _rope_tables(DH_C, 2)
    dec_a = _retention_tables()

    for l in range(DEPTH):
        last = l == DEPTH - 1
        i = l // 2
        mod_l = mod[l]
        row = lambda v: v[None, :]
        route = _route_params(moe_w_grp[l], moe_b_grp[l], moe_w_rexp[l], moe_b_rexp[l])
        if l % 2 == 0:
            assert not last
            w_in = jnp.pad(ab_w_in[i], ((0, 0), (0, AB_COLS - ab_w_in.shape[2]))).astype(BF16)
            z = _mod_matmul(xs, mod_l, w_in, *rope_a, 2 * H_A, DK_A // 4)
            ya = _scan_a(z, dec_a, row(ab_gn_a[i]))
            wf = ab_w_lr_f[i].reshape(GLA_RANK, H_B // 2, LANES)
            wb = ab_w_lr_b[i].reshape(GLA_RANK, H_B // 2, LANES)
            wlr = jnp.zeros((H_B // 2, LANES, 2 * LANES), F32)
            wlr = wlr.at[:, 0:GLA_RANK, 0:LANES].set(jnp.swapaxes(wf, 0, 1))
            wlr = wlr.at[:, GLA_RANK:2 * GLA_RANK, LANES:].set(jnp.swapaxes(wb, 0, 1))
            blr = jnp.concatenate([ab_b_lr_f[i].reshape(H_B // 2, 1, LANES),
                                   ab_b_lr_b[i].reshape(H_B // 2, 1, LANES)], axis=-1)
            yb = _scan_b(z, wlr, blr, row(ab_gn_b[i]))
            x1, mi, mf, cnt = _proj_ln(ya, yb, 0, ab_w_out[i].astype(BF16), xs, mod_l, row(ln1_g[l]),
                                       row(ln1_b[l]), *route, False)
        else:
            lam_init = 0.8 - 0.6 * math.exp(-0.3 * l)
            lam = (jnp.exp(jnp.sum(c_lq1[i] * c_lk1[i], axis=-1))
                   - jnp.exp(jnp.sum(c_lq2[i] * c_lk2[i], axis=-1))).astype(F32) + lam_init
            z = _mod_matmul(xs, mod_l, c_w_qkv[i].astype(BF16), *rope_c, 2 * H_C, DH_C // 4)
            gsub = row(c_subln_g[i])
            y = _attention(z, lam, gsub, 1.0 - lam_init, last)
            x1, mi, mf, cnt = _proj_ln(y, y, 1, c_w_out[i].astype(BF16), xs, mod_l, row(ln1_g[l]),
                                       row(ln1_b[l]), *route, last)
        xs = _moe(x1, mi, mf, cnt, mod_l, l, moe_w_gate, moe_w_up, moe_w_down, row(ln2_g[l]), row(ln2_b[l]), last)
    return xs.reshape(BATCH, SEQ, D_MODEL)
```

```python
import functools
import math

import numpy as np
import jax
import jax.numpy as jnp
from jax import lax
from jax.experimental import pallas as pl
from jax.experimental.pallas import tpu as pltpu

F32 = jnp.float32
BF16 = jnp.bfloat16

D_MODEL = 1024
BATCH = 8
SEQ = 2048
DEPTH = 4
GRID_W = 64
CTX_LEN = 256
ROPE_BASE = 10000.0
LN_EPS = 1e-5
DEEPNORM_ALPHA = (2 * DEPTH) ** 0.25
H_A = 4
DK_A = 128
DV_A = 128
CHUNK_A = 128
RET_EXP_FWD = 5.0
RET_EXP_BWD = 5.5
H_B = 4
DK_B = 64
DV_B = 128
GLA_RANK = 16
GLA_TAU = 16.0
CHUNK_B = 64
H_C = 8
DH_C = 64
DV_C = 128
N_GROUPS = 4
EXPERTS_PER_GROUP = 8
N_EXPERTS = 32
D_EXPERT = 512

LANES = 128
T_ALL = CTX_LEN + SEQ
N_ALL = BATCH * T_ALL
TM = 256
TILES_PER_BATCH = T_ALL // TM
LATENT_TILES_PER_BATCH = SEQ // TM
AB_COLS = 29 * LANES
MOE_ROWS = 512
PACKED = D_MODEL // 2
ROW_UNROLL = 8
SCAN_UNROLL = 8
GLA_GROUP = 256
TQ = 256
ATTN_KEYS = 256
VMEM_LIMIT = 56 * 1024 * 1024


def _cparams(sem):
    return pltpu.CompilerParams(dimension_semantics=sem, vmem_limit_bytes=VMEM_LIMIT)


def _silu(v):
    return v * (1.0 / (1.0 + jnp.exp(-v)))


def _n_tiles(latent_only):
    return BATCH * (LATENT_TILES_PER_BATCH if latent_only else TILES_PER_BATCH)


def _row_tile(latent_only):
    if latent_only:
        return lambda i: (i // LATENT_TILES_PER_BATCH) * TILES_PER_BATCH + 1 + i % LATENT_TILES_PER_BATCH
    return lambda i: i


def _mod_row(latent_only):
    if latent_only:
        return lambda i: i // LATENT_TILES_PER_BATCH
    return lambda i: jnp.where(i % TILES_PER_BATCH == 0, BATCH, i // TILES_PER_BATCH)


def _ada_kernel(c_ref, w_ref, b_ref, o_ref):
    sc = _silu(c_ref[...])
    o_ref[0] = jnp.dot(sc.astype(BF16), w_ref[0].astype(BF16), preferred_element_type=F32) + b_ref[0]


def _ada_tables(c_all, ada_w, ada_b):
    tn = 1536
    n_out = 6 * D_MODEL
    return pl.pallas_call(
        _ada_kernel,
        grid=(DEPTH, n_out // tn),
        in_specs=[
            pl.BlockSpec((16, D_MODEL), lambda l, j: (0, 0)),
            pl.BlockSpec((1, D_MODEL, tn), lambda l, j: (l, 0, j)),
            pl.BlockSpec((1, 1, tn), lambda l, j: (l, 0, j)),
        ],
        out_specs=pl.BlockSpec((1, 16, tn), lambda l, j: (l, 0, j)),
        out_shape=jax.ShapeDtypeStruct((DEPTH, 16, n_out), F32),
        compiler_params=_cparams(("arbitrary", "arbitrary")),
        name="ada_tables",
    )(c_all, ada_w, ada_b.reshape(DEPTH, 1, n_out))


def _modmm_kernel(x_ref, mod_ref, w_ref, cos_ref, sup_ref, sdn_ref, o_ref, *, rope_blocks, quarter):
    u = x_ref[...] * (1.0 + mod_ref[0, 1:2, :]) + mod_ref[0, 0:1, :]
    z = jnp.dot(u.astype(BF16), w_ref[...], preferred_element_type=F32)
    cos, sup, sdn = cos_ref[...], sup_ref[...], sdn_ref[...]
    for c in range(rope_blocks):
        cols = slice(c * LANES, (c + 1) * LANES)
        o_ref[:, cols] = _rope(z[:, cols], cos, sup, sdn, quarter)
    o_ref[:, rope_blocks * LANES:] = z[:, rope_blocks * LANES:]


def _mod_matmul(x, mod_l, w_bf16, cos, sup, sdn, rope_blocks, quarter):
    n_out = w_bf16.shape[1]
    tbl = pl.BlockSpec((TM, LANES), lambda i: (i % TILES_PER_BATCH, 0))
    return pl.pallas_call(
        functools.partial(_modmm_kernel, rope_blocks=rope_blocks, quarter=quarter),
        grid=(N_ALL // TM,),
        in_specs=[
            pl.BlockSpec((TM, D_MODEL), lambda i: (i, 0)),
            pl.BlockSpec((1, 6, D_MODEL), lambda i: (_mod_row(False)(i), 0, 0)),
            pl.BlockSpec((D_MODEL, n_out), lambda i: (0, 0)),
            tbl, tbl, tbl,
        ],
        out_specs=pl.BlockSpec((TM, n_out), lambda i: (i, 0)),
        out_shape=jax.ShapeDtypeStruct((N_ALL, n_out), F32),
        compiler_params=_cparams(("arbitrary",)),
        name="mod_matmul",
    )(x, mod_l, w_bf16, cos, sup, sdn)


def _rope_tables(head_dim, reps):
    rows = SEQ // GRID_W
    row = np.repeat(np.arange(rows, dtype=np.float32), GRID_W)
    col = np.tile(np.arange(GRID_W, dtype=np.float32), rows)
    quarter = head_dim // 4
    inv = (ROPE_BASE ** (-np.arange(quarter, dtype=np.float32) / quarter)).astype(np.float32)
    ang_r = row[:, None] * inv
    ang_c = col[:, None] * inv
    ang = np.concatenate([ang_r, ang_r, ang_c, ang_c], axis=-1)
    cos = np.cos(ang).astype(np.float32)
    sin = np.sin(ang).astype(np.float32)
    q_idx = (np.arange(head_dim) // quarter) % 2
    sin_up = np.where(q_idx == 1, sin, 0.0).astype(np.float32)
    sin_dn = np.where(q_idx == 0, -sin, 0.0).astype(np.float32)

    def full(t, ctx_val):
        t = np.tile(t, (1, reps))
        return jnp.asarray(np.concatenate([np.full((CTX_LEN, t.shape[1]), ctx_val, np.float32), t], axis=0))

    return full(cos, 1.0), full(sin_up, 0.0), full(sin_dn, 0.0)


def _rope(x, cos, sin_up, sin_dn, quarter):
    width = x.shape[-1]
    return x * cos + pltpu.roll(x, quarter, 1) * sin_up + pltpu.roll(x, width - quarter, 1) * sin_dn


def _dot_tb(a, b):
    return lax.dot_general(a, b, (((1,), (1,)), ((), ())), preferred_element_type=F32)


def _dot_ta(a, b):
    return lax.dot_general(a, b, (((0,), (0,)), ((), ())), preferred_element_type=F32)


def _split_bf16(x, parts):
    out = []
    for _ in range(parts):
        t = x.astype(BF16)
        out.append(t)
        x = x - t.astype(F32)
    return out


def _dot_split(a, b):
    a_hi, a_lo = _split_bf16(a, 2)
    b_hi, b_lo = _split_bf16(b, 2)
    dot = lambda u, v: jnp.dot(u, v, preferred_element_type=F32)
    n = b.shape[1]
    both = dot(a_hi, jnp.concatenate([b_hi, b_lo], axis=1))
    return both[:, :n] + (both[:, n:] + dot(a_lo, b_hi))


def _dot_mask(mask_bf16, x):
    return sum(jnp.dot(mask_bf16, t, preferred_element_type=F32) for t in reversed(_split_bf16(x, 3)))


def _retention_tables():
    c = CHUNK_A
    i = np.arange(c, dtype=np.float64)
    out = np.zeros((H_A, 7, c, LANES), np.float64)
    for h in range(H_A):
        lgf = np.log1p(-np.exp2(-(RET_EXP_FWD + h)))
        lgb = np.log1p(-np.exp2(-(RET_EXP_BWD + h)))
        d = i[:, None] - i[None, :]
        out[h, 0] = np.where(d >= 0, np.exp(lgf * d), np.exp(lgb * (-d - 1)))
        out[h, 1] = np.exp(lgf * (i + 1))[:, None]
        out[h, 2] = np.exp(lgb * (c - 1 - i))[:, None]
        out[h, 3] = np.exp(lgf * (c - 1 - i))[:, None]
        out[h, 4] = np.exp(lgb * i)[:, None]
        out[h, 5] = np.exp(lgf * c)
        out[h, 6] = np.exp(lgb * c)
    return jnp.asarray(out.astype(np.float32))


def _scan_a_kernel(q_ref, k_ref, v_ref, g_ref, dec_ref, gn_ref, o_ref, sb_scr):
    c = CHUNK_A
    n_ctx = CTX_LEN // c
    n_all = T_ALL // c
    scale = DK_A ** -0.5
    dmat = dec_ref[0, 0]
    q_f, q_b, k_f, k_b = dec_ref[0, 1], dec_ref[0, 2], dec_ref[0, 3], dec_ref[0, 4]
    g_fc, g_bc = dec_ref[0, 5], dec_ref[0, 6]
    gn = gn_ref[...]
    zero = jnp.zeros((DK_A, DV_A), F32)

    def chunk(ci):
        return pl.ds(pl.multiple_of(ci * c, c), c)

    def kv_state(ci, k_dec):
        sl = chunk(ci)
        return _dot_ta((k_ref[sl, :] * k_dec).astype(BF16), v_ref[sl, :].astype(BF16))

    def run(lo, hi, sf0, sb0):
        def bwd(j, sb):
            ci = hi - 1 - j
            sb_scr[ci] = sb
            return g_bc * sb + kv_state(ci, k_b)

        sb_fin = lax.fori_loop(0, hi - lo, bwd, sb0, unroll=SCAN_UNROLL)

        def fwd(j, sf):
            ci = lo + j
            sl = chunk(ci)
            q = q_ref[sl, :] * scale
            k = k_ref[sl, :]
            vb = v_ref[sl, :].astype(BF16)
            att = _dot_tb(q.astype(BF16), k.astype(BF16)) * dmat
            o = jnp.dot(att.astype(BF16), vb, preferred_element_type=F32)
            o = o + jnp.dot((q * q_f).astype(BF16), sf.astype(BF16), preferred_element_type=F32)
            o = o + jnp.dot((q * q_b).astype(BF16), sb_scr[ci].astype(BF16), preferred_element_type=F32)
            o = o - jnp.mean(o, axis=-1, keepdims=True)
            o = o * lax.rsqrt(jnp.mean(o * o, axis=-1, keepdims=True) + LN_EPS)
            o_ref[sl, :] = _silu(g_ref[sl, :]) * (o * gn)
            return g_fc * sf + _dot_ta((k * k_f).astype(BF16), vb)

        sf_fin = lax.fori_loop(0, hi - lo, fwd, sf0, unroll=SCAN_UNROLL)
        return sf_fin, sb_fin

    sf_c, sb_c = run(0, n_ctx, zero, zero)
    run(n_ctx, n_all, sf_c, sb_c)


def _scan_a(z, dec, gn_a):
    blk = lambda col0: pl.BlockSpec((T_ALL, LANES), lambda b, h: (b, col0 + h))
    return pl.pallas_call(
        _scan_a_kernel,
        grid=(BATCH, H_A),
        in_specs=[blk(0), blk(4), blk(8), blk(12),
                  pl.BlockSpec((1, 7, CHUNK_A, LANES), lambda b, h: (h, 0, 0, 0)),
                  pl.BlockSpec((1, LANES), lambda b, h: (0, h))],
        out_specs=pl.BlockSpec((T_ALL, LANES), lambda b, h: (b, h)),
        out_shape=jax.ShapeDtypeStruct((N_ALL, H_A * DV_A), F32),
        scratch_shapes=[pltpu.VMEM((T_ALL // CHUNK_A, DK_A, DV_A), F32)],
        compiler_params=_cparams(("arbitrary", "arbitrary")),
        name="scan_retention",
    )(z, z, z, z, dec, gn_a)


def _log_sigmoid(g):
    return jnp.minimum(g, 0.0) - jnp.log1p(jnp.exp(-jnp.abs(g)))


def _scan_b_kernel(q_ref, k_ref, v_ref, g_ref, lr_ref, wlr_ref, blr_ref, gn_ref, o_ref,
                   qf_scr, kf_scr, qb_scr, kb_scr, ktf_scr, ktb_scr, ef_scr, eb_scr, sb_scr):
    c = CHUNK_B
    n_ctx = CTX_LEN // c
    n_all = T_ALL // c
    per_group = GLA_GROUP // c
    scale = DK_B ** -0.5

    gi_r = lax.broadcasted_iota(jnp.int32, (GLA_GROUP, GLA_GROUP), 0)
    gi_c = lax.broadcasted_iota(jnp.int32, (GLA_GROUP, GLA_GROUP), 1)
    same_chunk = (gi_r // c) == (gi_c // c)
    prefix = (same_chunk & (gi_c <= gi_r)).astype(BF16)
    suffix = (same_chunk & (gi_c >= gi_r)).astype(BF16)

    def prepare(gi, carry):
        sl = pl.ds(pl.multiple_of(gi * GLA_GROUP, GLA_GROUP), GLA_GROUP)
        gates = _dot_split(lr_ref[sl, :], wlr_ref[0]) + blr_ref[0]
        laf = _log_sigmoid(gates[:, :LANES]) * (1.0 / GLA_TAU)
        lab = _log_sigmoid(gates[:, LANES:]) * (1.0 / GLA_TAU)
        b = _dot_mask(prefix, laf)
        rb = _dot_mask(suffix, lab)
        q = q_ref[sl, :] * scale
        k = k_ref[sl, :]
        qf_scr[sl, :] = (q * jnp.exp(b)).astype(BF16)
        kf_scr[sl, :] = (k * jnp.exp(-b)).astype(BF16)
        qb_scr[sl, :] = (q * jnp.exp(rb - lab)).astype(BF16)
        kb_scr[sl, :] = (k * jnp.exp(-rb)).astype(BF16)
        b3 = b.reshape(per_group, c, LANES)
        rb3 = rb.reshape(per_group, c, LANES)
        k3 = k.reshape(per_group, c, LANES)
        b_tot = b3[:, c - 1:c, :]
        rb_tot = rb3[:, 0:1, :]
        ktf_scr[sl, :] = (k3 * jnp.exp(b_tot - b3)).reshape(GLA_GROUP, LANES).astype(BF16)
        ktb_scr[sl, :] = (k3 * jnp.exp(rb_tot - rb3)).reshape(GLA_GROUP, LANES).astype(BF16)
        for m in range(per_group):
            ef_scr[gi * per_group + m] = jnp.broadcast_to(jnp.exp(b_tot[m]), (8, LANES))
            eb_scr[gi * per_group + m] = jnp.broadcast_to(jnp.exp(rb_tot[m]), (8, LANES))
        return carry

    lax.fori_loop(0, T_ALL // GLA_GROUP, prepare, 0)

    lane = lax.broadcasted_iota(jnp.int32, (1, LANES), 1)
    masks = [lane < DK_B, lane >= DK_B]
    ri = lax.broadcasted_iota(jnp.int32, (c, c), 0)
    cj = lax.broadcasted_iota(jnp.int32, (c, c), 1)
    lower = cj <= ri
    gn = gn_ref[...]
    zero = jnp.zeros((DV_B, LANES), F32)
    zero_b = jnp.zeros((), BF16)

    def chunk(ci):
        return pl.ds(pl.multiple_of(ci * c, c), c)

    def run(lo, hi, sf0, sb0):
        def bwd(j, sb):
            ci = hi - 1 - j
            sl = chunk(ci)
            e_tot = eb_scr[ci][0:1, :]
            kt = ktb_scr[sl, :]
            v = v_ref[sl, :]
            new = []
            for h in range(2):
                sb_scr[ci, h] = sb[h]
                vh = v[:, h * DV_B:(h + 1) * DV_B].astype(BF16)
                new.append(sb[h] * e_tot + _dot_ta(vh, jnp.where(masks[h], kt, zero_b)))
            return tuple(new)

        sb_fin = lax.fori_loop(0, hi - lo, bwd, sb0, unroll=SCAN_UNROLL)

        def fwd(j, sf):
            ci = lo + j
            sl = chunk(ci)
            e_tot = ef_scr[ci][0:1, :]
            qf, kf, qb, kb, kt = qf_scr[sl, :], kf_scr[sl, :], qb_scr[sl, :], kb_scr[sl, :], ktf_scr[sl, :]
            v = v_ref[sl, :]
            g = g_ref[sl, :]
            new = []
            for h in range(2):
                pick = lambda t: jnp.where(masks[h], t, zero_b)
                vh = v[:, h * DV_B:(h + 1) * DV_B].astype(BF16)
                qfh, qbh = pick(qf), pick(qb)
                att = jnp.where(lower, _dot_tb(qfh, pick(kf)), _dot_tb(qbh, pick(kb)))
                o = jnp.dot(att.astype(BF16), vh, preferred_element_type=F32)
                o = o + _dot_tb(qfh, sf[h].astype(BF16))
                o = o + _dot_tb(qbh, sb_scr[ci, h].astype(BF16))
                o = o * lax.rsqrt(jnp.mean(o * o, axis=-1, keepdims=True) + LN_EPS)
                cols = slice(h * DV_B, (h + 1) * DV_B)
                o_ref[sl, cols] = _silu(g[:, cols]) * (o * gn[:, cols])
                new.append(sf[h] * e_tot + _dot_ta(vh, pick(kt)))
            return tuple(new)

        sf_fin = lax.fori_loop(0, hi - lo, fwd, sf0, unroll=SCAN_UNROLL)
        return sf_fin, sb_fin

    sf_c, sb_c = run(0, n_ctx, (zero, zero), (zero, zero))
    run(n_ctx, n_all, sf_c, sb_c)


def _scan_b(z, wlr, blr, gn_b):
    pairs = H_B // 2
    return pl.pallas_call(
        _scan_b_kernel,
        grid=(BATCH, pairs),
        in_specs=[
            pl.BlockSpec((T_ALL, LANES), lambda b, p: (b, 16 + p)),
            pl.BlockSpec((T_ALL, LANES), lambda b, p: (b, 18 + p)),
            pl.BlockSpec((T_ALL, 2 * DV_B), lambda b, p: (b, 10 + p)),
            pl.BlockSpec((T_ALL, 2 * DV_B), lambda b, p: (b, 12 + p)),
            pl.BlockSpec((T_ALL, LANES), lambda b, p: (b, 28)),
            pl.BlockSpec((1, LANES, 2 * LANES), lambda b, p: (p, 0, 0)),
            pl.BlockSpec((1, 1, 2 * LANES), lambda b, p: (p, 0, 0)),
            pl.BlockSpec((1, 2 * DV_B), lambda b, p: (0, p)),
        ],
        out_specs=pl.BlockSpec((T_ALL, 2 * DV_B), lambda b, p: (b, p)),
        out_shape=jax.ShapeDtypeStruct((N_ALL, H_B * DV_B), F32),
        scratch_shapes=[pltpu.VMEM((T_ALL, LANES), BF16)] * 6
                       + [pltpu.VMEM((T_ALL // CHUNK_B, 8, LANES), F32)] * 2
                       + [pltpu.VMEM((T_ALL // CHUNK_B, 2, DV_B, LANES), F32)],
        compiler_params=_cparams(("arbitrary", "arbitrary")),
        name="scan_gla",
    )(z, z, z, z, z, wlr, blr, gn_b)


def _attn_kernel(lam_ref, q_ref, k_ref, v_ref, gsub_ref, o_ref, k_scr, v_scr, *, post_scale, tile0):
    h = pl.program_id(1)
    t = pl.program_id(2)
    scale = DH_C ** -0.5 * math.log2(math.e)

    @pl.when(t == 0)
    def _():
        k_scr[...] = k_ref[...].astype(BF16)
        v_scr[:, 0:LANES] = v_ref[...].astype(BF16)
        v_scr[:, LANES:] = jnp.ones((T_ALL, LANES), BF16)

    lam = lam_ref[h]
    lane = lax.broadcasted_iota(jnp.int32, (1, LANES), 1)
    m1 = (lane < DH_C).astype(F32)
    m2 = (lane >= DH_C).astype(F32)

    def attend(n_keys):
        q = q_ref[...] * scale
        qs = [(q * m1).astype(BF16), (q * m2).astype(BF16)]
        run_max = [jnp.full((TQ, 1), -jnp.inf, F32) for _ in range(2)]
        acc = [jnp.zeros((TQ, 2 * LANES), F32) for _ in range(2)]
        chunk = min(ATTN_KEYS, n_keys)
        for c in range(n_keys // chunk):
            keys = slice(c * chunk, (c + 1) * chunk)
            kb = k_scr[keys, :]
            vb = v_scr[keys, :]
            for i in range(2):
                s = _dot_tb(qs[i], kb)
                new_max = jnp.maximum(run_max[i], jnp.max(s, axis=-1, keepdims=True))
                p = jnp.exp2(s - new_max).astype(BF16)
                acc[i] = acc[i] * jnp.exp2(run_max[i] - new_max) + jnp.dot(p, vb, preferred_element_type=F32)
                run_max[i] = new_max
        o = acc[0][:, :LANES] / acc[0][:, LANES:] - lam * (acc[1][:, :LANES] / acc[1][:, LANES:])
        o = o * lax.rsqrt(jnp.mean(o * o, axis=-1, keepdims=True) + LN_EPS)
        o_ref[...] = o * (gsub_ref[...] * post_scale)

    if tile0 == 0:
        pl.when(t == 0)(lambda: attend(CTX_LEN))
        pl.when(t > 0)(lambda: attend(T_ALL))
    else:
        attend(T_ALL)


def _attention(z, lam, gsub, post_scale, latent_only):
    tile0 = 1 if latent_only else 0
    n_qt = TILES_PER_BATCH - tile0
    kern = functools.partial(_attn_kernel, post_scale=post_scale, tile0=tile0)
    kv = lambda col0: pl.BlockSpec((T_ALL, LANES), lambda b, h, t, lam_r: (b, col0 + h))
    return pl.pallas_call(
        kern,
        grid_spec=pltpu.PrefetchScalarGridSpec(
            num_scalar_prefetch=1,
            grid=(BATCH, H_C, n_qt),
            in_specs=[pl.BlockSpec((TQ, LANES), lambda b, h, t, lam_r: (b * TILES_PER_BATCH + tile0 + t, h)),
                      kv(H_C), kv(2 * H_C),
                      pl.BlockSpec((1, LANES), lambda b, h, t, lam_r: (0, h))],
            out_specs=pl.BlockSpec((TQ, LANES), lambda b, h, t, lam_r: (b * n_qt + t, h)),
            scratch_shapes=[pltpu.VMEM((T_ALL, LANES), BF16), pltpu.VMEM((T_ALL, 2 * LANES), BF16)],
        ),
        out_shape=jax.ShapeDtypeStruct((BATCH * n_qt * TQ, H_C * DV_C), F32),
        compiler_params=_cparams(("arbitrary", "arbitrary", "arbitrary")),
        name="diff_attention",
    )(lam, z, z, z, gsub)


def _layer_norm(r, g, b):
    mu = jnp.mean(r, axis=-1, keepdims=True)
    d = r - mu
    var = jnp.mean(d * d, axis=-1, keepdims=True)
    return d * lax.rsqrt(var + LN_EPS) * g + b


def _route_tile(u, w_ref, b_ref, mi_ref, mf_ref, cnt_ref, carry):
    @pl.when(pl.program_id(0) == 0)
    def _():
        carry[...] = jnp.zeros_like(carry)

    logits = _dot_split(u, w_ref[...]) + b_ref[...]
    lane = lax.broadcasted_iota(jnp.int32, (TM, LANES), 1)
    lane_f = lane.astype(F32)
    neg = -jnp.inf
    big = 1e9

    gmask = lane < N_GROUPS
    gl = jnp.where(gmask, logits, neg)
    gmax = jnp.max(gl, axis=-1, keepdims=True)
    gidx = jnp.min(jnp.where(gl == gmax, lane_f, big), axis=-1, keepdims=True)
    gw = 1.0 / jnp.sum(jnp.where(gmask, jnp.exp(logits - gmax), 0.0), axis=-1, keepdims=True)

    e_lane = lane - N_GROUPS
    in_grp = (e_lane >= 0) & (e_lane < N_EXPERTS) & ((e_lane >> 3) == gidx.astype(jnp.int32))
    el = jnp.where(in_grp, logits, neg)
    v1 = jnp.max(el, axis=-1, keepdims=True)
    i1 = jnp.min(jnp.where(el == v1, lane_f, big), axis=-1, keepdims=True)
    el2 = jnp.where(lane_f == i1, neg, el)
    v2 = jnp.max(el2, axis=-1, keepdims=True)
    i2 = jnp.min(jnp.where(el2 == v2, lane_f, big), axis=-1, keepdims=True)
    t = jnp.exp(v2 - v1)
    c0 = gw / (1.0 + t)
    c1 = gw * t / (1.0 + t)
    e0 = i1 - N_GROUPS
    e1 = i2 - N_GROUPS

    oh0 = lane_f == e0
    oh1 = lane_f == e1
    cnt = oh0.astype(F32) + oh1.astype(F32)
    ri = lax.broadcasted_iota(jnp.int32, (TM, TM), 0)
    cj = lax.broadcasted_iota(jnp.int32, (TM, TM), 1)
    strict = (cj < ri).astype(BF16)
    before = jnp.dot(strict, cnt.astype(BF16), preferred_element_type=F32) + carry[0:1, :]
    r0 = jnp.sum(jnp.where(oh0, before, 0.0), axis=-1, keepdims=True)
    r1 = jnp.sum(jnp.where(oh1, before, 0.0), axis=-1, keepdims=True)
    carry[0:1, :] = carry[0:1, :] + jnp.sum(cnt, axis=0, keepdims=True)

    mi = jnp.where(lane == 0, e0, jnp.where(lane == 1, e1, jnp.where(lane == 2, r0, jnp.where(lane == 3, r1, 0.0))))
    mi_ref[...] = mi.astype(jnp.int32)
    mf_ref[...] = jnp.where(lane == 0, c0, jnp.where(lane == 1, c1, 0.0))
    cnt_ref[...] = carry[...]


def _proj_ln_kernel(y1_ref, y2_ref, w_ref, x_ref, mod_ref, g_ref, b_ref, wr_ref, br_ref,
                    o_ref, mi_ref, mf_ref, cnt_ref, carry):
    heads = jnp.concatenate([y1_ref[...], y2_ref[...]], axis=1).astype(BF16)
    y = jnp.dot(heads, w_ref[...], preferred_element_type=F32)
    r = DEEPNORM_ALPHA * x_ref[...] + mod_ref[0, 2:3, :] * y
    x1 = _layer_norm(r, g_ref[...], b_ref[...])
    o_ref[...] = x1
    u = x1 * (1.0 + mod_ref[0, 4:5, :]) + mod_ref[0, 3:4, :]
    _route_tile(u, wr_ref, br_ref, mi_ref, mf_ref, cnt_ref, carry)


def _proj_ln(y1, y2, col2, w_out_bf16, x, mod_l, ln_g, ln_b, w_route, b_route, latent_only):
    half = D_MODEL // 2
    rt = _row_tile(latent_only)
    mr = _mod_row(latent_only)
    n_tiles = _n_tiles(latent_only)
    n_tok = n_tiles * TM
    row_blk = lambda w: pl.BlockSpec((TM, w), lambda i: (i, 0))
    const = lambda shape: pl.BlockSpec(shape, lambda i: (0, 0))
    return pl.pallas_call(
        _proj_ln_kernel,
        grid=(n_tiles,),
        in_specs=[
            pl.BlockSpec((TM, half), lambda i: (i, 0)),
            pl.BlockSpec((TM, half), lambda i: (i, col2)),
            const((D_MODEL, D_MODEL)),
            pl.BlockSpec((TM, D_MODEL), lambda i: (rt(i), 0)),
            pl.BlockSpec((1, 6, D_MODEL), lambda i: (mr(i), 0, 0)),
            const((1, D_MODEL)), const((1, D_MODEL)), const((D_MODEL, LANES)), const((1, LANES)),
        ],
        out_specs=[row_blk(D_MODEL), row_blk(LANES), row_blk(LANES), const((8, LANES))],
        out_shape=[jax.ShapeDtypeStruct((n_tok, D_MODEL), F32),
                   jax.ShapeDtypeStruct((n_tok, LANES), jnp.int32),
                   jax.ShapeDtypeStruct((n_tok, LANES), F32),
                   jax.ShapeDtypeStruct((8, LANES), F32)],
        scratch_shapes=[pltpu.VMEM((8, LANES), F32)],
        compiler_params=_cparams(("arbitrary",)),
        name="proj_ln_route",
    )(y1, y2, w_out_bf16, x, mod_l, ln_g, ln_b, w_route, b_route)


def _pack_rows(x):
    half = x.shape[-1] // 2
    bits = lambda t: lax.bitcast_convert_type(t.astype(BF16).astype(F32), jnp.uint32)
    return (bits(x[:, :half]) >> 16) | (bits(x[:, half:]) & jnp.uint32(0xFFFF0000))


def _unpack_rows(w):
    lo = lax.bitcast_convert_type(w << 16, F32)
    hi = lax.bitcast_convert_type(w & jnp.uint32(0xFFFF0000), F32)
    return jnp.concatenate([lo, hi], axis=-1)


def _each_row(fn):
    def body(g, carry):
        for j in range(ROW_UNROLL):
            for k in range(2):
                fn(g, j, k)
        return carry
    lax.fori_loop(0, TM // ROW_UNROLL, body, 0)


def _dispatch_kernel(pad_end_ref, padded_ref, x_ref, mod_ref, dest_hbm, xs_hbm,
                     idx_smem, u_scr, zero_scr, sem_idx, sem_row, sem_zero):
    i = pl.program_id(0)
    slot = i % 2
    per_tile = 2 * TM

    @pl.when(i == 0)
    def _():
        zero_scr[...] = jnp.zeros_like(zero_scr)

        def zero_block(first_row):
            rows = pl.ds(pl.multiple_of(first_row, MOE_ROWS), MOE_ROWS)
            return pltpu.make_async_copy(zero_scr, xs_hbm.at[rows], sem_zero)

        n_rows = xs_hbm.shape[0]
        total = pad_end_ref[N_EXPERTS - 1]
        for e in range(N_EXPERTS):
            pl.when(padded_ref[e] > 0)(lambda e=e: zero_block(pad_end_ref[e] - MOE_ROWS).start())
            pl.when(total + e * MOE_ROWS < n_rows)(lambda e=e: zero_block(total + e * MOE_ROWS).start())
        for e in range(N_EXPERTS):
            pl.when(padded_ref[e] > 0)(lambda e=e: zero_block(0).wait())
            pl.when(total + e * MOE_ROWS < n_rows)(lambda e=e: zero_block(0).wait())

    def idx_copy(tile, sl):
        return pltpu.make_async_copy(dest_hbm.at[pl.ds(tile * per_tile, per_tile)],
                                     idx_smem.at[pl.ds(sl * per_tile, per_tile)], sem_idx)

    pl.when(i == 0)(lambda: idx_copy(0, 0).start())
    idx_copy(i, slot).wait()
    pl.when(i + 1 < pl.num_programs(0))(lambda: idx_copy(i + 1, 1 - slot).start())
    u = x_ref[...] * (1.0 + mod_ref[0, 4:5, :]) + mod_ref[0, 3:4, :]
    u_scr[slot] = _pack_rows(u).reshape(TM // ROW_UNROLL, ROW_UNROLL, PACKED)

    def row_copy(sl, g, j, dst_row):
        return pltpu.make_async_copy(u_scr.at[sl, g, pl.ds(j, 1)], xs_hbm.at[pl.ds(dst_row, 1)], sem_row.at[sl])

    def dest_of(sl, g, j, k):
        return idx_smem[sl * per_tile + g * (2 * ROW_UNROLL) + (2 * j + k)]

    _each_row(lambda g, j, k: row_copy(slot, g, j, dest_of(slot, g, j, k)).start())

    @pl.when(i > 0)
    def _():
        _each_row(lambda g, j, k: row_copy(1 - slot, g, j, 0).wait())

    @pl.when(i == pl.num_programs(0) - 1)
    def _():
        _each_row(lambda g, j, k: row_copy(slot, g, j, 0).wait())


def _dispatch(pad_end, padded, x1, mod_l, dest, n_blocks, latent_only):
    mr = _mod_row(latent_only)
    return pl.pallas_call(
        _dispatch_kernel,
        grid_spec=pltpu.PrefetchScalarGridSpec(
            num_scalar_prefetch=2,
            grid=(_n_tiles(latent_only),),
            in_specs=[
                pl.BlockSpec((TM, D_MODEL), lambda i, pe, pd: (i, 0)),
                pl.BlockSpec((1, 6, D_MODEL), lambda i, pe, pd: (mr(i), 0, 0)),
                pl.BlockSpec(memory_space=pl.ANY),
            ],
            out_specs=pl.BlockSpec(memory_space=pl.ANY),
            scratch_shapes=[
                pltpu.SMEM((2 * 2 * TM,), jnp.int32),
                pltpu.VMEM((2, TM // ROW_UNROLL, ROW_UNROLL, PACKED), jnp.uint32),
                pltpu.VMEM((MOE_ROWS, PACKED), jnp.uint32),
                pltpu.SemaphoreType.DMA(()),
                pltpu.SemaphoreType.DMA((2,)),
                pltpu.SemaphoreType.DMA(()),
            ],
        ),
        out_shape=jax.ShapeDtypeStruct((n_blocks * MOE_ROWS, PACKED), jnp.uint32),
        compiler_params=_cparams(("arbitrary",)),
        name="moe_dispatch",
    )(pad_end, padded, x1, mod_l, dest)


def _expert_kernel(blk_exp_ref, n_used_ref, x_ref, wg_ref, wu_ref, wd_ref, o_ref, wgu_scr, wd_scr):
    i = pl.program_id(0)
    used = i < n_used_ref[0]
    changed = jnp.logical_or(i == 0, blk_exp_ref[i] != blk_exp_ref[jnp.maximum(i - 1, 0)])

    @pl.when(jnp.logical_and(used, changed))
    def _():
        wgu_scr[:, 0:D_EXPERT] = wg_ref[0, 0].astype(BF16)
        wgu_scr[:, D_EXPERT:] = wu_ref[0, 0].astype(BF16)
        wd_scr[...] = wd_ref[0, 0].astype(BF16)

    @pl.when(used)
    def _():
        x = _unpack_rows(x_ref[...]).astype(BF16)
        gate_up = jnp.dot(x, wgu_scr[...], preferred_element_type=F32)
        hid = (_silu(gate_up[:, :D_EXPERT]) * gate_up[:, D_EXPERT:]).astype(BF16)
        o_ref[...] = _pack_rows(jnp.dot(hid, wd_scr[...], preferred_element_type=F32))

    @pl.when(jnp.logical_not(used))
    def _():
        o_ref[...] = jnp.zeros_like(o_ref)


def _experts(blk_exp, n_used, xs, layer, w_gate, w_up, w_down):
    n_blocks = xs.shape[0] // MOE_ROWS
    row_in = pl.BlockSpec((MOE_ROWS, PACKED), lambda i, be, nu: (jnp.minimum(i, nu[0] - 1), 0))
    w_in = pl.BlockSpec((1, 1, D_MODEL, D_EXPERT), lambda i, be, nu: (layer, be[i], 0, 0))
    w_out = pl.BlockSpec((1, 1, D_EXPERT, D_MODEL), lambda i, be, nu: (layer, be[i], 0, 0))
    return pl.pallas_call(
        _expert_kernel,
        grid_spec=pltpu.PrefetchScalarGridSpec(
            num_scalar_prefetch=2,
            grid=(n_blocks,),
            in_specs=[row_in, w_in, w_in, w_out],
            out_specs=pl.BlockSpec((MOE_ROWS, PACKED), lambda i, be, nu: (i, 0)),
            scratch_shapes=[pltpu.VMEM((D_MODEL, 2 * D_EXPERT), BF16), pltpu.VMEM((D_EXPERT, D_MODEL), BF16)],
        ),
        out_shape=jax.ShapeDtypeStruct(xs.shape, jnp.uint32),
        compiler_params=_cparams(("arbitrary",)),
        name="moe_experts",
    )(blk_exp, n_used, xs, w_gate, w_up, w_down)


def _combine_ln_kernel(ys_hbm, dest_hbm, mf_ref, x_ref, mod_ref, g_ref, b_ref, o_ref,
                       idx_smem, y_buf, sem_idx, sem_row):
    i = pl.program_id(0)
    slot = i % 2
    per_tile = 2 * TM

    def row_copy(sl, g, j, k, src_row):
        return pltpu.make_async_copy(ys_hbm.at[pl.ds(src_row, 1)], y_buf.at[sl, k, g, pl.ds(j, 1)], sem_row.at[sl])

    def idx_copy(tile, sl):
        return pltpu.make_async_copy(dest_hbm.at[pl.ds(tile * per_tile, per_tile)],
                                     idx_smem.at[pl.ds(sl * per_tile, per_tile)], sem_idx)

    def request(sl):
        _each_row(lambda g, j, k: row_copy(
            sl, g, j, k, idx_smem[sl * per_tile + g * (2 * ROW_UNROLL) + (2 * j + k)]).start())

    n = pl.num_programs(0)

    @pl.when(i == 0)
    def _():
        idx_copy(0, 0).start()
        idx_copy(0, 0).wait()
        request(0)
        pl.when(n > 1)(lambda: idx_copy(1, 1).start())

    @pl.when(i + 1 < n)
    def _():
        idx_copy(i + 1, 1 - slot).wait()
        request(1 - slot)
        pl.when(i + 2 < n)(lambda: idx_copy(i + 2, slot).start())

    _each_row(lambda g, j, k: row_copy(slot, g, j, k, 0).wait())

    mf = mf_ref[...]
    y0 = _unpack_rows(y_buf[slot, 0].reshape(TM, PACKED))
    y1 = _unpack_rows(y_buf[slot, 1].reshape(TM, PACKED))
    y = mf[:, 0:1] * y0 + mf[:, 1:2] * y1
    r = DEEPNORM_ALPHA * x_ref[...] + mod_ref[0, 5:6, :] * y
    o_ref[...] = _layer_norm(r, g_ref[...], b_ref[...])


def _combine_ln(ys, dest, mf, x1, mod_l, ln_g, ln_b, latent_only):
    mr = _mod_row(latent_only)
    n_tiles = _n_tiles(latent_only)
    return pl.pallas_call(
        _combine_ln_kernel,
        grid=(n_tiles,),
        in_specs=[
            pl.BlockSpec(memory_space=pl.ANY),
            pl.BlockSpec(memory_space=pl.ANY),
            pl.BlockSpec((TM, LANES), lambda i: (i, 0)),
            pl.BlockSpec((TM, D_MODEL), lambda i: (i, 0)),
            pl.BlockSpec((1, 6, D_MODEL), lambda i: (mr(i), 0, 0)),
            pl.BlockSpec((1, D_MODEL), lambda i: (0, 0)),
            pl.BlockSpec((1, D_MODEL), lambda i: (0, 0)),
        ],
        out_specs=pl.BlockSpec((TM, D_MODEL), lambda i: (i, 0)),
        out_shape=jax.ShapeDtypeStruct((n_tiles * TM, D_MODEL), F32),
        scratch_shapes=[
            pltpu.SMEM((2 * 2 * TM,), jnp.int32),
            pltpu.VMEM((2, 2, TM // ROW_UNROLL, ROW_UNROLL, PACKED), jnp.uint32),
            pltpu.SemaphoreType.DMA(()),
            pltpu.SemaphoreType.DMA((2,)),
        ],
        compiler_params=_cparams(("arbitrary",)),
        name="combine_ln",
    )(ys, dest, mf, x1, mod_l, ln_g, ln_b)


def _route_params(w_grp, b_grp, w_rexp, b_rexp):
    pad = LANES - N_GROUPS - N_EXPERTS
    w_route = jnp.concatenate([w_grp, w_rexp, jnp.zeros((D_MODEL, pad), F32)], axis=1)
    b_route = jnp.concatenate([b_grp, b_rexp, jnp.zeros((pad,), F32)])[None, :]
    return w_route, b_route


def _moe(x1, mi, mf, cnt, mod_l, layer, w_gate, w_up, w_down, ln_g, ln_b, latent_only):
    counts = cnt[0, :N_EXPERTS].astype(jnp.int32)
    padded = (counts + MOE_ROWS - 1) // MOE_ROWS * MOE_ROWS
    pad_end = jnp.cumsum(padded)
    pad_start = pad_end - padded
    n_tok = x1.shape[0]
    n_blocks = (2 * n_tok) // MOE_ROWS + N_EXPERTS
    experts = jnp.arange(N_EXPERTS, dtype=jnp.int32)
    start_of = jnp.sum(jnp.where(mi[:, 0:2, None] == experts, pad_start, 0), axis=-1)
    dest = (start_of + mi[:, 2:4]).reshape(-1)
    blk_start = jnp.arange(n_blocks, dtype=jnp.int32) * MOE_ROWS
    blk_exp = jnp.minimum(jnp.sum((pad_end[None, :] <= blk_start[:, None]).astype(jnp.int32), axis=1),
                          N_EXPERTS - 1)
    n_used = pad_end[-1:] // MOE_ROWS
    xs = _dispatch(pad_end, padded, x1, mod_l, dest, n_blocks, latent_only)
    ys = _experts(blk_exp, n_used, xs, layer, w_gate, w_up, w_down)
    return _combine_ln(ys, dest, mf, x1, mod_l, ln_g, ln_b, latent_only)


def kernel(x, c, ctx, c_ctx, ada_w, ada_b, ln1_g, ln1_b, ln2_g, ln2_b, ab_w_in, ab_w_lr_f, ab_b_lr_f, ab_w_lr_b, ab_b_lr_b, ab_gn_a, ab_gn_b, ab_w_out, c_w_qkv, c_lq1, c_lk1, c_lq2, c_lk2, c_subln_g, c_w_out, moe_w_grp, moe_b_grp, moe_w_rexp, moe_b_rexp, moe_w_gate, moe_w_up, moe_w_down):
    assert x.shape == (BATCH, SEQ, D_MODEL) and ctx.shape == (BATCH, CTX_LEN, D_MODEL)
    xs = jnp.concatenate([ctx, x], axis=1).reshape(N_ALL, D_MODEL)
    c_all = jnp.concatenate([c, c_ctx[None, :], jnp.zeros((16 - BATCH - 1, D_MODEL), F32)], axis=0)
    mod = _ada_tables(c_all, ada_w, ada_b).reshape(DEPTH, 16, 6, D_MODEL)

    rope_a = _rope_tables(DK_A, 1)
    rope_c = _rope_tables(DH_C, 2)
    dec_a = _retention_tables()

    for l in range(DEPTH):
        last = l == DEPTH - 1
        i = l // 2
        mod_l = mod[l]
        row = lambda v: v[None, :]
        route = _route_params(moe_w_grp[l], moe_b_grp[l], moe_w_rexp[l], moe_b_rexp[l])
        if l % 2 == 0:
            assert not last
            w_in = jnp.pad(ab_w_in[i], ((0, 0), (0, AB_COLS - ab_w_in.shape[2]))).astype(BF16)
            z = _mod_matmul(xs, mod_l, w_in, *rope_a, 2 * H_A, DK_A // 4)
            ya = _scan_a(z, dec_a, row(ab_gn_a[i]))
            wf = ab_w_lr_f[i].reshape(GLA_RANK, H_B // 2, LANES)
            wb = ab_w_lr_b[i].reshape(GLA_RANK, H_B // 2, LANES)
            wlr = jnp.zeros((H_B // 2, LANES, 2 * LANES), F32)
            wlr = wlr.at[:, 0:GLA_RANK, 0:LANES].set(jnp.swapaxes(wf, 0, 1))
            wlr = wlr.at[:, GLA_RANK:2 * GLA_RANK, LANES:].set(jnp.swapaxes(wb, 0, 1))
            blr = jnp.concatenate([ab_b_lr_f[i].reshape(H_B // 2, 1, LANES),
                                   ab_b_lr_b[i].reshape(H_B // 2, 1, LANES)], axis=-1)
            yb = _scan_b(z, wlr, blr, row(ab_gn_b[i]))
            x1, mi, mf, cnt = _proj_ln(ya, yb, 0, ab_w_out[i].astype(BF16), xs, mod_l, row(ln1_g[l]),
                                       row(ln1_b[l]), *route, False)
        else:
            lam_init = 0.8 - 0.6 * math.exp(-0.3 * l)
            lam = (jnp.exp(jnp.sum(c_lq1[i] * c_lk1[i], axis=-1))
                   - jnp.exp(jnp.sum(c_lq2[i] * c_lk2[i], axis=-1))).astype(F32) + lam_init
            z = _mod_matmul(xs, mod_l, c_w_qkv[i].astype(BF16), *rope_c, 2 * H_C, DH_C // 4)
            gsub = row(c_subln_g[i])
            y = _attention(z, lam, gsub, 1.0 - lam_init, last)
            x1, mi, mf, cnt = _proj_ln(y, y, 1, c_w_out[i].astype(BF16), xs, mod_l, row(ln1_g[l]),
                                       row(ln1_b[l]), *route, last)
        xs = _moe(x1, mi, mf, cnt, mod_l, l, moe_w_gate, moe_w_up, moe_w_down, row(ln2_g[l]), row(ln2_b[l]), last)
    return xs.reshape(BATCH, SEQ, D_MODEL)
```
